```python
import jax, jax.numpy as jnp
from jax import lax
import numpy as np

D_MODEL = 1024
BATCH = 32
SEQ = 2048
DEPTH = 1

ATTN_HEAD_DIM = 64
N_Q_HEADS = D_MODEL // 128
N_KV_HEADS = 2
ATTN_WIDTH = N_Q_HEADS * ATTN_HEAD_DIM
KV_WIDTH = N_KV_HEADS * ATTN_HEAD_DIM
WINDOW = 128
ROPE_THETA = 10000.0
HGRN_EXPAND = 128
HGRN_HEADS = (D_MODEL // 2) // HGRN_EXPAND
HGRN_WIDTH = HGRN_HEADS * HGRN_EXPAND
HGRN_HEAD_V = 128
HGRN_V_WIDTH = HGRN_HEADS * HGRN_HEAD_V
HGRN_CHUNK = 64
SPLITS = [ATTN_WIDTH, KV_WIDTH, KV_WIDTH, HGRN_WIDTH, HGRN_WIDTH, HGRN_V_WIDTH, HGRN_V_WIDTH, D_MODEL, D_MODEL]
IN_COLS = sum(SPLITS)
SPLIT_IDX = list(np.cumsum(SPLITS)[:-1])
N_EXPERTS = 32
TOP_K = 4
D_EXPERT = D_MODEL
SWIGLU_ALPHA = 1.702
SWIGLU_LIMIT = 7.0
MOE_BLOCK = 512
NORM_EPS = 1e-5

kernel_name = 'hybrid_swa_hgrn2_moe_block'


def rms_norm(x, g):
    x32 = x.astype(jnp.float32)
    y = x32 * lax.rsqrt(jnp.mean(x32 * x32, axis=-1, keepdims=True) + NORM_EPS)
    return (y * g.astype(jnp.float32)).astype(x.dtype)


def rotary(x, positions):
    hd = x.shape[-1]
    half = hd // 2
    inv_freq = ROPE_THETA ** (-(jnp.arange(half, dtype=jnp.float32) * 2.0 / hd))
    ang = positions.astype(jnp.float32)[:, None] * inv_freq[None, :]
    cos = jnp.cos(ang)[None, :, None, :]
    sin = jnp.sin(ang)[None, :, None, :]
    x32 = x.astype(jnp.float32)
    x1, x2 = x32[..., :half], x32[..., half:]
    out = jnp.concatenate([x1 * cos - x2 * sin, x2 * cos + x1 * sin], axis=-1)
    return out.astype(x.dtype)


def sliding_window_attention(q, k, v, sinks):
    B, S = q.shape[0], q.shape[1]
    nB = S // WINDOW
    G = N_Q_HEADS // N_KV_HEADS
    hd = ATTN_HEAD_DIM
    qb = q.reshape(B, nB, WINDOW, N_KV_HEADS, G, hd)

    def band(t):
        tp = jnp.pad(t, ((0, 0), (WINDOW, 0), (0, 0), (0, 0))).reshape(B, nB + 1, WINDOW, N_KV_HEADS, hd)
        return jnp.concatenate([tp[:, :-1], tp[:, 1:]], axis=2)

    kb, vb = band(k), band(v)
    s = jnp.einsum('bnqhgd,bnkhd->bnhgqk', qb, kb, preferred_element_type=jnp.float32) * (hd ** -0.5)
    qi = jnp.arange(WINDOW)[:, None]
    kj = jnp.arange(2 * WINDOW)[None, :]
    band_mask = (kj > qi) & (kj <= qi + WINDOW)
    first_ok = (jnp.arange(nB)[:, None, None] > 0) | (kj[None] >= WINDOW)
    mask = (band_mask[None] & first_ok)[None, :, None, None]
    s = jnp.where(mask, s, -jnp.inf)
    sink = sinks.astype(jnp.float32).reshape(N_KV_HEADS, G)[None, None, :, :, None]
    m = jnp.maximum(jnp.max(s, axis=-1), sink)
    p = jnp.exp(s - m[..., None])
    denom = jnp.sum(p, axis=-1) + jnp.exp(sink - m)
    p = p / denom[..., None]
    o = jnp.einsum('bnhgqk,bnkhd->bnqhgd', p.astype(v.dtype), vb)
    return o.reshape(B, S, ATTN_WIDTH)


def hgrn2_recurrence(q, k, v, log_f):
    B, S, H, dk = q.shape
    dv = v.shape[-1]
    C = HGRN_CHUNK
    nC = S // C

    def chunks(t):
        return t.reshape(B, nC, C, H, t.shape[-1]).transpose(1, 0, 3, 2, 4)

    causal = jnp.tril(jnp.ones((C, C), dtype=bool))[None, None, :, :, None]

    def step(state, inp):
        qc, kc, vc, gc = inp
        Gc = jnp.cumsum(gc, axis=2)
        inter = jnp.einsum('bhtk,bhkv->bhtv', qc * jnp.exp(Gc), state)
        rel = jnp.where(causal, Gc[:, :, :, None, :] - Gc[:, :, None, :, :], -jnp.inf)
        attn = jnp.einsum('bhtk,bhsk,bhtsk->bhts', qc, kc, jnp.exp(rel))
        intra = jnp.einsum('bhts,bhsv->bhtv', attn, vc)
        G_last = Gc[:, :, -1:, :]
        new_state = jnp.exp(G_last[:, :, 0, :])[..., None] * state + jnp.einsum(
            'bhsk,bhsv->bhkv', kc * jnp.exp(G_last - Gc), vc)
        return new_state, inter + intra

    state0 = jnp.zeros((B, H, dk, dv), jnp.float32)
    _, o = lax.scan(step, state0, (chunks(q), chunks(k), chunks(v), chunks(log_f)))
    return o.transpose(1, 0, 3, 2, 4).reshape(B, S, H, dv)


def hybrid_mixer(u, w_in, b_in, sinks, lb, hg_norm_g, w_up_attn, w_up_hgrn, w_out):
    B, S, _ = u.shape
    z = u @ w_in + b_in
    zq, zk, zv, hq, hf, hi, hg, ga, gh = jnp.split(z, SPLIT_IDX, axis=-1)
    pos = jnp.arange(S)
    q = rotary(zq.reshape(B, S, N_Q_HEADS, ATTN_HEAD_DIM), pos)
    k = rotary(zk.reshape(B, S, N_KV_HEADS, ATTN_HEAD_DIM), pos)
    v = zv.reshape(B, S, N_KV_HEADS, ATTN_HEAD_DIM)
    y_attn = sliding_window_attention(q, k, v, sinks)
    f = lb + (1.0 - lb) * jax.nn.sigmoid(hf.astype(jnp.float32))
    kk = 1.0 - f
    log_f = jnp.log(f)
    shp_k = (B, S, HGRN_HEADS, HGRN_EXPAND)
    o = hgrn2_recurrence(hq.astype(jnp.float32).reshape(shp_k), kk.reshape(shp_k),
                         hi.astype(jnp.float32).reshape(B, S, HGRN_HEADS, HGRN_HEAD_V),
                         log_f.reshape(shp_k))
    o = o * lax.rsqrt(jnp.mean(o * o, axis=-1, keepdims=True) + NORM_EPS)
    o = o.reshape(B, S, HGRN_V_WIDTH) * hg_norm_g.astype(jnp.float32)
    y_hgrn = (o * jax.nn.silu(hg.astype(jnp.float32))).astype(u.dtype)
    merged = jax.nn.sigmoid(ga) * (y_attn @ w_up_attn) + jax.nn.sigmoid(gh) * (y_hgrn @ w_up_hgrn)
    return merged @ w_out


def moe_ffn(u, w_router, b_router, w1, b1, w2, b2):
    B, S, D = u.shape
    xt = u.reshape(B * S, D)
    N = B * S
    A = N * TOP_K
    logits = (xt @ w_router + b_router).astype(jnp.float32)
    top_val, top_idx = lax.top_k(logits, TOP_K)
    gates = jax.nn.softmax(top_val, axis=-1)
    e_flat = top_idx.reshape(-1).astype(jnp.int32)
    tok_flat = jnp.arange(A, dtype=jnp.int32) // TOP_K
    w_flat = gates.reshape(-1)
    order = jnp.argsort(e_flat)
    e_sorted = e_flat[order]
    counts = jnp.bincount(e_flat, length=N_EXPERTS)
    starts = jnp.cumsum(counts) - counts
    padded = ((counts + MOE_BLOCK - 1) // MOE_BLOCK) * MOE_BLOCK
    pad_ends = jnp.cumsum(padded)
    pad_starts = pad_ends - padded
    dest = pad_starts[e_sorted] + (jnp.arange(A, dtype=jnp.int32) - starts[e_sorted])
    n_blocks = -(-A // MOE_BLOCK) + N_EXPERTS
    P = n_blocks * MOE_BLOCK
    row_tok = jnp.zeros((P,), jnp.int32).at[dest].set(tok_flat[order])
    row_w = jnp.zeros((P,), jnp.float32).at[dest].set(w_flat[order])
    blk_e = jnp.minimum(jnp.searchsorted(pad_ends, jnp.arange(n_blocks) * MOE_BLOCK, side='right'),
                        N_EXPERTS - 1).astype(jnp.int32)

    def block(acc, inp):
        tok, w, e = inp
        h = xt[tok] @ w1[e] + b1[e]
        glu = jnp.minimum(h[:, :D_EXPERT], SWIGLU_LIMIT)
        lin = jnp.clip(h[:, D_EXPERT:], -SWIGLU_LIMIT, SWIGLU_LIMIT)
        act = glu * jax.nn.sigmoid(SWIGLU_ALPHA * glu) * (lin + 1.0)
        y = act @ w2[e] + b2[e]
        return acc.at[tok].add(y.astype(jnp.float32) * w[:, None]), None

    acc0 = jnp.zeros((N, D), jnp.float32)
    out, _ = lax.scan(block, acc0, (row_tok.reshape(n_blocks, MOE_BLOCK),
                                    row_w.reshape(n_blocks, MOE_BLOCK), blk_e))
    return out.astype(u.dtype).reshape(B, S, D)


def setup_inputs(seed: int = 0) -> dict:
    key = jax.random.key(seed)
    ks = jax.random.split(key, 20)
    f32 = jnp.float32

    def nrm(k, shape, scale):
        return jax.random.normal(k, shape, f32) * scale

    return {
        'x': nrm(ks[0], (BATCH, SEQ, D_MODEL), 1.0),
        'norm_mix_g': 1.0 + nrm(ks[1], (DEPTH, D_MODEL), 0.02),
        'w_in': nrm(ks[2], (DEPTH, D_MODEL, IN_COLS), D_MODEL ** -0.5),
        'b_in': nrm(ks[3], (DEPTH, IN_COLS), 0.02),
        'attn_sinks': nrm(ks[4], (DEPTH, N_Q_HEADS), 0.5),
        'hgrn_lb': nrm(ks[5], (DEPTH + 1, HGRN_WIDTH), 0.5),
        'hgrn_norm_g': 1.0 + nrm(ks[6], (DEPTH, HGRN_V_WIDTH), 0.02),
        'w_up_attn': nrm(ks[7], (DEPTH, ATTN_WIDTH, D_MODEL), ATTN_WIDTH ** -0.5),
        'w_up_hgrn': nrm(ks[8], (DEPTH, HGRN_V_WIDTH, D_MODEL), HGRN_V_WIDTH ** -0.5),
        'w_out': nrm(ks[9], (DEPTH, D_MODEL, D_MODEL), D_MODEL ** -0.5),
        'norm_ffn_g': 1.0 + nrm(ks[10], (DEPTH, D_MODEL), 0.02),
        'w_router': nrm(ks[11], (DEPTH, D_MODEL, N_EXPERTS), D_MODEL ** -0.5),
        'b_router': nrm(ks[12], (DEPTH, N_EXPERTS), 0.01),
        'w_moe1': nrm(ks[13], (DEPTH, N_EXPERTS, D_MODEL, 2 * D_EXPERT), D_MODEL ** -0.5),
        'b_moe1': nrm(ks[14], (DEPTH, N_EXPERTS, 2 * D_EXPERT), 0.02),
        'w_moe2': nrm(ks[15], (DEPTH, N_EXPERTS, D_EXPERT, D_MODEL), D_EXPERT ** -0.5),
        'b_moe2': nrm(ks[16], (DEPTH, N_EXPERTS, D_MODEL), 0.02),
        'norm_final_g': 1.0 + nrm(ks[17], (D_MODEL,), 0.02),
    }


def reference(x, norm_mix_g, w_in, b_in, attn_sinks, hgrn_lb, hgrn_norm_g, w_up_attn, w_up_hgrn,
              w_out, norm_ffn_g, w_router, b_router, w_moe1, b_moe1, w_moe2, b_moe2, norm_final_g):
    lb_all = jnp.cumsum(jax.nn.softmax(hgrn_lb.astype(jnp.float32), axis=0), axis=0)
    h = x
    for l in range(DEPTH):
        h = h + hybrid_mixer(rms_norm(h, norm_mix_g[l]), w_in[l], b_in[l], attn_sinks[l], lb_all[l],
                             hgrn_norm_g[l], w_up_attn[l], w_up_hgrn[l], w_out[l])
        h = h + moe_ffn(rms_norm(h, norm_ffn_g[l]), w_router[l], b_router[l], w_moe1[l], b_moe1[l],
                        w_moe2[l], b_moe2[l])
    return rms_norm(h, norm_final_g)
```

```python
import functools

import numpy as np
import jax
import jax.numpy as jnp
from jax import lax
from jax.experimental import pallas as pl
from jax.experimental.pallas import tpu as pltpu

D_MODEL = 1024
HEAD_DIM = 64
N_Q_HEADS = 8
N_KV_HEADS = 2
ATTN_WIDTH = N_Q_HEADS * HEAD_DIM
KV_WIDTH = N_KV_HEADS * HEAD_DIM
WINDOW = 128
ROPE_THETA = 10000.0
HGRN_HEADS = 4
HGRN_DK = 128
HGRN_WIDTH = HGRN_HEADS * HGRN_DK
N_EXPERTS = 32
TOP_K = 4
D_EXPERT = 1024
SWIGLU_ALPHA = 1.702
SWIGLU_LIMIT = 7.0
MOE_BLOCK = 512
NORM_EPS = 1e-5

_OFF_Q = 0
_OFF_K = _OFF_Q + ATTN_WIDTH
_OFF_V = _OFF_K + KV_WIDTH
_OFF_HQ = _OFF_V + KV_WIDTH
_OFF_HF = _OFF_HQ + HGRN_WIDTH
_OFF_HI = _OFF_HF + HGRN_WIDTH
_OFF_HG = _OFF_HI + HGRN_WIDTH
_OFF_GA = _OFF_HG + HGRN_WIDTH
_OFF_GH = _OFF_GA + D_MODEL
IN_COLS = _OFF_GH + D_MODEL

LANES = 128
SEQ_TILE = 256
HGRN_CHUNK = 64
HGRN_DIAG = 16
COMBINE_TILE = 256
DISPATCH_TILE = 256
VMEM_LIMIT = 56 * 1024 * 1024

_BF = jnp.bfloat16
_F32 = jnp.float32


def _dot(a, b):
    return jnp.dot(a, b, preferred_element_type=_F32)


def _dot_nt(a, b, precision=None):
    return lax.dot_general(a, b, (((1,), (1,)), ((), ())), precision=precision,
                           preferred_element_type=_F32)


def _dot_tn(a, b):
    return lax.dot_general(a, b, (((0,), (0,)), ((), ())), preferred_element_type=_F32)


def _rowsum(x):
    return jnp.sum(x, axis=1, keepdims=True)


def _rms(x, g):
    ms = _rowsum(x * x) * (1.0 / x.shape[1])
    return x * lax.rsqrt(ms + NORM_EPS) * g


def _pack_bf16_pairs(x):
    n = x.shape[1] // 2
    lo = lax.bitcast_convert_type(x[:, :n].astype(_BF).astype(_F32), jnp.uint32)
    hi = lax.bitcast_convert_type(x[:, n:].astype(_BF).astype(_F32), jnp.uint32)
    return (lo >> 16) | (hi & jnp.uint32(0xFFFF0000))


def _unpack_bf16_pairs(u):
    lo = lax.bitcast_convert_type(u << 16, _F32)
    hi = lax.bitcast_convert_type(u & jnp.uint32(0xFFFF0000), _F32)
    return jnp.concatenate([lo, hi], axis=1)


def _mixer_kernel(sinks_ref, x_ref, cos_ref, sin_ref, gmix_ref, w_in_ref, b_in_ref, lb_ref,
                  hgn_ref, w_ua_ref, w_uh_ref, w_out_ref, gffn_ref, w_rt_ref, b_r_ref,
                  h1_ref, xp_ref, idx_ref, rank_ref, gate_ref, cnt_ref,
                  kc_ref, vc_ref, st_ref, cnt_sc, ya_ref, o_ref):
    T = SEQ_TILE
    b = pl.program_id(0)
    s = pl.program_id(1)

    @pl.when(jnp.logical_and(b == 0, s == 0))
    def _():
        cnt_sc[...] = jnp.zeros_like(cnt_sc)

    @pl.when(s == 0)
    def _():
        kc_ref[...] = jnp.zeros_like(kc_ref)
        vc_ref[...] = jnp.zeros_like(vc_ref)
        st_ref[...] = jnp.zeros_like(st_ref)

    x = x_ref[0]
    xn = _rms(x, gmix_ref[...]).astype(_BF)

    def proj(off, width):
        return _dot(xn, w_in_ref[:, off:off + width]) + b_in_ref[:, off:off + width]

    cos = cos_ref[...]
    sin = sin_ref[...]

    def rope(t):
        return t * cos + pltpu.roll(t, 64, axis=1) * sin

    zq = proj(_OFF_Q, ATTN_WIDTH)
    scale = HEAD_DIM ** -0.5
    q_pairs = [(rope(zq[:, LANES * p:LANES * (p + 1)]) * scale).astype(_BF) for p in range(4)]
    k_rot = rope(proj(_OFF_K, KV_WIDTH))
    v_new = proj(_OFF_V, KV_WIDTH)

    k_ext = jnp.concatenate([kc_ref[...], k_rot], axis=0)
    v_ext = jnp.concatenate([vc_ref[...], v_new], axis=0)
    kc_ref[...] = k_rot[T - WINDOW:, :]
    vc_ref[...] = v_new[T - WINDOW:, :]

    lane = lax.broadcasted_iota(jnp.int32, (1, LANES), 1)
    slot_a = (lane % 64) < 32
    lane_lo = lane < 64
    k_r32 = pltpu.roll(k_ext, 32, axis=1)
    k_r96 = pltpu.roll(k_ext, 96, axis=1)
    v_r64 = pltpu.roll(v_ext, 64, axis=1)
    zero = jnp.zeros_like(k_ext)
    k_var = [(jnp.where(slot_a, k_ext, zero).astype(_BF), jnp.where(slot_a, zero, k_r32).astype(_BF)),
             (jnp.where(slot_a, k_r96, zero).astype(_BF), jnp.where(slot_a, zero, k_ext).astype(_BF))]
    v_var = [(jnp.where(lane_lo, v_ext, zero).astype(_BF), jnp.where(lane_lo, zero, v_r64).astype(_BF)),
             (jnp.where(lane_lo, v_r64, zero).astype(_BF), jnp.where(lane_lo, zero, v_ext).astype(_BF))]

    qi = lax.broadcasted_iota(jnp.int32, (2 * WINDOW, 2 * WINDOW), 0) % WINDOW
    kj = lax.broadcasted_iota(jnp.int32, (2 * WINDOW, 2 * WINDOW), 1)
    band = jnp.logical_and(kj > qi, kj <= qi + WINDOW)
    row_top = lax.broadcasted_iota(jnp.int32, (2 * WINDOW, 1), 0) < WINDOW
    neg_inf = jnp.float32(-jnp.inf)

    for n in range(T // WINDOW):
        if n == 0:
            ok = jnp.logical_and(band, jnp.logical_or(kj >= WINDOW, s > 0))
        else:
            ok = band
        r0 = n * WINDOW
        for j in range(N_KV_HEADS):
            q2 = jnp.concatenate([q_pairs[2 * j][r0:r0 + WINDOW], q_pairs[2 * j + 1][r0:r0 + WINDOW]], axis=0)
            kcat = jnp.concatenate([k_var[j][0][r0:r0 + 2 * WINDOW], k_var[j][1][r0:r0 + 2 * WINDOW]], axis=0)
            vcat = jnp.concatenate([v_var[j][0][r0:r0 + 2 * WINDOW], v_var[j][1][r0:r0 + 2 * WINDOW]], axis=0)
            sc = _dot_nt(q2, kcat)
            ps, rs = [], []
            for half in range(2):
                sh = jnp.where(ok, sc[:, 2 * WINDOW * half:2 * WINDOW * (half + 1)], neg_inf)
                snk = jnp.where(row_top, sinks_ref[4 * j + half], sinks_ref[4 * j + 2 + half])
                m = jnp.maximum(jnp.max(sh, axis=1, keepdims=True), snk)
                p = jnp.exp(sh - m)
                den = _rowsum(p) + jnp.exp(snk - m)
                ps.append(p.astype(_BF))
                rs.append(1.0 / den)
            o = _dot(jnp.concatenate(ps, axis=1), vcat)
            o = o * jnp.where(lane_lo, rs[0], rs[1])
            ya_ref[r0:r0 + WINDOW, LANES * (2 * j):LANES * (2 * j + 1)] = o[:WINDOW]
            ya_ref[r0:r0 + WINDOW, LANES * (2 * j + 1):LANES * (2 * j + 2)] = o[WINDOW:]

    HW = HGRN_WIDTH
    a0 = lb_ref[0:1, :]
    a1 = lb_ref[1:2, :]
    am = jnp.maximum(a0, a1)
    e0 = jnp.exp(a0 - am)
    lb = e0 / (e0 + jnp.exp(a1 - am))
    f = lb + (1.0 - lb) * jax.nn.sigmoid(proj(_OFF_HF, HW))
    kk = 1.0 - f
    g = jnp.log(f)
    hq = proj(_OFF_HQ, HW)
    hv = proj(_OFF_HI, HW)

    row = lax.broadcasted_iota(jnp.int32, (T, 1), 0)
    c0 = HGRN_DIAG

    def rdown(t, d):
        return pltpu.roll(t, d, axis=0)

    def rup(t, d):
        return pltpu.roll(t, T - d, axis=0)

    lcum = g
    d = 1
    while d < c0:
        lcum = lcum + jnp.where((row % c0) >= d, rdown(lcum, d), 0.0)
        d *= 2
    tot = jnp.where((row % c0) == c0 - 1, lcum, 0.0)
    d = 1
    while d < c0:
        tot = tot + rup(tot, d)
        d *= 2

    def head(t, h):
        return t[:, HGRN_DK * h:HGRN_DK * (h + 1)]

    qk0 = hq * kk
    o_parts = [_rowsum(head(qk0, h)) * head(hv, h) for h in range(HGRN_HEADS)]
    for d in range(1, c0):
        ks = rdown(kk, d)
        ls = rdown(lcum, d)
        vs = rdown(hv, d)
        pd = hq * ks * jnp.exp(jnp.minimum(lcum - ls, 0.0))
        okd = (row % c0) >= d
        for h in range(HGRN_HEADS):
            a = jnp.where(okd, _rowsum(head(pd, h)), 0.0)
            o_parts[h] = o_parts[h] + a * head(vs, h)

    C = HGRN_CHUNK
    levels = []
    bsz = c0
    lb_cum, lb_tot = lcum, tot
    while bsz < C:
        odd = (row % (2 * bsz)) >= bsz
        ex = jnp.exp(jnp.where(odd, lb_cum, lb_tot - lb_cum))
        ql = jnp.where(odd, hq * ex, 0.0).astype(_BF)
        kl = jnp.where(odd, 0.0, kk * ex).astype(_BF)
        levels.append((bsz, ql, kl))
        prev_tot = rdown(lb_tot, bsz)
        lb_cum = lb_cum + jnp.where(odd, prev_tot, 0.0)
        lb_tot = lb_tot + jnp.where(odd, prev_tot, rup(lb_tot, bsz))
        bsz *= 2
    q_in = (hq * jnp.exp(lb_cum)).astype(_BF)
    k_st = (kk * jnp.exp(lb_tot - lb_cum)).astype(_BF)
    hv_b = hv.astype(_BF)

    ti = lax.broadcasted_iota(jnp.int32, (C, C), 0)
    si = lax.broadcasted_iota(jnp.int32, (C, C), 1)
    lvl_masks = [jnp.logical_and(jnp.logical_and((ti // (2 * bs)) == (si // (2 * bs)), (ti % (2 * bs)) >= bs),
                                 (si % (2 * bs)) < bs) for bs, _, _ in levels]

    for c in range(T // C):
        rs_ = slice(c * C, (c + 1) * C)
        dec = jnp.exp(lb_tot[c * C:c * C + 1, :])
        for h in range(HGRN_HEADS):
            cs_ = slice(HGRN_DK * h, HGRN_DK * (h + 1))
            st = st_ref[h]
            vc = hv_b[rs_, cs_]
            inter = _dot_nt(q_in[rs_, cs_], st.astype(_BF))
            amat = jnp.zeros((C, C), _F32)
            for (bs, ql, kl), msk in zip(levels, lvl_masks):
                amat = amat + jnp.where(msk, _dot_nt(ql[rs_, cs_], kl[rs_, cs_]), 0.0)
            intra = _dot(amat.astype(_BF), vc)
            st_ref[h] = st * dec[:, cs_] + _dot_tn(vc, k_st[rs_, cs_])
            o_ref[rs_, cs_] = inter + intra + o_parts[h][rs_]

    o = o_ref[...]
    hg = proj(_OFF_HG, HW)
    yh_parts = []
    for h in range(HGRN_HEADS):
        oh = head(o, h)
        ms = _rowsum(oh * oh) * (1.0 / HGRN_DK)
        yh_parts.append(oh * lax.rsqrt(ms + NORM_EPS))
    yh = jnp.concatenate(yh_parts, axis=1) * hgn_ref[...]
    yh = (yh * (hg * jax.nn.sigmoid(hg))).astype(_BF)

    up_a = _dot(ya_ref[...].astype(_BF), w_ua_ref[...])
    up_h = _dot(yh, w_uh_ref[...])
    merged = jax.nn.sigmoid(proj(_OFF_GA, D_MODEL)) * up_a + jax.nn.sigmoid(proj(_OFF_GH, D_MODEL)) * up_h
    h1 = x + _dot(merged.astype(_BF), w_out_ref[...])
    h1_ref[0] = h1

    xn2 = _rms(h1, gffn_ref[...])
    xp_ref[0] = _pack_bf16_pairs(xn2)
    logits = _dot_nt(w_rt_ref[...], xn2, precision=lax.Precision.HIGHEST) + b_r_ref[...]
    eidx = lax.broadcasted_iota(jnp.int32, (N_EXPERTS, T), 0)
    vals, sels, ohs = [], [], []
    l = logits
    for _ in range(TOP_K):
        m = jnp.max(l, axis=0, keepdims=True)
        sel = jnp.min(jnp.where(l == m, eidx, N_EXPERTS), axis=0, keepdims=True)
        oh = eidx == sel
        vals.append(m)
        sels.append(sel)
        ohs.append(oh)
        l = jnp.where(oh, neg_inf, l)
    es = [jnp.exp(v - vals[0]) for v in vals]
    den = es[0] + es[1] + es[2] + es[3]
    chosen = jnp.zeros((N_EXPERTS, T), _F32)
    for oh in ohs:
        chosen = chosen + jnp.where(oh, 1.0, 0.0)
    ui = lax.broadcasted_iota(jnp.int32, (T, T), 0)
    uj = lax.broadcasted_iota(jnp.int32, (T, T), 1)
    upper = jnp.where(ui < uj, 1.0, 0.0).astype(_BF)
    before = _dot(chosen.astype(_BF), upper) + cnt_sc[:, 0:1]
    for k in range(TOP_K):
        idx_ref[k:k + 1, :] = sels[k]
        gate_ref[k:k + 1, :] = es[k] / den
        rank_ref[k:k + 1, :] = jnp.sum(jnp.where(ohs[k], before, 0.0), axis=0, keepdims=True).astype(jnp.int32)
    cnt_sc[...] = cnt_sc[...] + _rowsum(chosen)
    cnt_ref[...] = cnt_sc[...]


def _const_spec(shape):
    return pl.BlockSpec(shape, lambda b, s: (0,) * len(shape))


def _mixer_call(x, sinks, cos_t, sin_t, gmix, w_in, b_in, lb, hgn, w_ua, w_uh, w_out, gffn, w_rt, b_r):
    B, S, _ = x.shape
    T = SEQ_TILE
    N = B * S
    nS = S // T
    tok_spec = lambda w: pl.BlockSpec((4, T), lambda b, s: (0, b * nS + s))
    out_shape = (
        jax.ShapeDtypeStruct((B, S, D_MODEL), _F32),
        jax.ShapeDtypeStruct((B, S, D_MODEL // 2), jnp.uint32),
        jax.ShapeDtypeStruct((TOP_K, N), jnp.int32),
        jax.ShapeDtypeStruct((TOP_K, N), jnp.int32),
        jax.ShapeDtypeStruct((TOP_K, N), _F32),
        jax.ShapeDtypeStruct((N_EXPERTS, LANES), _F32),
    )
    in_specs = [
        pl.BlockSpec(memory_space=pltpu.SMEM),
        pl.BlockSpec((1, T, D_MODEL), lambda b, s: (b, s, 0)),
        pl.BlockSpec((T, LANES), lambda b, s: (s, 0)),
        pl.BlockSpec((T, LANES), lambda b, s: (s, 0)),
        _const_spec((1, D_MODEL)),
        _const_spec((D_MODEL, IN_COLS)),
        _const_spec((1, IN_COLS)),
        _const_spec((2, HGRN_WIDTH)),
        _const_spec((1, HGRN_WIDTH)),
        _const_spec((ATTN_WIDTH, D_MODEL)),
        _const_spec((HGRN_WIDTH, D_MODEL)),
        _const_spec((D_MODEL, D_MODEL)),
        _const_spec((1, D_MODEL)),
        _const_spec((N_EXPERTS, D_MODEL)),
        _const_spec((N_EXPERTS, 1)),
    ]
    out_specs = (
        pl.BlockSpec((1, T, D_MODEL), lambda b, s: (b, s, 0)),
        pl.BlockSpec((1, T, D_MODEL // 2), lambda b, s: (b, s, 0)),
        tok_spec(0), tok_spec(0), tok_spec(0),
        pl.BlockSpec((N_EXPERTS, LANES), lambda b, s: (0, 0)),
    )
    scratch = [
        pltpu.VMEM((WINDOW, KV_WIDTH), _F32),
        pltpu.VMEM((WINDOW, KV_WIDTH), _F32),
        pltpu.VMEM((HGRN_HEADS, HGRN_DK, HGRN_DK), _F32),
        pltpu.VMEM((N_EXPERTS, LANES), _F32),
        pltpu.VMEM((T, ATTN_WIDTH), _F32),
        pltpu.VMEM((T, HGRN_WIDTH), _F32),
    ]
    return pl.pallas_call(
        _mixer_kernel,
        grid=(B, nS),
        in_specs=in_specs,
        out_specs=out_specs,
        out_shape=out_shape,
        scratch_shapes=scratch,
        compiler_params=pltpu.CompilerParams(
            dimension_semantics=("arbitrary", "arbitrary"), vmem_limit_bytes=VMEM_LIMIT),
        name="mixer",
    )(sinks, x, cos_t, sin_t, gmix, w_in, b_in, lb, hgn, w_ua, w_uh, w_out, gffn, w_rt, b_r)


def _dispatch_kernel(dest_ref, xp_hbm, xs_in_hbm, xs_hbm, sem):
    del xs_in_hbm
    T = DISPATCH_TILE
    base = pl.program_id(0) * T

    def row_copy(t, k):
        return pltpu.make_async_copy(xp_hbm.at[pl.ds(base + t, 1)],
                                     xs_hbm.at[pl.ds(dest_ref[k, t], 1)], sem)

    def issue(t, carry):
        for k in range(TOP_K):
            row_copy(t, k).start()
        return carry

    lax.fori_loop(0, T, issue, 0)

    def drain(t, carry):
        for k in range(TOP_K):
            row_copy(t, k).wait()
        return carry

    lax.fori_loop(0, T, drain, 0)


def _dispatch_call(dest, xp, n_rows):
    N = xp.shape[0]
    T = DISPATCH_TILE
    xs0 = jnp.zeros((n_rows, xp.shape[1]), xp.dtype)
    return pl.pallas_call(
        _dispatch_kernel,
        grid=(N // T,),
        in_specs=[
            pl.BlockSpec((TOP_K, T), lambda i: (0, i), memory_space=pltpu.SMEM),
            pl.BlockSpec(memory_space=pl.ANY),
            pl.BlockSpec(memory_space=pl.ANY),
        ],
        out_specs=pl.BlockSpec(memory_space=pl.ANY),
        out_shape=jax.ShapeDtypeStruct(xs0.shape, xs0.dtype),
        scratch_shapes=[pltpu.SemaphoreType.DMA],
        input_output_aliases={2: 0},
        compiler_params=pltpu.CompilerParams(dimension_semantics=("arbitrary",)),
        name="dispatch",
    )(dest, xp, xs0)


def _expert_kernel(blk_e_ref, nb_ref, xs_ref, w1_ref, b1_ref, w2_ref, b2_ref, ys_ref):
    i = pl.program_id(0)

    @pl.when(i < nb_ref[0])
    def _():
        xb = _unpack_bf16_pairs(xs_ref[...]).astype(_BF)
        h = _dot(xb, w1_ref[0]) + b1_ref[0]
        glu = jnp.minimum(h[:, :D_EXPERT], SWIGLU_LIMIT)
        lin = jnp.clip(h[:, D_EXPERT:], -SWIGLU_LIMIT, SWIGLU_LIMIT)
        act = glu * jax.nn.sigmoid(SWIGLU_ALPHA * glu) * (lin + 1.0)
        y = _dot(act.astype(_BF), w2_ref[0]) + b2_ref[0]
        ys_ref[...] = _pack_bf16_pairs(y)

    @pl.when(i >= nb_ref[0])
    def _():
        ys_ref[...] = jnp.zeros_like(ys_ref)


def _expert_call(blk_e, nb_used, xs, w1, b1, w2, b2):
    n_blocks = xs.shape[0] // MOE_BLOCK
    M = MOE_BLOCK
    grid_spec = pltpu.PrefetchScalarGridSpec(
        num_scalar_prefetch=2,
        grid=(n_blocks,),
        in_specs=[
            pl.BlockSpec((M, D_MODEL // 2), lambda i, be, nb: (jnp.minimum(i, nb[0] - 1), 0)),
            pl.BlockSpec((1, D_MODEL, 2 * D_EXPERT), lambda i, be, nb: (be[i], 0, 0)),
            pl.BlockSpec((1, 1, 2 * D_EXPERT), lambda i, be, nb: (be[i], 0, 0)),
            pl.BlockSpec((1, D_EXPERT, D_MODEL), lambda i, be, nb: (be[i], 0, 0)),
            pl.BlockSpec((1, 1, D_MODEL), lambda i, be, nb: (be[i], 0, 0)),
        ],
        out_specs=pl.BlockSpec((M, D_MODEL // 2), lambda i, be, nb: (i, 0)),
    )
    return pl.pallas_call(
        _expert_kernel,
        grid_spec=grid_spec,
        out_shape=jax.ShapeDtypeStruct(xs.shape, jnp.uint32),
        compiler_params=pltpu.CompilerParams(
            dimension_semantics=("arbitrary",), vmem_limit_bytes=VMEM_LIMIT),
        name="experts",
    )(blk_e, nb_used, xs, w1, b1, w2, b2)


def _combine_kernel(dest_ref, h1_ref, gate_ref, gfin_ref, ys_hbm, out_ref, buf, sem):
    T = COMBINE_TILE

    def row_copy(t, k):
        return pltpu.make_async_copy(ys_hbm.at[pl.ds(dest_ref[k, t], 1)],
                                     buf.at[k, pl.ds(t, 1)], sem)

    def issue(t, carry):
        for k in range(TOP_K):
            row_copy(t, k).start()
        return carry

    lax.fori_loop(0, T, issue, 0)

    def drain(t, carry):
        for k in range(TOP_K):
            row_copy(t, k).wait()
        return carry

    lax.fori_loop(0, T, drain, 0)

    acc = h1_ref[...]
    gates = gate_ref[...]
    for k in range(TOP_K):
        acc = acc + gates[:, k:k + 1] * _unpack_bf16_pairs(buf[k])
    out_ref[...] = _rms(acc, gfin_ref[...])


def _combine_call(dest, h1, gates_t, gfin, ys):
    N = h1.shape[0]
    T = COMBINE_TILE
    return pl.pallas_call(
        _combine_kernel,
        grid=(N // T,),
        in_specs=[
            pl.BlockSpec((TOP_K, T), lambda i: (0, i), memory_space=pltpu.SMEM),
            pl.BlockSpec((T, D_MODEL), lambda i: (i, 0)),
            pl.BlockSpec((T, TOP_K), lambda i: (i, 0)),
            pl.BlockSpec((1, D_MODEL), lambda i: (0, 0)),
            pl.BlockSpec(memory_space=pl.ANY),
        ],
        out_specs=pl.BlockSpec((T, D_MODEL), lambda i: (i, 0)),
        out_shape=jax.ShapeDtypeStruct((N, D_MODEL), _F32),
        scratch_shapes=[pltpu.VMEM((TOP_K, T, D_MODEL // 2), jnp.uint32), pltpu.SemaphoreType.DMA],
        compiler_params=pltpu.CompilerParams(dimension_semantics=("arbitrary",)),
        name="combine",
    )(dest, h1, gates_t, gfin, ys)


def _qk_column_permutation():
    half = HEAD_DIM // 2
    r = np.arange(half)

    def pair(base, ha, hb):
        return np.concatenate([base + ha * HEAD_DIM + r, base + hb * HEAD_DIM + r,
                               base + ha * HEAD_DIM + half + r, base + hb * HEAD_DIM + half + r])

    q = np.concatenate([pair(_OFF_Q, 2 * p, 2 * p + 1) for p in range(N_Q_HEADS // 2)])
    k = pair(_OFF_K, 0, 1)
    return np.concatenate([q, k, np.arange(_OFF_V, IN_COLS)])


def _rope_tables(S):
    half = HEAD_DIM // 2
    inv_freq = ROPE_THETA ** (-(jnp.arange(half, dtype=_F32) * 2.0 / HEAD_DIM))
    ang = jnp.arange(S, dtype=_F32)[:, None] * inv_freq[None, :]
    cos = jnp.tile(jnp.cos(ang), (1, 4))
    sin = jnp.tile(jnp.sin(ang), (1, 4))
    sign = jnp.where(jnp.arange(LANES) < 64, -1.0, 1.0).astype(_F32)
    return cos, sin * sign[None, :]


def kernel(x, norm_mix_g, w_in, b_in, attn_sinks, hgrn_lb, hgrn_norm_g, w_up_attn, w_up_hgrn, w_out,
           norm_ffn_g, w_router, b_router, w_moe1, b_moe1, w_moe2, b_moe2, norm_final_g):
    B, S, D = x.shape
    N = B * S
    assert D == D_MODEL and S % SEQ_TILE == 0 and SEQ_TILE % WINDOW == 0
    assert N % COMBINE_TILE == 0 and N % DISPATCH_TILE == 0 and norm_mix_g.shape[0] == 1

    perm = _qk_column_permutation()
    cos_t, sin_t = _rope_tables(S)
    h1, xp, idx, rank, gates, cnt = _mixer_call(
        x, attn_sinks[0], cos_t, sin_t, norm_mix_g,
        w_in[0][:, perm].astype(_BF), b_in[:, perm], hgrn_lb, hgrn_norm_g,
        w_up_attn[0].astype(_BF), w_up_hgrn[0].astype(_BF), w_out[0].astype(_BF),
        norm_ffn_g, w_router[0].T, b_router[0][:, None])

    counts = cnt[:, 0].astype(jnp.int32)
    padded = ((counts + MOE_BLOCK - 1) // MOE_BLOCK) * MOE_BLOCK
    pad_ends = jnp.cumsum(padded)
    pad_starts = pad_ends - padded
    n_blocks = (N * TOP_K) // MOE_BLOCK + N_EXPERTS
    dest = pad_starts[idx] + rank
    blk_e = jnp.minimum(jnp.searchsorted(pad_ends, jnp.arange(n_blocks, dtype=jnp.int32) * MOE_BLOCK,
                                         side='right'), N_EXPERTS - 1).astype(jnp.int32)
    nb_used = (pad_ends[-1:] // MOE_BLOCK).astype(jnp.int32)

    xs = _dispatch_call(dest, xp.reshape(N, D // 2), n_blocks * MOE_BLOCK)
    ys = _expert_call(blk_e, nb_used, xs, w_moe1[0].astype(_BF), b_moe1[0][:, None, :],
                      w_moe2[0].astype(_BF), b_moe2[0][:, None, :])
    out = _combine_call(dest, h1.reshape(N, D), gates.T, norm_final_g[None, :], ys)
    return out.reshape(B, S, D)
```

```python
import functools

import numpy as np
import jax
import jax.numpy as jnp
from jax import lax
from jax.experimental import pallas as pl
from jax.experimental.pallas import tpu as pltpu

D_MODEL = 1024
HEAD_DIM = 64
N_Q_HEADS = 8
N_KV_HEADS = 2
ATTN_WIDTH = N_Q_HEADS * HEAD_DIM
KV_WIDTH = N_KV_HEADS * HEAD_DIM
WINDOW = 128
ROPE_THETA = 10000.0
HGRN_HEADS = 4
HGRN_DK = 128
HGRN_WIDTH = HGRN_HEADS * HGRN_DK
N_EXPERTS = 32
TOP_K = 4
D_EXPERT = 1024
SWIGLU_ALPHA = 1.702
SWIGLU_LIMIT = 7.0
MOE_BLOCK = 512
NORM_EPS = 1e-5

_OFF_Q = 0
_OFF_K = _OFF_Q + ATTN_WIDTH
_OFF_V = _OFF_K + KV_WIDTH
_OFF_HQ = _OFF_V + KV_WIDTH
_OFF_HF = _OFF_HQ + HGRN_WIDTH
_OFF_HI = _OFF_HF + HGRN_WIDTH
_OFF_HG = _OFF_HI + HGRN_WIDTH
_OFF_GA = _OFF_HG + HGRN_WIDTH
_OFF_GH = _OFF_GA + D_MODEL
IN_COLS = _OFF_GH + D_MODEL

LANES = 128
SEQ_TILE = 256
HGRN_CHUNK = 64
HGRN_DIAG = 16
COMBINE_TILE = 256
DISPATCH_TILE = 256
VMEM_LIMIT = 56 * 1024 * 1024

_BF = jnp.bfloat16
_F32 = jnp.float32


def _dot(a, b):
    return jnp.dot(a, b, preferred_element_type=_F32)


def _dot_nt(a, b, precision=None):
    return lax.dot_general(a, b, (((1,), (1,)), ((), ())), precision=precision,
                           preferred_element_type=_F32)


def _dot_tn(a, b):
    return lax.dot_general(a, b, (((0,), (0,)), ((), ())), preferred_element_type=_F32)


def _rowsum(x):
    return jnp.sum(x, axis=1, keepdims=True)


def _rms(x, g):
    ms = _rowsum(x * x) * (1.0 / x.shape[1])
    return x * lax.rsqrt(ms + NORM_EPS) * g


def _pack_bf16_pairs(x):
    n = x.shape[1] // 2
    lo = lax.bitcast_convert_type(x[:, :n].astype(_BF).astype(_F32), jnp.uint32)
    hi = lax.bitcast_convert_type(x[:, n:].astype(_BF).astype(_F32), jnp.uint32)
    return (lo >> 16) | (hi & jnp.uint32(0xFFFF0000))


def _unpack_bf16_pairs(u):
    lo = lax.bitcast_convert_type(u << 16, _F32)
    hi = lax.bitcast_convert_type(u & jnp.uint32(0xFFFF0000), _F32)
    return jnp.concatenate([lo, hi], axis=1)


def _mixer_kernel(sinks_ref, x_ref, cos_ref, sin_ref, gmix_ref, w_in_ref, b_in_ref, lb_ref,
                  hgn_ref, w_ua_ref, w_uh_ref, w_out_ref, gffn_ref, w_rt_ref, b_r_ref,
                  h1_ref, xp_ref, idx_ref, rank_ref, gate_ref, cnt_ref,
                  kc_ref, vc_ref, st_ref, cnt_sc, ya_ref, o_ref):
    T = SEQ_TILE
    b = pl.program_id(0)
    s = pl.program_id(1)

    @pl.when(jnp.logical_and(b == 0, s == 0))
    def _():
        cnt_sc[...] = jnp.zeros_like(cnt_sc)

    @pl.when(s == 0)
    def _():
        kc_ref[...] = jnp.zeros_like(kc_ref)
        vc_ref[...] = jnp.zeros_like(vc_ref)
        st_ref[...] = jnp.zeros_like(st_ref)

    x = x_ref[0]
    xn = _rms(x, gmix_ref[...]).astype(_BF)

    def proj(off, width):
        return _dot(xn, w_in_ref[:, off:off + width]) + b_in_ref[:, off:off + width]

    cos = cos_ref[...]
    sin = sin_ref[...]

    def rope(t):
        return t * cos + pltpu.roll(t, 64, axis=1) * sin

    zq = proj(_OFF_Q, ATTN_WIDTH)
    scale = HEAD_DIM ** -0.5
    q_pairs = [(rope(zq[:, LANES * p:LANES * (p + 1)]) * scale).astype(_BF) for p in range(4)]
    k_rot = rope(proj(_OFF_K, KV_WIDTH))
    v_new = proj(_OFF_V, KV_WIDTH)

    k_ext = jnp.concatenate([kc_ref[...], k_rot], axis=0)
    v_ext = jnp.concatenate([vc_ref[...], v_new], axis=0)
    kc_ref[...] = k_rot[T - WINDOW:, :]
    vc_ref[...] = v_new[T - WINDOW:, :]

    lane = lax.broadcasted_iota(jnp.int32, (1, LANES), 1)
    slot_a = (lane % 64) < 32
    lane_lo = lane < 64
    k_r32 = pltpu.roll(k_ext, 32, axis=1)
    k_r96 = pltpu.roll(k_ext, 96, axis=1)
    v_r64 = pltpu.roll(v_ext, 64, axis=1)
    zero = jnp.zeros_like(k_ext)
    k_var = [(jnp.where(slot_a, k_ext, zero).astype(_BF), jnp.where(slot_a, zero, k_r32).astype(_BF)),
             (jnp.where(slot_a, k_r96, zero).astype(_BF), jnp.where(slot_a, zero, k_ext).astype(_BF))]
    v_var = [(jnp.where(lane_lo, v_ext, zero).astype(_BF), jnp.where(lane_lo, zero, v_r64).astype(_BF)),
             (jnp.where(lane_lo, v_r64, zero).astype(_BF), jnp.where(lane_lo, zero, v_ext).astype(_BF))]

    qi = lax.broadcasted_iota(jnp.int32, (2 * WINDOW, 2 * WINDOW), 0) % WINDOW
    kj = lax.broadcasted_iota(jnp.int32, (2 * WINDOW, 2 * WINDOW), 1)
    band = jnp.logical_and(kj > qi, kj <= qi + WINDOW)
    row_top = lax.broadcasted_iota(jnp.int32, (2 * WINDOW, 1), 0) < WINDOW
    neg_inf = jnp.float32(-jnp.inf)

    for n in range(T // WINDOW):
        if n == 0:
            ok = jnp.logical_and(band, jnp.logical_or(kj >= WINDOW, s > 0))
        else:
            ok = band
        r0 = n * WINDOW
        for j in range(N_KV_HEADS):
            q2 = jnp.concatenate([q_pairs[2 * j][r0:r0 + WINDOW], q_pairs[2 * j + 1][r0:r0 + WINDOW]], axis=0)
            kcat = jnp.concatenate([k_var[j][0][r0:r0 + 2 * WINDOW], k_var[j][1][r0:r0 + 2 * WINDOW]], axis=0)
            vcat = jnp.concatenate([v_var[j][0][r0:r0 + 2 * WINDOW], v_var[j][1][r0:r0 + 2 * WINDOW]], axis=0)
            sc = _dot_nt(q2, kcat)
            ps, rs = [], []
            for half in range(2):
                sh = jnp.where(ok, sc[:, 2 * WINDOW * half:2 * WINDOW * (half + 1)], neg_inf)
                snk = jnp.where(row_top, sinks_ref[4 * j + half], sinks_ref[4 * j + 2 + half])
                m = jnp.maximum(jnp.max(sh, axis=1, keepdims=True), snk)
                p = jnp.exp(sh - m)
                den = _rowsum(p) + jnp.exp(snk - m)
                ps.append(p.astype(_BF))
                rs.append(1.0 / den)
            o = _dot(jnp.concatenate(ps, axis=1), vcat)
            o = o * jnp.where(lane_lo, rs[0], rs[1])
            ya_ref[r0:r0 + WINDOW, LANES * (2 * j):LANES * (2 * j + 1)] = o[:WINDOW]
            ya_ref[r0:r0 + WINDOW, LANES * (2 * j + 1):LANES * (2 * j + 2)] = o[WINDOW:]

    HW = HGRN_WIDTH
    a0 = lb_ref[0:1, :]
    a1 = lb_ref[1:2, :]
    am = jnp.maximum(a0, a1)
    e0 = jnp.exp(a0 - am)
    lb = e0 / (e0 + jnp.exp(a1 - am))
    f = lb + (1.0 - lb) * jax.nn.sigmoid(proj(_OFF_HF, HW))
    kk = 1.0 - f
    g = jnp.log(f)
    hq = proj(_OFF_HQ, HW)
    hv = proj(_OFF_HI, HW)

    row = lax.broadcasted_iota(jnp.int32, (T, 1), 0)
    c0 = HGRN_DIAG

    def rdown(t, d):
        return pltpu.roll(t, d, axis=0)

    def rup(t, d):
        return pltpu.roll(t, T - d, axis=0)

    lcum = g
    d = 1
    while d < c0:
        lcum = lcum + jnp.where((row % c0) >= d, rdown(lcum, d), 0.0)
        d *= 2
    tot = jnp.where((row % c0) == c0 - 1, lcum, 0.0)
    d = 1
    while d < c0:
        tot = tot + rup(tot, d)
        d *= 2

    def head(t, h):
        return t[:, HGRN_DK * h:HGRN_DK * (h + 1)]

    qk0 = hq * kk
    o_parts = [_rowsum(head(qk0, h)) * head(hv, h) for h in range(HGRN_HEADS)]
    for d in range(1, c0):
        ks = rdown(kk, d)
        ls = rdown(lcum, d)
        vs = rdown(hv, d)
        pd = hq * ks * jnp.exp(jnp.minimum(lcum - ls, 0.0))
        okd = (row % c0) >= d
        for h in range(HGRN_HEADS):
            a = jnp.where(okd, _rowsum(head(pd, h)), 0.0)
            o_parts[h] = o_parts[h] + a * head(vs, h)

    C = HGRN_CHUNK
    levels = []
    bsz = c0
    lb_cum, lb_tot = lcum, tot
    while bsz < C:
        odd = (row % (2 * bsz)) >= bsz
        ex = jnp.exp(jnp.where(odd, lb_cum, lb_tot - lb_cum))
        ql = jnp.where(odd, hq * ex, 0.0).astype(_BF)
        kl = jnp.where(odd, 0.0, kk * ex).astype(_BF)
        levels.append((bsz, ql, kl))
        prev_tot = rdown(lb_tot, bsz)
        lb_cum = lb_cum + jnp.where(odd, prev_tot, 0.0)
        lb_tot = lb_tot + jnp.where(odd, prev_tot, rup(lb_tot, bsz))
        bsz *= 2
    q_in = (hq * jnp.exp(lb_cum)).astype(_BF)
    k_st = (kk * jnp.exp(lb_tot - lb_cum)).astype(_BF)
    hv_b = hv.astype(_BF)

    ti = lax.broadcasted_iota(jnp.int32, (C, C), 0)
    si = lax.broadcasted_iota(jnp.int32, (C, C), 1)
    lvl_masks = [jnp.logical_and(jnp.logical_and((ti // (2 * bs)) == (si // (2 * bs)), (ti % (2 * bs)) >= bs),
                                 (si % (2 * bs)) < bs) for bs, _, _ in levels]

    for c in range(T // C):
        rs_ = slice(c * C, (c + 1) * C)
        dec = jnp.exp(lb_tot[c * C:c * C + 1, :])
        for h in range(HGRN_HEADS):
            cs_ = slice(HGRN_DK * h, HGRN_DK * (h + 1))
            st = st_ref[h]
            vc = hv_b[rs_, cs_]
            inter = _dot_nt(q_in[rs_, cs_], st.astype(_BF))
            amat = jnp.zeros((C, C), _F32)
            for (bs, ql, kl), msk in zip(levels, lvl_masks):
                amat = amat + jnp.where(msk, _dot_nt(ql[rs_, cs_], kl[rs_, cs_]), 0.0)
            intra = _dot(amat.astype(_BF), vc)
            st_ref[h] = st * dec[:, cs_] + _dot_tn(vc, k_st[rs_, cs_])
            o_ref[rs_, cs_] = inter + intra + o_parts[h][rs_]

    o = o_ref[...]
    hg = proj(_OFF_HG, HW)
    yh_parts = []
    for h in range(HGRN_HEADS):
        oh = head(o, h)
        ms = _rowsum(oh * oh) * (1.0 / HGRN_DK)
        yh_parts.append(oh * lax.rsqrt(ms + NORM_EPS))
    yh = jnp.concatenate(yh_parts, axis=1) * hgn_ref[...]
    yh = (yh * (hg * jax.nn.sigmoid(hg))).astype(_BF)

    up_a = _dot(ya_ref[...].astype(_BF), w_ua_ref[...])
    up_h = _dot(yh, w_uh_ref[...])
    merged = jax.nn.sigmoid(proj(_OFF_GA, D_MODEL)) * up_a + jax.nn.sigmoid(proj(_OFF_GH, D_MODEL)) * up_h
    h1 = x + _dot(merged.astype(_BF), w_out_ref[...])
    h1_ref[0] = h1

    xn2 = _rms(h1, gffn_ref[...])
    xp_ref[0] = _pack_bf16_pairs(xn2)
    logits = _dot_nt(w_rt_ref[...], xn2, precision=lax.Precision.HIGHEST) + b_r_ref[...]
    eidx = lax.broadcasted_iota(jnp.int32, (N_EXPERTS, T), 0)
    vals, sels, ohs = [], [], []
    l = logits
    for _ in range(TOP_K):
        m = jnp.max(l, axis=0, keepdims=True)
        sel = jnp.min(jnp.where(l == m, eidx, N_EXPERTS), axis=0, keepdims=True)
        oh = eidx == sel
        vals.append(m)
        sels.append(sel)
        ohs.append(oh)
        l = jnp.where(oh, neg_inf, l)
    es = [jnp.exp(v - vals[0]) for v in vals]
    den = es[0] + es[1] + es[2] + es[3]
    chosen = jnp.zeros((N_EXPERTS, T), _F32)
    for oh in ohs:
        chosen = chosen + jnp.where(oh, 1.0, 0.0)
    ui = lax.broadcasted_iota(jnp.int32, (T, T), 0)
    uj = lax.broadcasted_iota(jnp.int32, (T, T), 1)
    upper = jnp.where(ui < uj, 1.0, 0.0).astype(_BF)
    before = _dot(chosen.astype(_BF), upper) + cnt_sc[:, 0:1]
    for k in range(TOP_K):
        idx_ref[k:k + 1, :] = sels[k]
        gate_ref[k:k + 1, :] = es[k] / den
        rank_ref[k:k + 1, :] = jnp.sum(jnp.where(ohs[k], before, 0.0), axis=0, keepdims=True).astype(jnp.int32)
    cnt_sc[...] = cnt_sc[...] + _rowsum(chosen)
    cnt_ref[...] = cnt_sc[...]


def _const_spec(shape):
    return pl.BlockSpec(shape, lambda b, s: (0,) * len(shape))


def _mixer_call(x, sinks, cos_t, sin_t, gmix, w_in, b_in, lb, hgn, w_ua, w_uh, w_out, gffn, w_rt, b_r):
    B, S, _ = x.shape
    T = SEQ_TILE
    N = B * S
    nS = S // T
    tok_spec = lambda w: pl.BlockSpec((4, T), lambda b, s: (0, b * nS + s))
    out_shape = (
        jax.ShapeDtypeStruct((B, S, D_MODEL), _F32),
        jax.ShapeDtypeStruct((B, S, D_MODEL // 2), jnp.uint32),
        jax.ShapeDtypeStruct((TOP_K, N), jnp.int32),
        jax.ShapeDtypeStruct((TOP_K, N), jnp.int32),
        jax.ShapeDtypeStruct((TOP_K, N), _F32),
        jax.ShapeDtypeStruct((N_EXPERTS, LANES), _F32),
    )
    in_specs = [
        pl.BlockSpec(memory_space=pltpu.SMEM),
        pl.BlockSpec((1, T, D_MODEL), lambda b, s: (b, s, 0)),
        pl.BlockSpec((T, LANES), lambda b, s: (s, 0)),
        pl.BlockSpec((T, LANES), lambda b, s: (s, 0)),
        _const_spec((1, D_MODEL)),
        _const_spec((D_MODEL, IN_COLS)),
        _const_spec((1, IN_COLS)),
        _const_spec((2, HGRN_WIDTH)),
        _const_spec((1, HGRN_WIDTH)),
        _const_spec((ATTN_WIDTH, D_MODEL)),
        _const_spec((HGRN_WIDTH, D_MODEL)),
        _const_spec((D_MODEL, D_MODEL)),
        _const_spec((1, D_MODEL)),
        _const_spec((N_EXPERTS, D_MODEL)),
        _const_spec((N_EXPERTS, 1)),
    ]
    out_specs = (
        pl.BlockSpec((1, T, D_MODEL), lambda b, s: (b, s, 0)),
        pl.BlockSpec((1, T, D_MODEL // 2), lambda b, s: (b, s, 0)),
        tok_spec(0), tok_spec(0), tok_spec(0),
        pl.BlockSpec((N_EXPERTS, LANES), lambda b, s: (0, 0)),
    )
    scratch = [
        pltpu.VMEM((WINDOW, KV_WIDTH), _F32),
        pltpu.VMEM((WINDOW, KV_WIDTH), _F32),
        pltpu.VMEM((HGRN_HEADS, HGRN_DK, HGRN_DK), _F32),
        pltpu.VMEM((N_EXPERTS, LANES), _F32),
        pltpu.VMEM((T, ATTN_WIDTH), _F32),
        pltpu.VMEM((T, HGRN_WIDTH), _F32),
    ]
    return pl.pallas_call(
        _mixer_kernel,
        grid=(B, nS),
        in_specs=in_specs,
        out_specs=out_specs,
        out_shape=out_shape,
        scratch_shapes=scratch,
        compiler_params=pltpu.CompilerParams(
            dimension_semantics=("arbitrary", "arbitrary"), vmem_limit_bytes=VMEM_LIMIT),
        name="mixer",
    )(sinks, x, cos_t, sin_t, gmix, w_in, b_in, lb, hgn, w_ua, w_uh, w_out, gffn, w_rt, b_r)


def _dispatch_kernel(dest_ref, xp_ref, xs_in_hbm, xs_hbm, sem):
    del xs_in_hbm
    T = DISPATCH_TILE

    def row_copy(t, k):
        return pltpu.make_async_copy(xp_ref.at[pl.ds(t, 1)], xs_hbm.at[pl.ds(dest_ref[k, t], 1)], sem)

    def issue(t, carry):
        for k in range(TOP_K):
            row_copy(t, k).start()
        return carry

    lax.fori_loop(0, T, issue, 0)

    def drain(t, carry):
        for k in range(TOP_K):
            row_copy(t, k).wait()
        return carry

    lax.fori_loop(0, T, drain, 0)


def _dispatch_call(dest, xp, n_rows):
    N = xp.shape[0]
    T = DISPATCH_TILE
    xs0 = jnp.zeros((n_rows, xp.shape[1]), xp.dtype)
    return pl.pallas_call(
        _dispatch_kernel,
        grid=(N // T,),
        in_specs=[
            pl.BlockSpec((TOP_K, T), lambda i: (0, i), memory_space=pltpu.SMEM),
            pl.BlockSpec((T, xp.shape[1]), lambda i: (i, 0)),
            pl.BlockSpec(memory_space=pl.ANY),
        ],
        out_specs=pl.BlockSpec(memory_space=pl.ANY),
        out_shape=jax.ShapeDtypeStruct(xs0.shape, xs0.dtype),
        scratch_shapes=[pltpu.SemaphoreType.DMA],
        input_output_aliases={2: 0},
        compiler_params=pltpu.CompilerParams(dimension_semantics=("arbitrary",)),
        name="dispatch",
    )(dest, xp, xs0)


def _expert_kernel(blk_e_ref, nb_ref, xs_ref, w1_ref, b1_ref, w2_ref, b2_ref, ys_ref):
    i = pl.program_id(0)

    @pl.when(i < nb_ref[0])
    def _():
        xb = _unpack_bf16_pairs(xs_ref[...]).astype(_BF)
        h = _dot(xb, w1_ref[0]) + b1_ref[0]
        glu = jnp.minimum(h[:, :D_EXPERT], SWIGLU_LIMIT)
        lin = jnp.clip(h[:, D_EXPERT:], -SWIGLU_LIMIT, SWIGLU_LIMIT)
        act = glu * jax.nn.sigmoid(SWIGLU_ALPHA * glu) * (lin + 1.0)
        y = _dot(act.astype(_BF), w2_ref[0]) + b2_ref[0]
        ys_ref[...] = _pack_bf16_pairs(y)

    @pl.when(i >= nb_ref[0])
    def _():
        ys_ref[...] = jnp.zeros_like(ys_ref)


def _expert_call(blk_e, nb_used, xs, w1, b1, w2, b2):
    n_blocks = xs.shape[0] // MOE_BLOCK
    M = MOE_BLOCK
    grid_spec = pltpu.PrefetchScalarGridSpec(
        num_scalar_prefetch=2,
        grid=(n_blocks,),
        in_specs=[
            pl.BlockSpec((M, D_MODEL // 2), lambda i, be, nb: (jnp.minimum(i, nb[0] - 1), 0)),
            pl.BlockSpec((1, D_MODEL, 2 * D_EXPERT), lambda i, be, nb: (be[i], 0, 0)),
            pl.BlockSpec((1, 1, 2 * D_EXPERT), lambda i, be, nb: (be[i], 0, 0)),
            pl.BlockSpec((1, D_EXPERT, D_MODEL), lambda i, be, nb: (be[i], 0, 0)),
            pl.BlockSpec((1, 1, D_MODEL), lambda i, be, nb: (be[i], 0, 0)),
        ],
        out_specs=pl.BlockSpec((M, D_MODEL // 2), lambda i, be, nb: (i, 0)),
    )
    return pl.pallas_call(
        _expert_kernel,
        grid_spec=grid_spec,
        out_shape=jax.ShapeDtypeStruct(xs.shape, jnp.uint32),
        compiler_params=pltpu.CompilerParams(
            dimension_semantics=("arbitrary",), vmem_limit_bytes=VMEM_LIMIT),
        name="experts",
    )(blk_e, nb_used, xs, w1, b1, w2, b2)


def _combine_kernel(dest_ref, h1_ref, gate_ref, gfin_ref, ys_hbm, out_ref, buf, sem):
    T = COMBINE_TILE

    def row_copy(t, k):
        return pltpu.make_async_copy(ys_hbm.at[pl.ds(dest_ref[k, t], 1)],
                                     buf.at[k, pl.ds(t, 1)], sem)

    def issue(t, carry):
        for k in range(TOP_K):
            row_copy(t, k).start()
        return carry

    lax.fori_loop(0, T, issue, 0)

    def drain(t, carry):
        for k in range(TOP_K):
            row_copy(t, k).wait()
        return carry

    lax.fori_loop(0, T, drain, 0)

    acc = h1_ref[...]
    gates = gate_ref[...]
    for k in range(TOP_K):
        acc = acc + gates[:, k:k + 1] * _unpack_bf16_pairs(buf[k])
    out_ref[...] = _rms(acc, gfin_ref[...])


def _combine_call(dest, h1, gates_t, gfin, ys):
    N = h1.shape[0]
    T = COMBINE_TILE
    return pl.pallas_call(
        _combine_kernel,
        grid=(N // T,),
        in_specs=[
            pl.BlockSpec((TOP_K, T), lambda i: (0, i), memory_space=pltpu.SMEM),
            pl.BlockSpec((T, D_MODEL), lambda i: (i, 0)),
            pl.BlockSpec((T, TOP_K), lambda i: (i, 0)),
            pl.BlockSpec((1, D_MODEL), lambda i: (0, 0)),
            pl.BlockSpec(memory_space=pl.ANY),
        ],
        out_specs=pl.BlockSpec((T, D_MODEL), lambda i: (i, 0)),
        out_shape=jax.ShapeDtypeStruct((N, D_MODEL), _F32),
        scratch_shapes=[pltpu.VMEM((TOP_K, T, D_MODEL // 2), jnp.uint32), pltpu.SemaphoreType.DMA],
        compiler_params=pltpu.CompilerParams(dimension_semantics=("arbitrary",)),
        name="combine",
    )(dest, h1, gates_t, gfin, ys)


def _qk_column_permutation():
    half = HEAD_DIM // 2
    r = np.arange(half)

    def pair(base, ha, hb):
        return np.concatenate([base + ha * HEAD_DIM + r, base + hb * HEAD_DIM + r,
                               base + ha * HEAD_DIM + half + r, base + hb * HEAD_DIM + half + r])

    q = np.concatenate([pair(_OFF_Q, 2 * p, 2 * p + 1) for p in range(N_Q_HEADS // 2)])
    k = pair(_OFF_K, 0, 1)
    return np.concatenate([q, k, np.arange(_OFF_V, IN_COLS)])


def _rope_tables(S):
    half = HEAD_DIM // 2
    inv_freq = ROPE_THETA ** (-(jnp.arange(half, dtype=_F32) * 2.0 / HEAD_DIM))
    ang = jnp.arange(S, dtype=_F32)[:, None] * inv_freq[None, :]
    cos = jnp.tile(jnp.cos(ang), (1, 4))
    sin = jnp.tile(jnp.sin(ang), (1, 4))
    sign = jnp.where(jnp.arange(LANES) < 64, -1.0, 1.0).astype(_F32)
    return cos, sin * sign[None, :]


def kernel(x, norm_mix_g, w_in, b_in, attn_sinks, hgrn_lb, hgrn_norm_g, w_up_attn, w_up_hgrn, w_out,
           norm_ffn_g, w_router, b_router, w_moe1, b_moe1, w_moe2, b_moe2, norm_final_g):
    B, S, D = x.shape
    N = B * S
    assert D == D_MODEL and S % SEQ_TILE == 0 and SEQ_TILE % WINDOW == 0
    assert N % COMBINE_TILE == 0 and N % DISPATCH_TILE == 0 and norm_mix_g.shape[0] == 1

    perm = _qk_column_permutation()
    cos_t, sin_t = _rope_tables(S)
    h1, xp, idx, rank, gates, cnt = _mixer_call(
        x, attn_sinks[0], cos_t, sin_t, norm_mix_g,
        w_in[0][:, perm].astype(_BF), b_in[:, perm], hgrn_lb, hgrn_norm_g,
        w_up_attn[0].astype(_BF), w_up_hgrn[0].astype(_BF), w_out[0].astype(_BF),
        norm_ffn_g, w_router[0].T, b_router[0][:, None])

    counts = cnt[:, 0].astype(jnp.int32)
    padded = ((counts + MOE_BLOCK - 1) // MOE_BLOCK) * MOE_BLOCK
    pad_ends = jnp.cumsum(padded)
    pad_starts = pad_ends - padded
    n_blocks = (N * TOP_K) // MOE_BLOCK + N_EXPERTS
    dest = rank
    for e in range(N_EXPERTS):
        dest = dest + jnp.where(idx == e, pad_starts[e], 0)
    blk_start = jnp.arange(n_blocks, dtype=jnp.int32) * MOE_BLOCK
    blk_e = jnp.minimum(jnp.sum((pad_ends[None, :] <= blk_start[:, None]).astype(jnp.int32), axis=1),
                        N_EXPERTS - 1)
    nb_used = (pad_ends[-1:] // MOE_BLOCK).astype(jnp.int32)

    xs = _dispatch_call(dest, xp.reshape(N, D // 2), n_blocks * MOE_BLOCK)
    ys = _expert_call(blk_e, nb_used, xs, w_moe1[0].astype(_BF), b_moe1[0][:, None, :],
                      w_moe2[0].astype(_BF), b_moe2[0][:, None, :])
    out = _combine_call(dest, h1.reshape(N, D), gates.T, norm_final_g[None, :], ys)
    return out.reshape(B, S, D)
```

```python
import functools

import numpy as np
import jax
import jax.numpy as jnp
from jax import lax
from jax.experimental import pallas as pl
from jax.experimental.pallas import tpu as pltpu
from jax.experimental.pallas import tpu_sc as plsc

D_MODEL = 1024
HEAD_DIM = 64
N_Q_HEADS = 8
N_KV_HEADS = 2
ATTN_WIDTH = N_Q_HEADS * HEAD_DIM
KV_WIDTH = N_KV_HEADS * HEAD_DIM
WINDOW = 128
ROPE_THETA = 10000.0
HGRN_HEADS = 4
HGRN_DK = 128
HGRN_WIDTH = HGRN_HEADS * HGRN_DK
N_EXPERTS = 32
TOP_K = 4
D_EXPERT = 1024
SWIGLU_ALPHA = 1.702
SWIGLU_LIMIT = 7.0
MOE_BLOCK = 512
NORM_EPS = 1e-5

_OFF_Q = 0
_OFF_K = _OFF_Q + ATTN_WIDTH
_OFF_V = _OFF_K + KV_WIDTH
_OFF_HQ = _OFF_V + KV_WIDTH
_OFF_HF = _OFF_HQ + HGRN_WIDTH
_OFF_HI = _OFF_HF + HGRN_WIDTH
_OFF_HG = _OFF_HI + HGRN_WIDTH
_OFF_GA = _OFF_HG + HGRN_WIDTH
_OFF_GH = _OFF_GA + D_MODEL
IN_COLS = _OFF_GH + D_MODEL

LANES = 128
SEQ_TILE = 256
HGRN_CHUNK = 64
HGRN_DIAG = 16
COMBINE_TILE = 512
SC_CORES = 2
SC_WORKERS = 32
SC_CHUNK = 128
VMEM_LIMIT = 56 * 1024 * 1024

_BF = jnp.bfloat16
_F32 = jnp.float32


def _dot(a, b):
    return jnp.dot(a, b, preferred_element_type=_F32)


def _dot_nt(a, b, precision=None):
    return lax.dot_general(a, b, (((1,), (1,)), ((), ())), precision=precision,
                           preferred_element_type=_F32)


def _dot_tn(a, b):
    return lax.dot_general(a, b, (((0,), (0,)), ((), ())), preferred_element_type=_F32)


def _rowsum(x):
    return jnp.sum(x, axis=1, keepdims=True)


def _rms(x, g):
    ms = _rowsum(x * x) * (1.0 / x.shape[1])
    return x * lax.rsqrt(ms + NORM_EPS) * g


def _pack_bf16_pairs(x):
    n = x.shape[1] // 2
    lo = lax.bitcast_convert_type(x[:, :n].astype(_BF).astype(_F32), jnp.uint32)
    hi = lax.bitcast_convert_type(x[:, n:].astype(_BF).astype(_F32), jnp.uint32)
    return (lo >> 16) | (hi & jnp.uint32(0xFFFF0000))


def _unpack_bf16_pairs(u):
    lo = lax.bitcast_convert_type(u << 16, _F32)
    hi = lax.bitcast_convert_type(u & jnp.uint32(0xFFFF0000), _F32)
    return jnp.concatenate([lo, hi], axis=1)


def _mixer_kernel(sinks_ref, x_ref, cos_ref, sin_ref, gmix_ref, w_in_ref, b_in_ref, lb_ref,
                  hgn_ref, w_ua_ref, w_uh_ref, w_out_ref, gffn_ref, w_rt_ref, b_r_ref,
                  h1_ref, xp_ref, idx_ref, rank_ref, gate_ref, cnt_ref,
                  kc_ref, vc_ref, st_ref, cnt_sc, ya_ref, o_ref):
    T = SEQ_TILE
    b = pl.program_id(0)
    s = pl.program_id(1)

    @pl.when(jnp.logical_and(b == 0, s == 0))
    def _():
        cnt_sc[...] = jnp.zeros_like(cnt_sc)

    @pl.when(s == 0)
    def _():
        kc_ref[...] = jnp.zeros_like(kc_ref)
        vc_ref[...] = jnp.zeros_like(vc_ref)
        st_ref[...] = jnp.zeros_like(st_ref)

    x = x_ref[0]
    xn = _rms(x, gmix_ref[...]).astype(_BF)

    def proj(off, width):
        return _dot(xn, w_in_ref[:, off:off + width]) + b_in_ref[:, off:off + width]

    cos = cos_ref[...]
    sin = sin_ref[...]

    def rope(t):
        return t * cos + pltpu.roll(t, 64, axis=1) * sin

    zq = proj(_OFF_Q, ATTN_WIDTH)
    scale = HEAD_DIM ** -0.5
    q_pairs = [(rope(zq[:, LANES * p:LANES * (p + 1)]) * scale).astype(_BF) for p in range(4)]
    k_rot = rope(proj(_OFF_K, KV_WIDTH))
    v_new = proj(_OFF_V, KV_WIDTH)

    k_ext = jnp.concatenate([kc_ref[...], k_rot], axis=0)
    v_ext = jnp.concatenate([vc_ref[...], v_new], axis=0)
    kc_ref[...] = k_rot[T - WINDOW:, :]
    vc_ref[...] = v_new[T - WINDOW:, :]

    lane = lax.broadcasted_iota(jnp.int32, (1, LANES), 1)
    slot_a = (lane % 64) < 32
    lane_lo = lane < 64
    k_r32 = pltpu.roll(k_ext, 32, axis=1)
    k_r96 = pltpu.roll(k_ext, 96, axis=1)
    v_r64 = pltpu.roll(v_ext, 64, axis=1)
    zero = jnp.zeros_like(k_ext)
    k_var = [(jnp.where(slot_a, k_ext, zero).astype(_BF), jnp.where(slot_a, zero, k_r32).astype(_BF)),
             (jnp.where(slot_a, k_r96, zero).astype(_BF), jnp.where(slot_a, zero, k_ext).astype(_BF))]
    v_var = [(jnp.where(lane_lo, v_ext, zero).astype(_BF), jnp.where(lane_lo, zero, v_r64).astype(_BF)),
             (jnp.where(lane_lo, v_r64, zero).astype(_BF), jnp.where(lane_lo, zero, v_ext).astype(_BF))]

    qi = lax.broadcasted_iota(jnp.int32, (2 * WINDOW, 2 * WINDOW), 0) % WINDOW
    kj = lax.broadcasted_iota(jnp.int32, (2 * WINDOW, 2 * WINDOW), 1)
    band = jnp.logical_and(kj > qi, kj <= qi + WINDOW)
    row_top = lax.broadcasted_iota(jnp.int32, (2 * WINDOW, 1), 0) < WINDOW
    neg_inf = jnp.float32(-jnp.inf)

    for n in range(T // WINDOW):
        if n == 0:
            ok = jnp.logical_and(band, jnp.logical_or(kj >= WINDOW, s > 0))
        else:
            ok = band
        r0 = n * WINDOW
        for j in range(N_KV_HEADS):
            q2 = jnp.concatenate([q_pairs[2 * j][r0:r0 + WINDOW], q_pairs[2 * j + 1][r0:r0 + WINDOW]], axis=0)
            kcat = jnp.concatenate([k_var[j][0][r0:r0 + 2 * WINDOW], k_var[j][1][r0:r0 + 2 * WINDOW]], axis=0)
            vcat = jnp.concatenate([v_var[j][0][r0:r0 + 2 * WINDOW], v_var[j][1][r0:r0 + 2 * WINDOW]], axis=0)
            sc = _dot_nt(q2, kcat)
            ps, rs = [], []
            for half in range(2):
                sh = jnp.where(ok, sc[:, 2 * WINDOW * half:2 * WINDOW * (half + 1)], neg_inf)
                snk = jnp.where(row_top, sinks_ref[4 * j + half], sinks_ref[4 * j + 2 + half])
                m = jnp.maximum(jnp.max(sh, axis=1, keepdims=True), snk)
                p = jnp.exp(sh - m)
                den = _rowsum(p) + jnp.exp(snk - m)
                ps.append(p.astype(_BF))
                rs.append(1.0 / den)
            o = _dot(jnp.concatenate(ps, axis=1), vcat)
            o = o * jnp.where(lane_lo, rs[0], rs[1])
            ya_ref[r0:r0 + WINDOW, LANES * (2 * j):LANES * (2 * j + 1)] = o[:WINDOW]
            ya_ref[r0:r0 + WINDOW, LANES * (2 * j + 1):LANES * (2 * j + 2)] = o[WINDOW:]

    HW = HGRN_WIDTH
    a0 = lb_ref[0:1, :]
    a1 = lb_ref[1:2, :]
    am = jnp.maximum(a0, a1)
    e0 = jnp.exp(a0 - am)
    lb = e0 / (e0 + jnp.exp(a1 - am))
    f = lb + (1.0 - lb) * jax.nn.sigmoid(proj(_OFF_HF, HW))
    kk = 1.0 - f
    g = jnp.log(f)
    hq = proj(_OFF_HQ, HW)
    hv = proj(_OFF_HI, HW)

    row = lax.broadcasted_iota(jnp.int32, (T, 1), 0)
    c0 = HGRN_DIAG

    def rdown(t, d):
        return pltpu.roll(t, d, axis=0)

    def rup(t, d):
        return pltpu.roll(t, T - d, axis=0)

    lcum = g
    d = 1
    while d < c0:
        lcum = lcum + jnp.where((row % c0) >= d, rdown(lcum, d), 0.0)
        d *= 2
    tot = jnp.where((row % c0) == c0 - 1, lcum, 0.0)
    d = 1
    while d < c0:
        tot = tot + rup(tot, d)
        d *= 2

    def head(t, h):
        return t[:, HGRN_DK * h:HGRN_DK * (h + 1)]

    qk0 = hq * kk
    o_parts = [_rowsum(head(qk0, h)) * head(hv, h) for h in range(HGRN_HEADS)]
    for d in range(1, c0):
        ks = rdown(kk, d)
        ls = rdown(lcum, d)
        vs = rdown(hv, d)
        pd = hq * ks * jnp.exp(jnp.minimum(lcum - ls, 0.0))
        okd = (row % c0) >= d
        for h in range(HGRN_HEADS):
            a = jnp.where(okd, _rowsum(head(pd, h)), 0.0)
            o_parts[h] = o_parts[h] + a * head(vs, h)

    C = HGRN_CHUNK
    levels = []
    bsz = c0
    lb_cum, lb_tot = lcum, tot
    while bsz < C:
        odd = (row % (2 * bsz)) >= bsz
        ex = jnp.exp(jnp.where(odd, lb_cum, lb_tot - lb_cum))
        ql = jnp.where(odd, hq * ex, 0.0).astype(_BF)
        kl = jnp.where(odd, 0.0, kk * ex).astype(_BF)
        levels.append((bsz, ql, kl))
        prev_tot = rdown(lb_tot, bsz)
        lb_cum = lb_cum + jnp.where(odd, prev_tot, 0.0)
        lb_tot = lb_tot + jnp.where(odd, prev_tot, rup(lb_tot, bsz))
        bsz *= 2
    q_in = (hq * jnp.exp(lb_cum)).astype(_BF)
    k_st = (kk * jnp.exp(lb_tot - lb_cum)).astype(_BF)
    hv_b = hv.astype(_BF)

    ti = lax.broadcasted_iota(jnp.int32, (C, C), 0)
    si = lax.broadcasted_iota(jnp.int32, (C, C), 1)
    lvl_masks = [jnp.logical_and(jnp.logical_and((ti // (2 * bs)) == (si // (2 * bs)), (ti % (2 * bs)) >= bs),
                                 (si % (2 * bs)) < bs) for bs, _, _ in levels]

    for c in range(T // C):
        rs_ = slice(c * C, (c + 1) * C)
        dec = jnp.exp(lb_tot[c * C:c * C + 1, :])
        for h in range(HGRN_HEADS):
            cs_ = slice(HGRN_DK * h, HGRN_DK * (h + 1))
            st = st_ref[h]
            vc = hv_b[rs_, cs_]
            inter = _dot_nt(q_in[rs_, cs_], st.astype(_BF))
            amat = jnp.zeros((C, C), _F32)
            for (bs, ql, kl), msk in zip(levels, lvl_masks):
                amat = amat + jnp.where(msk, _dot_nt(ql[rs_, cs_], kl[rs_, cs_]), 0.0)
            intra = _dot(amat.astype(_BF), vc)
            st_ref[h] = st * dec[:, cs_] + _dot_tn(vc, k_st[rs_, cs_])
            o_ref[rs_, cs_] = inter + intra + o_parts[h][rs_]

    o = o_ref[...]
    hg = proj(_OFF_HG, HW)
    yh_parts = []
    for h in range(HGRN_HEADS):
        oh = head(o, h)
        ms = _rowsum(oh * oh) * (1.0 / HGRN_DK)
        yh_parts.append(oh * lax.rsqrt(ms + NORM_EPS))
    yh = jnp.concatenate(yh_parts, axis=1) * hgn_ref[...]
    yh = (yh * (hg * jax.nn.sigmoid(hg))).astype(_BF)

    up_a = _dot(ya_ref[...].astype(_BF), w_ua_ref[...])
    up_h = _dot(yh, w_uh_ref[...])
    merged = jax.nn.sigmoid(proj(_OFF_GA, D_MODEL)) * up_a + jax.nn.sigmoid(proj(_OFF_GH, D_MODEL)) * up_h
    h1 = x + _dot(merged.astype(_BF), w_out_ref[...])
    h1_ref[0] = h1

    xn2 = _rms(h1, gffn_ref[...])
    xp_ref[0] = _pack_bf16_pairs(xn2)
    logits = _dot_nt(w_rt_ref[...], xn2, precision=lax.Precision.HIGHEST) + b_r_ref[...]
    eidx = lax.broadcasted_iota(jnp.int32, (N_EXPERTS, T), 0)
    vals, sels, ohs = [], [], []
    l = logits
    for _ in range(TOP_K):
        m = jnp.max(l, axis=0, keepdims=True)
        sel = jnp.min(jnp.where(l == m, eidx, N_EXPERTS), axis=0, keepdims=True)
        oh = eidx == sel
        vals.append(m)
        sels.append(sel)
        ohs.append(oh)
        l = jnp.where(oh, neg_inf, l)
    es = [jnp.exp(v - vals[0]) for v in vals]
    den = es[0] + es[1] + es[2] + es[3]
    chosen = jnp.zeros((N_EXPERTS, T), _F32)
    for oh in ohs:
        chosen = chosen + jnp.where(oh, 1.0, 0.0)
    ui = lax.broadcasted_iota(jnp.int32, (T, T), 0)
    uj = lax.broadcasted_iota(jnp.int32, (T, T), 1)
    upper = jnp.where(ui < uj, 1.0, 0.0).astype(_BF)
    before = _dot(chosen.astype(_BF), upper) + cnt_sc[:, 0:1]
    for k in range(TOP_K):
        idx_ref[k:k + 1, :] = sels[k]
        gate_ref[k:k + 1, :] = es[k] / den
        rank_ref[k:k + 1, :] = jnp.sum(jnp.where(ohs[k], before, 0.0), axis=0, keepdims=True).astype(jnp.int32)
    cnt_sc[...] = cnt_sc[...] + _rowsum(chosen)
    cnt_ref[...] = cnt_sc[...]


def _const_spec(shape):
    return pl.BlockSpec(shape, lambda b, s: (0,) * len(shape))


def _mixer_call(x, sinks, cos_t, sin_t, gmix, w_in, b_in, lb, hgn, w_ua, w_uh, w_out, gffn, w_rt, b_r):
    B, S, _ = x.shape
    T = SEQ_TILE
    N = B * S
    nS = S // T
    tok_spec = lambda w: pl.BlockSpec((4, T), lambda b, s: (0, b * nS + s))
    out_shape = (
        jax.ShapeDtypeStruct((B, S, D_MODEL), _F32),
        jax.ShapeDtypeStruct((B, S, D_MODEL // 2), jnp.uint32),
        jax.ShapeDtypeStruct((TOP_K, N), jnp.int32),
        jax.ShapeDtypeStruct((TOP_K, N), jnp.int32),
        jax.ShapeDtypeStruct((TOP_K, N), _F32),
        jax.ShapeDtypeStruct((N_EXPERTS, LANES), _F32),
    )
    in_specs = [
        pl.BlockSpec(memory_space=pltpu.SMEM),
        pl.BlockSpec((1, T, D_MODEL), lambda b, s: (b, s, 0)),
        pl.BlockSpec((T, LANES), lambda b, s: (s, 0)),
        pl.BlockSpec((T, LANES), lambda b, s: (s, 0)),
        _const_spec((1, D_MODEL)),
        _const_spec((D_MODEL, IN_COLS)),
        _const_spec((1, IN_COLS)),
        _const_spec((2, HGRN_WIDTH)),
        _const_spec((1, HGRN_WIDTH)),
        _const_spec((ATTN_WIDTH, D_MODEL)),
        _const_spec((HGRN_WIDTH, D_MODEL)),
        _const_spec((D_MODEL, D_MODEL)),
        _const_spec((1, D_MODEL)),
        _const_spec((N_EXPERTS, D_MODEL)),
        _const_spec((N_EXPERTS, 1)),
    ]
    out_specs = (
        pl.BlockSpec((1, T, D_MODEL), lambda b, s: (b, s, 0)),
        pl.BlockSpec((1, T, D_MODEL // 2), lambda b, s: (b, s, 0)),
        tok_spec(0), tok_spec(0), tok_spec(0),
        pl.BlockSpec((N_EXPERTS, LANES), lambda b, s: (0, 0)),
    )
    scratch = [
        pltpu.VMEM((WINDOW, KV_WIDTH), _F32),
        pltpu.VMEM((WINDOW, KV_WIDTH), _F32),
        pltpu.VMEM((HGRN_HEADS, HGRN_DK, HGRN_DK), _F32),
        pltpu.VMEM((N_EXPERTS, LANES), _F32),
        pltpu.VMEM((T, ATTN_WIDTH), _F32),
        pltpu.VMEM((T, HGRN_WIDTH), _F32),
    ]
    return pl.pallas_call(
        _mixer_kernel,
        grid=(B, nS),
        in_specs=in_specs,
        out_specs=out_specs,
        out_shape=out_shape,
        scratch_shapes=scratch,
        compiler_params=pltpu.CompilerParams(
            dimension_semantics=("arbitrary", "arbitrary"), vmem_limit_bytes=VMEM_LIMIT),
        name="mixer",
    )(sinks, x, cos_t, sin_t, gmix, w_in, b_in, lb, hgn, w_ua, w_uh, w_out, gffn, w_rt, b_r)


def _sc_mesh():
    return plsc.VectorSubcoreMesh(core_axis_name="c", subcore_axis_name="s")


def _sc_worker_id():
    return lax.axis_index("s") * SC_CORES + lax.axis_index("c")


def _sc_scatter_rows(x, dest, n_rows):
    N, C = x.shape
    K = dest.shape[0]
    n_chunks = N // SC_CHUNK
    per_worker = n_chunks // SC_WORKERS
    dest3 = dest.reshape(K, n_chunks, SC_CHUNK)

    @functools.partial(
        pl.kernel, out_type=jax.ShapeDtypeStruct((n_rows, C), x.dtype), mesh=_sc_mesh(),
        scratch_types=[pltpu.VMEM((K, SC_CHUNK), jnp.int32), pltpu.VMEM((SC_CHUNK, C), x.dtype),
                       pltpu.SemaphoreType.DMA])
    def scatter(x_hbm, i_hbm, o_hbm, idx_v, rows_v, sem):
        base = _sc_worker_id() * per_worker

        @pl.loop(0, per_worker)
        def _(j):
            c = base + j
            off = pl.multiple_of(c * SC_CHUNK, SC_CHUNK)
            for k in range(K):
                pltpu.sync_copy(i_hbm.at[k, c], idx_v.at[k])
            pltpu.sync_copy(x_hbm.at[pl.ds(off, SC_CHUNK)], rows_v)
            for k in range(K):
                pltpu.async_copy(rows_v, o_hbm.at[idx_v.at[k]], sem).wait()

    return scatter(x, dest3)


def _sc_gather_rows(table, idx):
    M = idx.shape[0]
    C = table.shape[1]
    per_worker = M // SC_WORKERS
    n_chunks = per_worker // SC_CHUNK

    @functools.partial(
        pl.kernel, out_type=jax.ShapeDtypeStruct((M, C), table.dtype), mesh=_sc_mesh(),
        scratch_types=[pltpu.VMEM((SC_CHUNK,), jnp.int32), pltpu.VMEM((SC_CHUNK, C), table.dtype),
                       pltpu.SemaphoreType.DMA])
    def gather(t_hbm, i_hbm, o_hbm, idx_v, rows_v, sem):
        base = _sc_worker_id() * per_worker

        @pl.loop(0, n_chunks)
        def _(j):
            off = pl.multiple_of(base + j * SC_CHUNK, SC_CHUNK)
            pltpu.sync_copy(i_hbm.at[pl.ds(off, SC_CHUNK)], idx_v)
            pltpu.async_copy(t_hbm.at[idx_v], rows_v, sem).wait()
            pltpu.sync_copy(rows_v, o_hbm.at[pl.ds(off, SC_CHUNK)])

    return gather(table, idx)


def _expert_kernel(blk_e_ref, nb_ref, xs_ref, w1_ref, b1_ref, w2_ref, b2_ref, ys_ref):
    i = pl.program_id(0)

    @pl.when(i < nb_ref[0])
    def _():
        xb = _unpack_bf16_pairs(xs_ref[...]).astype(_BF)
        h = _dot(xb, w1_ref[0]) + b1_ref[0]
        glu = jnp.minimum(h[:, :D_EXPERT], SWIGLU_LIMIT)
        lin = jnp.clip(h[:, D_EXPERT:], -SWIGLU_LIMIT, SWIGLU_LIMIT)
        act = glu * jax.nn.sigmoid(SWIGLU_ALPHA * glu) * (lin + 1.0)
        y = _dot(act.astype(_BF), w2_ref[0]) + b2_ref[0]
        ys_ref[...] = _pack_bf16_pairs(y)

    @pl.when(i >= nb_ref[0])
    def _():
        ys_ref[...] = jnp.zeros_like(ys_ref)


def _expert_call(blk_e, nb_used, xs, w1, b1, w2, b2):
    n_blocks = xs.shape[0] // MOE_BLOCK
    M = MOE_BLOCK
    grid_spec = pltpu.PrefetchScalarGridSpec(
        num_scalar_prefetch=2,
        grid=(n_blocks,),
        in_specs=[
            pl.BlockSpec((M, D_MODEL // 2), lambda i, be, nb: (jnp.minimum(i, nb[0] - 1), 0)),
            pl.BlockSpec((1, D_MODEL, 2 * D_EXPERT), lambda i, be, nb: (be[i], 0, 0)),
            pl.BlockSpec((1, 1, 2 * D_EXPERT), lambda i, be, nb: (be[i], 0, 0)),
            pl.BlockSpec((1, D_EXPERT, D_MODEL), lambda i, be, nb: (be[i], 0, 0)),
            pl.BlockSpec((1, 1, D_MODEL), lambda i, be, nb: (be[i], 0, 0)),
        ],
        out_specs=pl.BlockSpec((M, D_MODEL // 2), lambda i, be, nb: (i, 0)),
    )
    return pl.pallas_call(
        _expert_kernel,
        grid_spec=grid_spec,
        out_shape=jax.ShapeDtypeStruct(xs.shape, jnp.uint32),
        compiler_params=pltpu.CompilerParams(
            dimension_semantics=("arbitrary",), vmem_limit_bytes=VMEM_LIMIT),
        name="experts",
    )(blk_e, nb_used, xs, w1, b1, w2, b2)


def _combine_kernel(h1_ref, gate_ref, gfin_ref, y0_ref, y1_ref, y2_ref, y3_ref, out_ref):
    acc = h1_ref[...]
    gates = gate_ref[...]
    for k, y_ref in enumerate((y0_ref, y1_ref, y2_ref, y3_ref)):
        acc = acc + gates[:, k:k + 1] * _unpack_bf16_pairs(y_ref[...])
    out_ref[...] = _rms(acc, gfin_ref[...])


def _combine_call(h1, gates_t, gfin, yg):
    N = h1.shape[0]
    T = COMBINE_TILE
    nT = N // T
    y_specs = [pl.BlockSpec((T, D_MODEL // 2), functools.partial(lambda i, k: (k * nT + i, 0), k=k))
               for k in range(TOP_K)]
    return pl.pallas_call(
        _combine_kernel,
        grid=(nT,),
        in_specs=[
            pl.BlockSpec((T, D_MODEL), lambda i: (i, 0)),
            pl.BlockSpec((T, TOP_K), lambda i: (i, 0)),
            pl.BlockSpec((1, D_MODEL), lambda i: (0, 0)),
        ] + y_specs,
        out_specs=pl.BlockSpec((T, D_MODEL), lambda i: (i, 0)),
        out_shape=jax.ShapeDtypeStruct((N, D_MODEL), _F32),
        compiler_params=pltpu.CompilerParams(dimension_semantics=("arbitrary",)),
        name="combine",
    )(h1, gates_t, gfin, yg, yg, yg, yg)


def _qk_column_permutation():
    half = HEAD_DIM // 2
    r = np.arange(half)

    def pair(base, ha, hb):
        return np.concatenate([base + ha * HEAD_DIM + r, base + hb * HEAD_DIM + r,
                               base + ha * HEAD_DIM + half + r, base + hb * HEAD_DIM + half + r])

    q = np.concatenate([pair(_OFF_Q, 2 * p, 2 * p + 1) for p in range(N_Q_HEADS // 2)])
    k = pair(_OFF_K, 0, 1)
    return np.concatenate([q, k, np.arange(_OFF_V, IN_COLS)])


def _rope_tables(S):
    half = HEAD_DIM // 2
    inv_freq = ROPE_THETA ** (-(jnp.arange(half, dtype=_F32) * 2.0 / HEAD_DIM))
    ang = jnp.arange(S, dtype=_F32)[:, None] * inv_freq[None, :]
    cos = jnp.tile(jnp.cos(ang), (1, 4))
    sin = jnp.tile(jnp.sin(ang), (1, 4))
    sign = jnp.where(jnp.arange(LANES) < 64, -1.0, 1.0).astype(_F32)
    return cos, sin * sign[None, :]


def kernel(x, norm_mix_g, w_in, b_in, attn_sinks, hgrn_lb, hgrn_norm_g, w_up_attn, w_up_hgrn, w_out,
           norm_ffn_g, w_router, b_router, w_moe1, b_moe1, w_moe2, b_moe2, norm_final_g):
    B, S, D = x.shape
    N = B * S
    assert D == D_MODEL and S % SEQ_TILE == 0 and SEQ_TILE % WINDOW == 0
    assert N % COMBINE_TILE == 0 and N % (SC_WORKERS * SC_CHUNK) == 0 and norm_mix_g.shape[0] == 1

    perm = _qk_column_permutation()
    cos_t, sin_t = _rope_tables(S)
    h1, xp, idx, rank, gates, cnt = _mixer_call(
        x, attn_sinks[0], cos_t, sin_t, norm_mix_g,
        w_in[0][:, perm].astype(_BF), b_in[:, perm], hgrn_lb, hgrn_norm_g,
        w_up_attn[0].astype(_BF), w_up_hgrn[0].astype(_BF), w_out[0].astype(_BF),
        norm_ffn_g, w_router[0].T, b_router[0][:, None])

    counts = cnt[:, 0].astype(jnp.int32)
    padded = ((counts + MOE_BLOCK - 1) // MOE_BLOCK) * MOE_BLOCK
    pad_ends = jnp.cumsum(padded)
    pad_starts = pad_ends - padded
    n_blocks = (N * TOP_K) // MOE_BLOCK + N_EXPERTS
    dest = rank
    for e in range(N_EXPERTS):
        dest = dest + jnp.where(idx == e, pad_starts[e], 0)
    blk_start = jnp.arange(n_blocks, dtype=jnp.int32) * MOE_BLOCK
    blk_e = jnp.minimum(jnp.sum((pad_ends[None, :] <= blk_start[:, None]).astype(jnp.int32), axis=1),
                        N_EXPERTS - 1)
    nb_used = (pad_ends[-1:] // MOE_BLOCK).astype(jnp.int32)

    xs = _sc_scatter_rows(xp.reshape(N, D // 2), dest, n_blocks * MOE_BLOCK)
    ys = _expert_call(blk_e, nb_used, xs, w_moe1[0].astype(_BF), b_moe1[0][:, None, :],
                      w_moe2[0].astype(_BF), b_moe2[0][:, None, :])
    yg = _sc_gather_rows(ys, dest.reshape(TOP_K * N))
    out = _combine_call(h1.reshape(N, D), gates.T, norm_final_g[None, :], yg)
    return out.reshape(B, S, D)
```

```python
import functools

import numpy as np
import jax
import jax.numpy as jnp
from jax import lax
from jax.experimental import pallas as pl
from jax.experimental.pallas import tpu as pltpu
from jax.experimental.pallas import tpu_sc as plsc

D_MODEL = 1024
HEAD_DIM = 64
N_Q_HEADS = 8
N_KV_HEADS = 2
ATTN_WIDTH = N_Q_HEADS * HEAD_DIM
KV_WIDTH = N_KV_HEADS * HEAD_DIM
WINDOW = 128
ROPE_THETA = 10000.0
HGRN_HEADS = 4
HGRN_DK = 128
HGRN_WIDTH = HGRN_HEADS * HGRN_DK
N_EXPERTS = 32
TOP_K = 4
D_EXPERT = 1024
SWIGLU_ALPHA = 1.702
SWIGLU_LIMIT = 7.0
MOE_BLOCK = 512
NORM_EPS = 1e-5

_OFF_Q = 0
_OFF_K = _OFF_Q + ATTN_WIDTH
_OFF_V = _OFF_K + KV_WIDTH
_OFF_HQ = _OFF_V + KV_WIDTH
_OFF_HF = _OFF_HQ + HGRN_WIDTH
_OFF_HI = _OFF_HF + HGRN_WIDTH
_OFF_HG = _OFF_HI + HGRN_WIDTH
_OFF_GA = _OFF_HG + HGRN_WIDTH
_OFF_GH = _OFF_GA + D_MODEL
IN_COLS = _OFF_GH + D_MODEL

LANES = 128
SEQ_TILE = 512
HGRN_CHUNK = 256
HGRN_DIAG = 1
COMBINE_TILE = 512
SC_CORES = 2
SC_WORKERS = 32
SC_CHUNK = 128
VMEM_LIMIT = 56 * 1024 * 1024

_BF = jnp.bfloat16
_F32 = jnp.float32


def _dot(a, b):
    return jnp.dot(a, b, preferred_element_type=_F32)


def _dot_nt(a, b, precision=None):
    return lax.dot_general(a, b, (((1,), (1,)), ((), ())), precision=precision,
                           preferred_element_type=_F32)


def _dot_tn(a, b):
    return lax.dot_general(a, b, (((0,), (0,)), ((), ())), preferred_element_type=_F32)


def _rowsum(x):
    return jnp.sum(x, axis=1, keepdims=True)


def _rms(x, g):
    ms = _rowsum(x * x) * (1.0 / x.shape[1])
    return x * lax.rsqrt(ms + NORM_EPS) * g


def _pack_bf16_pairs(x):
    n = x.shape[1] // 2
    lo = lax.bitcast_convert_type(x[:, :n].astype(_BF).astype(_F32), jnp.uint32)
    hi = lax.bitcast_convert_type(x[:, n:].astype(_BF).astype(_F32), jnp.uint32)
    return (lo >> 16) | (hi & jnp.uint32(0xFFFF0000))


def _unpack_bf16_pairs(u):
    lo = lax.bitcast_convert_type(u << 16, _F32)
    hi = lax.bitcast_convert_type(u & jnp.uint32(0xFFFF0000), _F32)
    return jnp.concatenate([lo, hi], axis=1)


def _mixer_kernel(sinks_ref, x_ref, cos_ref, sin_ref, gmix_ref, w_in_ref, b_in_ref, lb_ref,
                  hgn_ref, lvl_ref, w_ua_ref, w_uh_ref, w_out_ref, gffn_ref, w_rt_ref, b_r_ref,
                  h1_ref, xp_ref, idx_ref, rank_ref, gate_ref, cnt_ref,
                  kc_ref, vc_ref, st_ref, cnt_sc, ya_ref, o_ref):
    T = SEQ_TILE
    b = pl.program_id(0)
    s = pl.program_id(1)

    @pl.when(jnp.logical_and(b == 0, s == 0))
    def _():
        cnt_sc[...] = jnp.zeros_like(cnt_sc)

    @pl.when(s == 0)
    def _():
        kc_ref[...] = jnp.zeros_like(kc_ref)
        vc_ref[...] = jnp.zeros_like(vc_ref)
        st_ref[...] = jnp.zeros_like(st_ref)

    x = x_ref[0]
    xn = _rms(x, gmix_ref[...]).astype(_BF)

    def proj(off, width):
        return _dot(xn, w_in_ref[:, off:off + width]) + b_in_ref[:, off:off + width]

    cos = cos_ref[...]
    sin = sin_ref[...]

    def rope(t):
        return t * cos + pltpu.roll(t, 64, axis=1) * sin

    zq = proj(_OFF_Q, ATTN_WIDTH)
    scale = HEAD_DIM ** -0.5
    q_pairs = [(rope(zq[:, LANES * p:LANES * (p + 1)]) * scale).astype(_BF) for p in range(4)]
    k_rot = rope(proj(_OFF_K, KV_WIDTH))
    v_new = proj(_OFF_V, KV_WIDTH)

    k_ext = jnp.concatenate([kc_ref[...], k_rot], axis=0)
    v_ext = jnp.concatenate([vc_ref[...], v_new], axis=0)
    kc_ref[...] = k_rot[T - WINDOW:, :]
    vc_ref[...] = v_new[T - WINDOW:, :]

    lane = lax.broadcasted_iota(jnp.int32, (1, LANES), 1)
    slot_a = (lane % 64) < 32
    lane_lo = lane < 64
    k_r32 = pltpu.roll(k_ext, 32, axis=1)
    k_r96 = pltpu.roll(k_ext, 96, axis=1)
    v_r64 = pltpu.roll(v_ext, 64, axis=1)
    zero = jnp.zeros_like(k_ext)
    k_var = [(jnp.where(slot_a, k_ext, zero).astype(_BF), jnp.where(slot_a, zero, k_r32).astype(_BF)),
             (jnp.where(slot_a, k_r96, zero).astype(_BF), jnp.where(slot_a, zero, k_ext).astype(_BF))]
    v_var = [(jnp.where(lane_lo, v_ext, zero).astype(_BF), jnp.where(lane_lo, zero, v_r64).astype(_BF)),
             (jnp.where(lane_lo, v_r64, zero).astype(_BF), jnp.where(lane_lo, zero, v_ext).astype(_BF))]

    qi = lax.broadcasted_iota(jnp.int32, (2 * WINDOW, 2 * WINDOW), 0) % WINDOW
    kj = lax.broadcasted_iota(jnp.int32, (2 * WINDOW, 2 * WINDOW), 1)
    band = jnp.logical_and(kj > qi, kj <= qi + WINDOW)
    row_top = lax.broadcasted_iota(jnp.int32, (2 * WINDOW, 1), 0) < WINDOW
    neg_inf = jnp.float32(-jnp.inf)

    for n in range(T // WINDOW):
        if n == 0:
            ok = jnp.logical_and(band, jnp.logical_or(kj >= WINDOW, s > 0))
        else:
            ok = band
        r0 = n * WINDOW
        for j in range(N_KV_HEADS):
            q2 = jnp.concatenate([q_pairs[2 * j][r0:r0 + WINDOW], q_pairs[2 * j + 1][r0:r0 + WINDOW]], axis=0)
            kcat = jnp.concatenate([k_var[j][0][r0:r0 + 2 * WINDOW], k_var[j][1][r0:r0 + 2 * WINDOW]], axis=0)
            vcat = jnp.concatenate([v_var[j][0][r0:r0 + 2 * WINDOW], v_var[j][1][r0:r0 + 2 * WINDOW]], axis=0)
            sc = _dot_nt(q2, kcat)
            ps, rs = [], []
            for half in range(2):
                sh = jnp.where(ok, sc[:, 2 * WINDOW * half:2 * WINDOW * (half + 1)], neg_inf)
                snk = jnp.where(row_top, sinks_ref[4 * j + half], sinks_ref[4 * j + 2 + half])
                m = jnp.maximum(jnp.max(sh, axis=1, keepdims=True), snk)
                p = jnp.exp(sh - m)
                den = _rowsum(p) + jnp.exp(snk - m)
                ps.append(p.astype(_BF))
                rs.append(1.0 / den)
            o = _dot(jnp.concatenate(ps, axis=1), vcat)
            o = o * jnp.where(lane_lo, rs[0], rs[1])
            ya_ref[r0:r0 + WINDOW, LANES * (2 * j):LANES * (2 * j + 1)] = o[:WINDOW]
            ya_ref[r0:r0 + WINDOW, LANES * (2 * j + 1):LANES * (2 * j + 2)] = o[WINDOW:]

    HW = HGRN_WIDTH
    a0 = lb_ref[0:1, :]
    a1 = lb_ref[1:2, :]
    am = jnp.maximum(a0, a1)
    e0 = jnp.exp(a0 - am)
    lb = e0 / (e0 + jnp.exp(a1 - am))
    f = lb + (1.0 - lb) * jax.nn.sigmoid(proj(_OFF_HF, HW))
    kk = 1.0 - f
    g = jnp.log(f)
    hq = proj(_OFF_HQ, HW)
    hv = proj(_OFF_HI, HW)

    row = lax.broadcasted_iota(jnp.int32, (T, 1), 0)
    c0 = HGRN_DIAG

    def rdown(t, d):
        return pltpu.roll(t, d, axis=0)

    def rup(t, d):
        return pltpu.roll(t, T - d, axis=0)

    lcum = g
    d = 1
    while d < c0:
        lcum = lcum + jnp.where((row % c0) >= d, rdown(lcum, d), 0.0)
        d *= 2
    tot = jnp.where((row % c0) == c0 - 1, lcum, 0.0)
    d = 1
    while d < c0:
        tot = tot + rup(tot, d)
        d *= 2

    def head(t, h):
        return t[:, HGRN_DK * h:HGRN_DK * (h + 1)]

    qk0 = hq * kk
    o_parts = [_rowsum(head(qk0, h)) * head(hv, h) for h in range(HGRN_HEADS)]
    for d in range(1, c0):
        ks = rdown(kk, d)
        ls = rdown(lcum, d)
        vs = rdown(hv, d)
        pd = hq * ks * jnp.exp(jnp.minimum(lcum - ls, 0.0))
        okd = (row % c0) >= d
        for h in range(HGRN_HEADS):
            a = jnp.where(okd, _rowsum(head(pd, h)), 0.0)
            o_parts[h] = o_parts[h] + a * head(vs, h)

    C = HGRN_CHUNK
    levels = []
    bsz = c0
    lb_cum, lb_tot = lcum, tot
    while bsz < C:
        odd = (row % (2 * bsz)) >= bsz
        ex = jnp.exp(jnp.where(odd, lb_cum, lb_tot - lb_cum))
        ql = jnp.where(odd, hq * ex, 0.0).astype(_BF)
        kl = jnp.where(odd, 0.0, kk * ex).astype(_BF)
        levels.append((bsz, ql, kl))
        prev_tot = rdown(lb_tot, bsz)
        lb_cum = lb_cum + jnp.where(odd, prev_tot, 0.0)
        lb_tot = lb_tot + jnp.where(odd, prev_tot, rup(lb_tot, bsz))
        bsz *= 2
    q_in = (hq * jnp.exp(lb_cum)).astype(_BF)
    k_st = (kk * jnp.exp(lb_tot - lb_cum)).astype(_BF)
    hv_b = hv.astype(_BF)

    lvl = lvl_ref[...]

    for c in range(T // C):
        rs_ = slice(c * C, (c + 1) * C)
        dec = jnp.exp(lb_tot[c * C:c * C + 1, :])
        for h in range(HGRN_HEADS):
            cs_ = slice(HGRN_DK * h, HGRN_DK * (h + 1))
            st = st_ref[h]
            vc = hv_b[rs_, cs_]
            inter = _dot_nt(q_in[rs_, cs_], st.astype(_BF))
            amat = jnp.zeros((C, C), _F32)
            for li, (bs, ql, kl) in enumerate(levels):
                amat = jnp.where(lvl == li, _dot_nt(ql[rs_, cs_], kl[rs_, cs_]), amat)
            intra = _dot(amat.astype(_BF), vc)
            st_ref[h] = st * dec[:, cs_] + _dot_tn(vc, k_st[rs_, cs_])
            o_ref[rs_, cs_] = inter + intra + o_parts[h][rs_]

    o = o_ref[...]
    hg = proj(_OFF_HG, HW)
    yh_parts = []
    for h in range(HGRN_HEADS):
        oh = head(o, h)
        ms = _rowsum(oh * oh) * (1.0 / HGRN_DK)
        yh_parts.append(oh * lax.rsqrt(ms + NORM_EPS))
    yh = jnp.concatenate(yh_parts, axis=1) * hgn_ref[...]
    yh = (yh * (hg * jax.nn.sigmoid(hg))).astype(_BF)

    up_a = _dot(ya_ref[...].astype(_BF), w_ua_ref[...])
    up_h = _dot(yh, w_uh_ref[...])
    merged = jax.nn.sigmoid(proj(_OFF_GA, D_MODEL)) * up_a + jax.nn.sigmoid(proj(_OFF_GH, D_MODEL)) * up_h
    h1 = x + _dot(merged.astype(_BF), w_out_ref[...])
    h1_ref[0] = h1

    xn2 = _rms(h1, gffn_ref[...])
    xp_ref[0] = _pack_bf16_pairs(xn2)
    x_hi = xn2.astype(_BF)
    x_lo = (xn2 - x_hi.astype(_F32)).astype(_BF)
    w_rt = w_rt_ref[...]
    w_hi = w_rt.astype(_BF)
    w_lo = (w_rt - w_hi.astype(_F32)).astype(_BF)
    logits = (_dot_nt(w_hi, x_hi) + (_dot_nt(w_hi, x_lo) + _dot_nt(w_lo, x_hi))) + b_r_ref[...]
    eidx = lax.broadcasted_iota(jnp.int32, (N_EXPERTS, T), 0)
    vals, sels, ohs = [], [], []
    l = logits
    for _ in range(TOP_K):
        m = jnp.max(l, axis=0, keepdims=True)
        sel = jnp.min(jnp.where(l == m, eidx, N_EXPERTS), axis=0, keepdims=True)
        oh = eidx == sel
        vals.append(m)
        sels.append(sel)
        ohs.append(oh)
        l = jnp.where(oh, neg_inf, l)
    es = [jnp.exp(v - vals[0]) for v in vals]
    den = es[0] + es[1] + es[2] + es[3]
    chosen = jnp.zeros((N_EXPERTS, T), _F32)
    for oh in ohs:
        chosen = chosen + jnp.where(oh, 1.0, 0.0)
    ui = lax.broadcasted_iota(jnp.int32, (T, T), 0)
    uj = lax.broadcasted_iota(jnp.int32, (T, T), 1)
    upper = jnp.where(ui < uj, 1.0, 0.0).astype(_BF)
    before = _dot(chosen.astype(_BF), upper) + cnt_sc[:, 0:1]
    for k in range(TOP_K):
        idx_ref[k:k + 1, :] = sels[k]
        gate_ref[k:k + 1, :] = es[k] / den
        rank_ref[k:k + 1, :] = jnp.sum(jnp.where(ohs[k], before, 0.0), axis=0, keepdims=True).astype(jnp.int32)
    cnt_sc[...] = cnt_sc[...] + _rowsum(chosen)
    cnt_ref[...] = cnt_sc[...]


def _const_spec(shape):
    return pl.BlockSpec(shape, lambda b, s: (0,) * len(shape), pipeline_mode=pl.Buffered(1))


def _mixer_call(x, sinks, cos_t, sin_t, gmix, w_in, b_in, lb, hgn, lvl, w_ua, w_uh, w_out, gffn, w_rt, b_r):
    B, S, _ = x.shape
    T = SEQ_TILE
    N = B * S
    nS = S // T
    tok_spec = lambda w: pl.BlockSpec((4, T), lambda b, s: (0, b * nS + s))
    out_shape = (
        jax.ShapeDtypeStruct((B, S, D_MODEL), _F32),
        jax.ShapeDtypeStruct((B, S, D_MODEL // 2), jnp.uint32),
        jax.ShapeDtypeStruct((TOP_K, N), jnp.int32),
        jax.ShapeDtypeStruct((TOP_K, N), jnp.int32),
        jax.ShapeDtypeStruct((TOP_K, N), _F32),
        jax.ShapeDtypeStruct((N_EXPERTS, LANES), _F32),
    )
    in_specs = [
        pl.BlockSpec(memory_space=pltpu.SMEM),
        pl.BlockSpec((1, T, D_MODEL), lambda b, s: (b, s, 0)),
        pl.BlockSpec((T, LANES), lambda b, s: (s, 0)),
        pl.BlockSpec((T, LANES), lambda b, s: (s, 0)),
        _const_spec((1, D_MODEL)),
        _const_spec((D_MODEL, IN_COLS)),
        _const_spec((1, IN_COLS)),
        _const_spec((2, HGRN_WIDTH)),
        _const_spec((1, HGRN_WIDTH)),
        _const_spec((HGRN_CHUNK, HGRN_CHUNK)),
        _const_spec((ATTN_WIDTH, D_MODEL)),
        _const_spec((HGRN_WIDTH, D_MODEL)),
        _const_spec((D_MODEL, D_MODEL)),
        _const_spec((1, D_MODEL)),
        _const_spec((N_EXPERTS, D_MODEL)),
        _const_spec((N_EXPERTS, 1)),
    ]
    out_specs = (
        pl.BlockSpec((1, T, D_MODEL), lambda b, s: (b, s, 0)),
        pl.BlockSpec((1, T, D_MODEL // 2), lambda b, s: (b, s, 0)),
        tok_spec(0), tok_spec(0), tok_spec(0),
        pl.BlockSpec((N_EXPERTS, LANES), lambda b, s: (0, 0)),
    )
    scratch = [
        pltpu.VMEM((WINDOW, KV_WIDTH), _F32),
        pltpu.VMEM((WINDOW, KV_WIDTH), _F32),
        pltpu.VMEM((HGRN_HEADS, HGRN_DK, HGRN_DK), _F32),
        pltpu.VMEM((N_EXPERTS, LANES), _F32),
        pltpu.VMEM((T, ATTN_WIDTH), _F32),
        pltpu.VMEM((T, HGRN_WIDTH), _F32),
    ]
    return pl.pallas_call(
        _mixer_kernel,
        grid=(B, nS),
        in_specs=in_specs,
        out_specs=out_specs,
        out_shape=out_shape,
        scratch_shapes=scratch,
        compiler_params=pltpu.CompilerParams(
            dimension_semantics=("arbitrary", "arbitrary"), vmem_limit_bytes=VMEM_LIMIT),
        name="mixer",
    )(sinks, x, cos_t, sin_t, gmix, w_in, b_in, lb, hgn, lvl, w_ua, w_uh, w_out, gffn, w_rt, b_r)


def _sc_mesh():
    return plsc.VectorSubcoreMesh(core_axis_name="c", subcore_axis_name="s")


def _sc_worker_id():
    return lax.axis_index("s") * SC_CORES + lax.axis_index("c")


def _sc_scatter_rows(x, dest, n_rows):
    N, C = x.shape
    K = dest.shape[0]
    n_chunks = N // SC_CHUNK
    per_worker = n_chunks // SC_WORKERS
    dest3 = dest.reshape(K, n_chunks, SC_CHUNK)

    @functools.partial(
        pl.kernel, out_type=jax.ShapeDtypeStruct((n_rows, C), x.dtype), mesh=_sc_mesh(),
        scratch_types=[pltpu.VMEM((K, SC_CHUNK), jnp.int32), pltpu.VMEM((SC_CHUNK, C), x.dtype),
                       pltpu.SemaphoreType.DMA])
    def scatter(x_hbm, i_hbm, o_hbm, idx_v, rows_v, sem):
        base = _sc_worker_id() * per_worker

        @pl.loop(0, per_worker)
        def _(j):
            c = base + j
            off = pl.multiple_of(c * SC_CHUNK, SC_CHUNK)
            for k in range(K):
                pltpu.sync_copy(i_hbm.at[k, c], idx_v.at[k])
            pltpu.sync_copy(x_hbm.at[pl.ds(off, SC_CHUNK)], rows_v)
            for k in range(K):
                pltpu.async_copy(rows_v, o_hbm.at[idx_v.at[k]], sem).wait()

    return scatter(x, dest3)


def _sc_gather_rows(table, idx):
    M = idx.shape[0]
    C = table.shape[1]
    per_worker = M // SC_WORKERS
    n_chunks = per_worker // SC_CHUNK

    @functools.partial(
        pl.kernel, out_type=jax.ShapeDtypeStruct((M, C), table.dtype), mesh=_sc_mesh(),
        scratch_types=[pltpu.VMEM((SC_CHUNK,), jnp.int32), pltpu.VMEM((SC_CHUNK, C), table.dtype),
                       pltpu.SemaphoreType.DMA])
    def gather(t_hbm, i_hbm, o_hbm, idx_v, rows_v, sem):
        base = _sc_worker_id() * per_worker

        @pl.loop(0, n_chunks)
        def _(j):
            off = pl.multiple_of(base + j * SC_CHUNK, SC_CHUNK)
            pltpu.sync_copy(i_hbm.at[pl.ds(off, SC_CHUNK)], idx_v)
            pltpu.async_copy(t_hbm.at[idx_v], rows_v, sem).wait()
            pltpu.sync_copy(rows_v, o_hbm.at[pl.ds(off, SC_CHUNK)])

    return gather(table, idx)


def _expert_kernel(blk_e_ref, nb_ref, xs_ref, w1_ref, b1_ref, w2_ref, b2_ref, ys_ref):
    i = pl.program_id(0)

    @pl.when(i < nb_ref[0])
    def _():
        xb = _unpack_bf16_pairs(xs_ref[...]).astype(_BF)
        h = _dot(xb, w1_ref[0]) + b1_ref[0]
        glu = jnp.minimum(h[:, :D_EXPERT], SWIGLU_LIMIT)
        lin = jnp.clip(h[:, D_EXPERT:], -SWIGLU_LIMIT, SWIGLU_LIMIT)
        act = glu * jax.nn.sigmoid(SWIGLU_ALPHA * glu) * (lin + 1.0)
        y = _dot(act.astype(_BF), w2_ref[0]) + b2_ref[0]
        ys_ref[...] = _pack_bf16_pairs(y)

    @pl.when(i >= nb_ref[0])
    def _():
        ys_ref[...] = jnp.zeros_like(ys_ref)


def _expert_call(blk_e, nb_used, xs, w1, b1, w2, b2):
    n_blocks = xs.shape[0] // MOE_BLOCK
    M = MOE_BLOCK
    grid_spec = pltpu.PrefetchScalarGridSpec(
        num_scalar_prefetch=2,
        grid=(n_blocks,),
        in_specs=[
            pl.BlockSpec((M, D_MODEL // 2), lambda i, be, nb: (jnp.minimum(i, nb[0] - 1), 0)),
            pl.BlockSpec((1, D_MODEL, 2 * D_EXPERT), lambda i, be, nb: (be[i], 0, 0)),
            pl.BlockSpec((1, 1, 2 * D_EXPERT), lambda i, be, nb: (be[i], 0, 0)),
            pl.BlockSpec((1, D_EXPERT, D_MODEL), lambda i, be, nb: (be[i], 0, 0)),
            pl.BlockSpec((1, 1, D_MODEL), lambda i, be, nb: (be[i], 0, 0)),
        ],
        out_specs=pl.BlockSpec((M, D_MODEL // 2), lambda i, be, nb: (i, 0)),
    )
    return pl.pallas_call(
        _expert_kernel,
        grid_spec=grid_spec,
        out_shape=jax.ShapeDtypeStruct(xs.shape, jnp.uint32),
        compiler_params=pltpu.CompilerParams(
            dimension_semantics=("arbitrary",), vmem_limit_bytes=VMEM_LIMIT),
        name="experts",
    )(blk_e, nb_used, xs, w1, b1, w2, b2)


def _combine_kernel(h1_ref, gate_ref, gfin_ref, y0_ref, y1_ref, y2_ref, y3_ref, out_ref):
    acc = h1_ref[...]
    gates = gate_ref[...]
    for k, y_ref in enumerate((y0_ref, y1_ref, y2_ref, y3_ref)):
        acc = acc + gates[:, k:k + 1] * _unpack_bf16_pairs(y_ref[...])
    out_ref[...] = _rms(acc, gfin_ref[...])


def _combine_call(h1, gates_t, gfin, yg):
    N = h1.shape[0]
    T = COMBINE_TILE
    nT = N // T
    y_specs = [pl.BlockSpec((T, D_MODEL // 2), functools.partial(lambda i, k: (k * nT + i, 0), k=k))
               for k in range(TOP_K)]
    return pl.pallas_call(
        _combine_kernel,
        grid=(nT,),
        in_specs=[
            pl.BlockSpec((T, D_MODEL), lambda i: (i, 0)),
            pl.BlockSpec((T, TOP_K), lambda i: (i, 0)),
            pl.BlockSpec((1, D_MODEL), lambda i: (0, 0)),
        ] + y_specs,
        out_specs=pl.BlockSpec((T, D_MODEL), lambda i: (i, 0)),
        out_shape=jax.ShapeDtypeStruct((N, D_MODEL), _F32),
        compiler_params=pltpu.CompilerParams(dimension_semantics=("arbitrary",)),
        name="combine",
    )(h1, gates_t, gfin, yg, yg, yg, yg)


def _qk_column_permutation():
    half = HEAD_DIM // 2
    r = np.arange(half)

    def pair(base, ha, hb):
        return np.concatenate([base + ha * HEAD_DIM + r, base + hb * HEAD_DIM + r,
                               base + ha * HEAD_DIM + half + r, base + hb * HEAD_DIM + half + r])

    q = np.concatenate([pair(_OFF_Q, 2 * p, 2 * p + 1) for p in range(N_Q_HEADS // 2)])
    k = pair(_OFF_K, 0, 1)
    return np.concatenate([q, k, np.arange(_OFF_V, IN_COLS)])


def _hgrn_level_table():
    t = np.arange(HGRN_CHUNK)[:, None]
    u = np.arange(HGRN_CHUNK)[None, :]
    first = HGRN_DIAG.bit_length() - 1
    top = np.floor(np.log2(np.maximum(t ^ u, 1))).astype(np.int32)
    return np.where((t > u) & (top >= first), top - first, -1).astype(np.int32)


def _rope_tables(S):
    half = HEAD_DIM // 2
    inv_freq = ROPE_THETA ** (-(jnp.arange(half, dtype=_F32) * 2.0 / HEAD_DIM))
    ang = jnp.arange(S, dtype=_F32)[:, None] * inv_freq[None, :]
    cos = jnp.tile(jnp.cos(ang), (1, 4))
    sin = jnp.tile(jnp.sin(ang), (1, 4))
    sign = jnp.where(jnp.arange(LANES) < 64, -1.0, 1.0).astype(_F32)
    return cos, sin * sign[None, :]


def kernel(x, norm_mix_g, w_in, b_in, attn_sinks, hgrn_lb, hgrn_norm_g, w_up_attn, w_up_hgrn, w_out,
           norm_ffn_g, w_router, b_router, w_moe1, b_moe1, w_moe2, b_moe2, norm_final_g):
    B, S, D = x.shape
    N = B * S
    assert D == D_MODEL and S % SEQ_TILE == 0 and SEQ_TILE % WINDOW == 0
    assert N % COMBINE_TILE == 0 and N % (SC_WORKERS * SC_CHUNK) == 0 and norm_mix_g.shape[0] == 1

    perm = _qk_column_permutation()
    cos_t, sin_t = _rope_tables(S)
    h1, xp, idx, rank, gates, cnt = _mixer_call(
        x, attn_sinks[0], cos_t, sin_t, norm_mix_g,
        w_in[0][:, perm].astype(_BF), b_in[:, perm], hgrn_lb, hgrn_norm_g, jnp.asarray(_hgrn_level_table()),
        w_up_attn[0].astype(_BF), w_up_hgrn[0].astype(_BF), w_out[0].astype(_BF),
        norm_ffn_g, w_router[0].T, b_router[0][:, None])

    counts = cnt[:, 0].astype(jnp.int32)
    padded = ((counts + MOE_BLOCK - 1) // MOE_BLOCK) * MOE_BLOCK
    pad_ends = jnp.cumsum(padded)
    pad_starts = pad_ends - padded
    n_blocks = (N * TOP_K) // MOE_BLOCK + N_EXPERTS
    dest = rank
    for e in range(N_EXPERTS):
        dest = dest + jnp.where(idx == e, pad_starts[e], 0)
    blk_start = jnp.arange(n_blocks, dtype=jnp.int32) * MOE_BLOCK
    blk_e = jnp.minimum(jnp.sum((pad_ends[None, :] <= blk_start[:, None]).astype(jnp.int32), axis=1),
                        N_EXPERTS - 1)
    nb_used = (pad_ends[-1:] // MOE_BLOCK).astype(jnp.int32)

    xs = _sc_scatter_rows(xp.reshape(N, D // 2), dest, n_blocks * MOE_BLOCK)
    ys = _expert_call(blk_e, nb_used, xs, w_moe1[0].astype(_BF), b_moe1[0][:, None, :],
                      w_moe2[0].astype(_BF), b_moe2[0][:, None, :])
    yg = _sc_gather_rows(ys, dest.reshape(TOP_K * N))
    out = _combine_call(h1.reshape(N, D), gates.T, norm_final_g[None, :], yg)
    return out.reshape(B, S, D)
```

```python
import functools

import numpy as np
import jax
import jax.numpy as jnp
from jax import lax
from jax.experimental import pallas as pl
from jax.experimental.pallas import tpu as pltpu
from jax.experimental.pallas import tpu_sc as plsc

D_MODEL = 1024
HEAD_DIM = 64
N_Q_HEADS = 8
N_KV_HEADS = 2
ATTN_WIDTH = N_Q_HEADS * HEAD_DIM
KV_WIDTH = N_KV_HEADS * HEAD_DIM
WINDOW = 128
ROPE_THETA = 10000.0
HGRN_HEADS = 4
HGRN_DK = 128
HGRN_WIDTH = HGRN_HEADS * HGRN_DK
N_EXPERTS = 32
TOP_K = 4
D_EXPERT = 1024
SWIGLU_ALPHA = 1.702
SWIGLU_LIMIT = 7.0
MOE_BLOCK = 512
NORM_EPS = 1e-5

_OFF_Q = 0
_OFF_K = _OFF_Q + ATTN_WIDTH
_OFF_V = _OFF_K + KV_WIDTH
_OFF_HQ = _OFF_V + KV_WIDTH
_OFF_HF = _OFF_HQ + HGRN_WIDTH
_OFF_HI = _OFF_HF + HGRN_WIDTH
_OFF_HG = _OFF_HI + HGRN_WIDTH
_OFF_GA = _OFF_HG + HGRN_WIDTH
_OFF_GH = _OFF_GA + D_MODEL
IN_COLS = _OFF_GH + D_MODEL

LANES = 128
SEQ_TILE = 512
HGRN_CHUNK = 256
HGRN_DIAG = 1
COMBINE_TILE = 512
WEIGHT_CAST_ROWS = 128
SC_CORES = 2
SC_WORKERS = 32
SC_CHUNK = 128
VMEM_LIMIT = 56 * 1024 * 1024

_BF = jnp.bfloat16
_F32 = jnp.float32


def _dot(a, b):
    return jnp.dot(a, b, preferred_element_type=_F32)


def _dot_nt(a, b, precision=None):
    return lax.dot_general(a, b, (((1,), (1,)), ((), ())), precision=precision,
                           preferred_element_type=_F32)


def _dot_tn(a, b):
    return lax.dot_general(a, b, (((0,), (0,)), ((), ())), preferred_element_type=_F32)


def _rowsum(x):
    return jnp.sum(x, axis=1, keepdims=True)


def _rms(x, g):
    ms = _rowsum(x * x) * (1.0 / x.shape[1])
    return x * lax.rsqrt(ms + NORM_EPS) * g


def _pack_bf16_pairs(x):
    n = x.shape[1] // 2
    lo = lax.bitcast_convert_type(x[:, :n].astype(_BF).astype(_F32), jnp.uint32)
    hi = lax.bitcast_convert_type(x[:, n:].astype(_BF).astype(_F32), jnp.uint32)
    return (lo >> 16) | (hi & jnp.uint32(0xFFFF0000))


def _unpack_bf16_pairs(u):
    lo = lax.bitcast_convert_type(u << 16, _F32)
    hi = lax.bitcast_convert_type(u & jnp.uint32(0xFFFF0000), _F32)
    return jnp.concatenate([lo, hi], axis=1)


def _mixer_kernel(sinks_ref, x_ref, cos_ref, sin_ref, gmix_ref, w_in_ref, b_in_ref, lb_ref,
                  hgn_ref, lvl_ref, w_ua_ref, w_uh_ref, w_out_ref, gffn_ref, w_rt_ref, b_r_ref,
                  h1_ref, xp_ref, idx_ref, rank_ref, gate_ref, cnt_ref,
                  kc_ref, vc_ref, st_ref, cnt_sc, ya_ref, o_ref):
    T = SEQ_TILE
    b = pl.program_id(0)
    s = pl.program_id(1)

    @pl.when(jnp.logical_and(b == 0, s == 0))
    def _():
        cnt_sc[...] = jnp.zeros_like(cnt_sc)

    @pl.when(s == 0)
    def _():
        kc_ref[...] = jnp.zeros_like(kc_ref)
        vc_ref[...] = jnp.zeros_like(vc_ref)
        st_ref[...] = jnp.zeros_like(st_ref)

    x = x_ref[0]
    xn = _rms(x, gmix_ref[...]).astype(_BF)

    def proj(off, width):
        return _dot(xn, w_in_ref[:, off:off + width]) + b_in_ref[:, off:off + width]

    cos = cos_ref[...]
    sin = sin_ref[...]

    def rope(t):
        return t * cos + pltpu.roll(t, 64, axis=1) * sin

    zq = proj(_OFF_Q, ATTN_WIDTH)
    scale = HEAD_DIM ** -0.5
    q_pairs = [(rope(zq[:, LANES * p:LANES * (p + 1)]) * scale).astype(_BF) for p in range(4)]
    k_rot = rope(proj(_OFF_K, KV_WIDTH))
    v_new = proj(_OFF_V, KV_WIDTH)

    k_ext = jnp.concatenate([kc_ref[...], k_rot], axis=0)
    v_ext = jnp.concatenate([vc_ref[...], v_new], axis=0)
    kc_ref[...] = k_rot[T - WINDOW:, :]
    vc_ref[...] = v_new[T - WINDOW:, :]

    lane = lax.broadcasted_iota(jnp.int32, (1, LANES), 1)
    slot_a = (lane % 64) < 32
    lane_lo = lane < 64
    k_r32 = pltpu.roll(k_ext, 32, axis=1)
    k_r96 = pltpu.roll(k_ext, 96, axis=1)
    v_r64 = pltpu.roll(v_ext, 64, axis=1)
    zero = jnp.zeros_like(k_ext)
    k_var = [(jnp.where(slot_a, k_ext, zero).astype(_BF), jnp.where(slot_a, zero, k_r32).astype(_BF)),
             (jnp.where(slot_a, k_r96, zero).astype(_BF), jnp.where(slot_a, zero, k_ext).astype(_BF))]
    v_var = [(jnp.where(lane_lo, v_ext, zero).astype(_BF), jnp.where(lane_lo, zero, v_r64).astype(_BF)),
             (jnp.where(lane_lo, v_r64, zero).astype(_BF), jnp.where(lane_lo, zero, v_ext).astype(_BF))]

    qi = lax.broadcasted_iota(jnp.int32, (2 * WINDOW, 2 * WINDOW), 0) % WINDOW
    kj = lax.broadcasted_iota(jnp.int32, (2 * WINDOW, 2 * WINDOW), 1)
    band = jnp.logical_and(kj > qi, kj <= qi + WINDOW)
    row_top = lax.broadcasted_iota(jnp.int32, (2 * WINDOW, 1), 0) < WINDOW
    neg_inf = jnp.float32(-jnp.inf)

    for n in range(T // WINDOW):
        if n == 0:
            ok = jnp.logical_and(band, jnp.logical_or(kj >= WINDOW, s > 0))
        else:
            ok = band
        r0 = n * WINDOW
        for j in range(N_KV_HEADS):
            q2 = jnp.concatenate([q_pairs[2 * j][r0:r0 + WINDOW], q_pairs[2 * j + 1][r0:r0 + WINDOW]], axis=0)
            kcat = jnp.concatenate([k_var[j][0][r0:r0 + 2 * WINDOW], k_var[j][1][r0:r0 + 2 * WINDOW]], axis=0)
            vcat = jnp.concatenate([v_var[j][0][r0:r0 + 2 * WINDOW], v_var[j][1][r0:r0 + 2 * WINDOW]], axis=0)
            sc = _dot_nt(q2, kcat)
            ps, rs = [], []
            for half in range(2):
                sh = jnp.where(ok, sc[:, 2 * WINDOW * half:2 * WINDOW * (half + 1)], neg_inf)
                snk = jnp.where(row_top, sinks_ref[4 * j + half], sinks_ref[4 * j + 2 + half])
                m = jnp.maximum(jnp.max(sh, axis=1, keepdims=True), snk)
                p = jnp.exp(sh - m)
                den = _rowsum(p) + jnp.exp(snk - m)
                ps.append(p.astype(_BF))
                rs.append(1.0 / den)
            o = _dot(jnp.concatenate(ps, axis=1), vcat)
            o = o * jnp.where(lane_lo, rs[0], rs[1])
            ya_ref[r0:r0 + WINDOW, LANES * (2 * j):LANES * (2 * j + 1)] = o[:WINDOW]
            ya_ref[r0:r0 + WINDOW, LANES * (2 * j + 1):LANES * (2 * j + 2)] = o[WINDOW:]

    HW = HGRN_WIDTH
    a0 = lb_ref[0:1, :]
    a1 = lb_ref[1:2, :]
    am = jnp.maximum(a0, a1)
    e0 = jnp.exp(a0 - am)
    lb = e0 / (e0 + jnp.exp(a1 - am))
    f = lb + (1.0 - lb) * jax.nn.sigmoid(proj(_OFF_HF, HW))
    kk = 1.0 - f
    g = jnp.log(f)
    hq = proj(_OFF_HQ, HW)
    hv = proj(_OFF_HI, HW)

    row = lax.broadcasted_iota(jnp.int32, (T, 1), 0)
    c0 = HGRN_DIAG

    def rdown(t, d):
        return pltpu.roll(t, d, axis=0)

    def rup(t, d):
        return pltpu.roll(t, T - d, axis=0)

    lcum = g
    d = 1
    while d < c0:
        lcum = lcum + jnp.where((row % c0) >= d, rdown(lcum, d), 0.0)
        d *= 2
    tot = jnp.where((row % c0) == c0 - 1, lcum, 0.0)
    d = 1
    while d < c0:
        tot = tot + rup(tot, d)
        d *= 2

    def head(t, h):
        return t[:, HGRN_DK * h:HGRN_DK * (h + 1)]

    qk0 = hq * kk
    o_parts = [_rowsum(head(qk0, h)) * head(hv, h) for h in range(HGRN_HEADS)]
    for d in range(1, c0):
        ks = rdown(kk, d)
        ls = rdown(lcum, d)
        vs = rdown(hv, d)
        pd = hq * ks * jnp.exp(jnp.minimum(lcum - ls, 0.0))
        okd = (row % c0) >= d
        for h in range(HGRN_HEADS):
            a = jnp.where(okd, _rowsum(head(pd, h)), 0.0)
            o_parts[h] = o_parts[h] + a * head(vs, h)

    C = HGRN_CHUNK
    levels = []
    bsz = c0
    lb_cum, lb_tot = lcum, tot
    while bsz < C:
        odd = (row % (2 * bsz)) >= bsz
        ex = jnp.exp(jnp.where(odd, lb_cum, lb_tot - lb_cum))
        ql = jnp.where(odd, hq * ex, 0.0).astype(_BF)
        kl = jnp.where(odd, 0.0, kk * ex).astype(_BF)
        levels.append((bsz, ql, kl))
        prev_tot = rdown(lb_tot, bsz)
        lb_cum = lb_cum + jnp.where(odd, prev_tot, 0.0)
        lb_tot = lb_tot + jnp.where(odd, prev_tot, rup(lb_tot, bsz))
        bsz *= 2
    q_in = (hq * jnp.exp(lb_cum)).astype(_BF)
    k_st = (kk * jnp.exp(lb_tot - lb_cum)).astype(_BF)
    hv_b = hv.astype(_BF)

    lvl = lvl_ref[...]

    for c in range(T // C):
        rs_ = slice(c * C, (c + 1) * C)
        dec = jnp.exp(lb_tot[c * C:c * C + 1, :])
        for h in range(HGRN_HEADS):
            cs_ = slice(HGRN_DK * h, HGRN_DK * (h + 1))
            st = st_ref[h]
            vc = hv_b[rs_, cs_]
            inter = _dot_nt(q_in[rs_, cs_], st.astype(_BF))
            amat = jnp.zeros((C, C), _F32)
            for li, (bs, ql, kl) in enumerate(levels):
                amat = jnp.where(lvl == li, _dot_nt(ql[rs_, cs_], kl[rs_, cs_]), amat)
            intra = _dot(amat.astype(_BF), vc)
            st_ref[h] = st * dec[:, cs_] + _dot_tn(vc, k_st[rs_, cs_])
            o_ref[rs_, cs_] = inter + intra + o_parts[h][rs_]

    o = o_ref[...]
    hg = proj(_OFF_HG, HW)
    yh_parts = []
    for h in range(HGRN_HEADS):
        oh = head(o, h)
        ms = _rowsum(oh * oh) * (1.0 / HGRN_DK)
        yh_parts.append(oh * lax.rsqrt(ms + NORM_EPS))
    yh = jnp.concatenate(yh_parts, axis=1) * hgn_ref[...]
    yh = (yh * (hg * jax.nn.sigmoid(hg))).astype(_BF)

    up_a = _dot(ya_ref[...].astype(_BF), w_ua_ref[...])
    up_h = _dot(yh, w_uh_ref[...])
    merged = jax.nn.sigmoid(proj(_OFF_GA, D_MODEL)) * up_a + jax.nn.sigmoid(proj(_OFF_GH, D_MODEL)) * up_h
    h1 = x + _dot(merged.astype(_BF), w_out_ref[...])
    h1_ref[0] = h1

    xn2 = _rms(h1, gffn_ref[...])
    xp_ref[0] = _pack_bf16_pairs(xn2)
    x_hi = xn2.astype(_BF)
    x_lo = (xn2 - x_hi.astype(_F32)).astype(_BF)
    w_rt = w_rt_ref[...]
    w_hi = w_rt.astype(_BF)
    w_lo = (w_rt - w_hi.astype(_F32)).astype(_BF)
    logits = (_dot_nt(w_hi, x_hi) + (_dot_nt(w_hi, x_lo) + _dot_nt(w_lo, x_hi))) + b_r_ref[...]
    eidx = lax.broadcasted_iota(jnp.int32, (N_EXPERTS, T), 0)
    vals, sels, ohs = [], [], []
    l = logits
    for _ in range(TOP_K):
        m = jnp.max(l, axis=0, keepdims=True)
        sel = jnp.min(jnp.where(l == m, eidx, N_EXPERTS), axis=0, keepdims=True)
        oh = eidx == sel
        vals.append(m)
        sels.append(sel)
        ohs.append(oh)
        l = jnp.where(oh, neg_inf, l)
    es = [jnp.exp(v - vals[0]) for v in vals]
    den = es[0] + es[1] + es[2] + es[3]
    chosen = jnp.zeros((N_EXPERTS, T), _F32)
    for oh in ohs:
        chosen = chosen + jnp.where(oh, 1.0, 0.0)
    ui = lax.broadcasted_iota(jnp.int32, (T, T), 0)
    uj = lax.broadcasted_iota(jnp.int32, (T, T), 1)
    upper = jnp.where(ui < uj, 1.0, 0.0).astype(_BF)
    before = _dot(chosen.astype(_BF), upper) + cnt_sc[:, 0:1]
    for k in range(TOP_K):
        idx_ref[k:k + 1, :] = sels[k]
        gate_ref[k:k + 1, :] = es[k] / den
        rank_ref[k:k + 1, :] = jnp.sum(jnp.where(ohs[k], before, 0.0), axis=0, keepdims=True).astype(jnp.int32)
    cnt_sc[...] = cnt_sc[...] + _rowsum(chosen)
    cnt_ref[...] = cnt_sc[...]


def _const_spec(shape):
    return pl.BlockSpec(shape, lambda b, s: (0,) * len(shape), pipeline_mode=pl.Buffered(1))


def _mixer_call(x, sinks, cos_t, sin_t, gmix, w_in, b_in, lb, hgn, lvl, w_ua, w_uh, w_out, gffn, w_rt, b_r):
    B, S, _ = x.shape
    T = SEQ_TILE
    N = B * S
    nS = S // T
    tok_spec = lambda w: pl.BlockSpec((4, T), lambda b, s: (0, b * nS + s))
    out_shape = (
        jax.ShapeDtypeStruct((B, S, D_MODEL), _F32),
        jax.ShapeDtypeStruct((B, S, D_MODEL // 2), jnp.uint32),
        jax.ShapeDtypeStruct((TOP_K, N), jnp.int32),
        jax.ShapeDtypeStruct((TOP_K, N), jnp.int32),
        jax.ShapeDtypeStruct((TOP_K, N), _F32),
        jax.ShapeDtypeStruct((N_EXPERTS, LANES), _F32),
    )
    in_specs = [
        pl.BlockSpec(memory_space=pltpu.SMEM),
        pl.BlockSpec((1, T, D_MODEL), lambda b, s: (b, s, 0)),
        pl.BlockSpec((T, LANES), lambda b, s: (s, 0)),
        pl.BlockSpec((T, LANES), lambda b, s: (s, 0)),
        _const_spec((1, D_MODEL)),
        _const_spec((D_MODEL, IN_COLS)),
        _const_spec((1, IN_COLS)),
        _const_spec((2, HGRN_WIDTH)),
        _const_spec((1, HGRN_WIDTH)),
        _const_spec((HGRN_CHUNK, HGRN_CHUNK)),
        _const_spec((ATTN_WIDTH, D_MODEL)),
        _const_spec((HGRN_WIDTH, D_MODEL)),
        _const_spec((D_MODEL, D_MODEL)),
        _const_spec((1, D_MODEL)),
        _const_spec((N_EXPERTS, D_MODEL)),
        _const_spec((N_EXPERTS, 1)),
    ]
    out_specs = (
        pl.BlockSpec((1, T, D_MODEL), lambda b, s: (b, s, 0)),
        pl.BlockSpec((1, T, D_MODEL // 2), lambda b, s: (b, s, 0)),
        tok_spec(0), tok_spec(0), tok_spec(0),
        pl.BlockSpec((N_EXPERTS, LANES), lambda b, s: (0, 0)),
    )
    scratch = [
        pltpu.VMEM((WINDOW, KV_WIDTH), _F32),
        pltpu.VMEM((WINDOW, KV_WIDTH), _F32),
        pltpu.VMEM((HGRN_HEADS, HGRN_DK, HGRN_DK), _F32),
        pltpu.VMEM((N_EXPERTS, LANES), _F32),
        pltpu.VMEM((T, ATTN_WIDTH), _F32),
        pltpu.VMEM((T, HGRN_WIDTH), _F32),
    ]
    return pl.pallas_call(
        _mixer_kernel,
        grid=(B, nS),
        in_specs=in_specs,
        out_specs=out_specs,
        out_shape=out_shape,
        scratch_shapes=scratch,
        compiler_params=pltpu.CompilerParams(
            dimension_semantics=("arbitrary", "arbitrary"), vmem_limit_bytes=VMEM_LIMIT),
        name="mixer",
    )(sinks, x, cos_t, sin_t, gmix, w_in, b_in, lb, hgn, lvl, w_ua, w_uh, w_out, gffn, w_rt, b_r)


def _sc_mesh():
    return plsc.VectorSubcoreMesh(core_axis_name="c", subcore_axis_name="s")


def _sc_worker_id():
    return lax.axis_index("s") * SC_CORES + lax.axis_index("c")


def _sc_scatter_rows(x, dest, n_rows):
    N, C = x.shape
    K = dest.shape[0]
    n_chunks = N // SC_CHUNK
    per_worker = n_chunks // SC_WORKERS
    dest3 = dest.reshape(K, n_chunks, SC_CHUNK)

    @functools.partial(
        pl.kernel, out_type=jax.ShapeDtypeStruct((n_rows, C), x.dtype), mesh=_sc_mesh(),
        scratch_types=[pltpu.VMEM((K, SC_CHUNK), jnp.int32), pltpu.VMEM((SC_CHUNK, C), x.dtype),
                       pltpu.SemaphoreType.DMA])
    def scatter(x_hbm, i_hbm, o_hbm, idx_v, rows_v, sem):
        base = _sc_worker_id() * per_worker

        @pl.loop(0, per_worker)
        def _(j):
            c = base + j
            off = pl.multiple_of(c * SC_CHUNK, SC_CHUNK)
            for k in range(K):
                pltpu.sync_copy(i_hbm.at[k, c], idx_v.at[k])
            pltpu.sync_copy(x_hbm.at[pl.ds(off, SC_CHUNK)], rows_v)
            for k in range(K):
                pltpu.async_copy(rows_v, o_hbm.at[idx_v.at[k]], sem).wait()

    return scatter(x, dest3)


def _sc_gather_rows(table, idx):
    M = idx.shape[0]
    C = table.shape[1]
    per_worker = M // SC_WORKERS
    n_chunks = per_worker // SC_CHUNK

    @functools.partial(
        pl.kernel, out_type=jax.ShapeDtypeStruct((M, C), table.dtype), mesh=_sc_mesh(),
        scratch_types=[pltpu.VMEM((SC_CHUNK,), jnp.int32), pltpu.VMEM((SC_CHUNK, C), table.dtype),
                       pltpu.SemaphoreType.DMA])
    def gather(t_hbm, i_hbm, o_hbm, idx_v, rows_v, sem):
        base = _sc_worker_id() * per_worker

        @pl.loop(0, n_chunks)
        def _(j):
            off = pl.multiple_of(base + j * SC_CHUNK, SC_CHUNK)
            pltpu.sync_copy(i_hbm.at[pl.ds(off, SC_CHUNK)], idx_v)
            pltpu.async_copy(t_hbm.at[idx_v], rows_v, sem).wait()
            pltpu.sync_copy(rows_v, o_hbm.at[pl.ds(off, SC_CHUNK)])

    return gather(table, idx)


def _expert_kernel(blk_e_ref, nb_ref, xs_ref, w1_ref, b1_ref, w2_ref, b2_ref, ys_ref, w1b_ref, w2b_ref):
    i = pl.program_id(0)
    active = i < nb_ref[0]
    new_expert = jnp.logical_or(i == 0, blk_e_ref[i] != blk_e_ref[jnp.maximum(i - 1, 0)])

    @pl.when(jnp.logical_and(active, new_expert))
    def _():
        for r in range(0, D_MODEL, WEIGHT_CAST_ROWS):
            w1b_ref[r:r + WEIGHT_CAST_ROWS, :] = w1_ref[0, r:r + WEIGHT_CAST_ROWS, :].astype(_BF)
        for r in range(0, D_EXPERT, WEIGHT_CAST_ROWS):
            w2b_ref[r:r + WEIGHT_CAST_ROWS, :] = w2_ref[0, r:r + WEIGHT_CAST_ROWS, :].astype(_BF)

    @pl.when(active)
    def _():
        xb = _unpack_bf16_pairs(xs_ref[...]).astype(_BF)
        h = _dot(xb, w1b_ref[...]) + b1_ref[0]
        glu = jnp.minimum(h[:, :D_EXPERT], SWIGLU_LIMIT)
        lin = jnp.clip(h[:, D_EXPERT:], -SWIGLU_LIMIT, SWIGLU_LIMIT)
        act = glu * jax.nn.sigmoid(SWIGLU_ALPHA * glu) * (lin + 1.0)
        y = _dot(act.astype(_BF), w2b_ref[...]) + b2_ref[0]
        ys_ref[...] = _pack_bf16_pairs(y)

    @pl.when(jnp.logical_not(active))
    def _():
        ys_ref[...] = jnp.zeros_like(ys_ref)


def _expert_call(blk_e, nb_used, xs, w1, b1, w2, b2):
    n_blocks = xs.shape[0] // MOE_BLOCK
    M = MOE_BLOCK
    grid_spec = pltpu.PrefetchScalarGridSpec(
        num_scalar_prefetch=2,
        grid=(n_blocks,),
        in_specs=[
            pl.BlockSpec((M, D_MODEL // 2), lambda i, be, nb: (jnp.minimum(i, nb[0] - 1), 0)),
            pl.BlockSpec((1, D_MODEL, 2 * D_EXPERT), lambda i, be, nb: (be[i], 0, 0)),
            pl.BlockSpec((1, 1, 2 * D_EXPERT), lambda i, be, nb: (be[i], 0, 0)),
            pl.BlockSpec((1, D_EXPERT, D_MODEL), lambda i, be, nb: (be[i], 0, 0)),
            pl.BlockSpec((1, 1, D_MODEL), lambda i, be, nb: (be[i], 0, 0)),
        ],
        out_specs=pl.BlockSpec((M, D_MODEL // 2), lambda i, be, nb: (i, 0)),
        scratch_shapes=[pltpu.VMEM((D_MODEL, 2 * D_EXPERT), _BF), pltpu.VMEM((D_EXPERT, D_MODEL), _BF)],
    )
    return pl.pallas_call(
        _expert_kernel,
        grid_spec=grid_spec,
        out_shape=jax.ShapeDtypeStruct(xs.shape, jnp.uint32),
        compiler_params=pltpu.CompilerParams(
            dimension_semantics=("arbitrary",), vmem_limit_bytes=VMEM_LIMIT),
        name="experts",
    )(blk_e, nb_used, xs, w1, b1, w2, b2)


def _combine_kernel(h1_ref, gate_ref, gfin_ref, y0_ref, y1_ref, y2_ref, y3_ref, out_ref):
    acc = h1_ref[...]
    gates = gate_ref[...]
    for k, y_ref in enumerate((y0_ref, y1_ref, y2_ref, y3_ref)):
        acc = acc + gates[:, k:k + 1] * _unpack_bf16_pairs(y_ref[...])
    out_ref[...] = _rms(acc, gfin_ref[...])


def _combine_call(h1, gates_t, gfin, yg):
    N = h1.shape[0]
    T = COMBINE_TILE
    nT = N // T
    y_specs = [pl.BlockSpec((T, D_MODEL // 2), functools.partial(lambda i, k: (k * nT + i, 0), k=k))
               for k in range(TOP_K)]
    return pl.pallas_call(
        _combine_kernel,
        grid=(nT,),
        in_specs=[
            pl.BlockSpec((T, D_MODEL), lambda i: (i, 0)),
            pl.BlockSpec((T, TOP_K), lambda i: (i, 0)),
            pl.BlockSpec((1, D_MODEL), lambda i: (0, 0)),
        ] + y_specs,
        out_specs=pl.BlockSpec((T, D_MODEL), lambda i: (i, 0)),
        out_shape=jax.ShapeDtypeStruct((N, D_MODEL), _F32),
        compiler_params=pltpu.CompilerParams(dimension_semantics=("arbitrary",)),
        name="combine",
    )(h1, gates_t, gfin, yg, yg, yg, yg)


def _qk_column_permutation():
    half = HEAD_DIM // 2
    r = np.arange(half)

    def pair(base, ha, hb):
        return np.concatenate([base + ha * HEAD_DIM + r, base + hb * HEAD_DIM + r,
                               base + ha * HEAD_DIM + half + r, base + hb * HEAD_DIM + half + r])

    q = np.concatenate([pair(_OFF_Q, 2 * p, 2 * p + 1) for p in range(N_Q_HEADS // 2)])
    k = pair(_OFF_K, 0, 1)
    return np.concatenate([q, k, np.arange(_OFF_V, IN_COLS)])


def _hgrn_level_table():
    t = np.arange(HGRN_CHUNK)[:, None]
    u = np.arange(HGRN_CHUNK)[None, :]
    first = HGRN_DIAG.bit_length() - 1
    top = np.floor(np.log2(np.maximum(t ^ u, 1))).astype(np.int32)
    return np.where((t > u) & (top >= first), top - first, -1).astype(np.int32)


def _rope_tables(S):
    half = HEAD_DIM // 2
    inv_freq = ROPE_THETA ** (-(jnp.arange(half, dtype=_F32) * 2.0 / HEAD_DIM))
    ang = jnp.arange(S, dtype=_F32)[:, None] * inv_freq[None, :]
    cos = jnp.tile(jnp.cos(ang), (1, 4))
    sin = jnp.tile(jnp.sin(ang), (1, 4))
    sign = jnp.where(jnp.arange(LANES) < 64, -1.0, 1.0).astype(_F32)
    return cos, sin * sign[None, :]


def kernel(x, norm_mix_g, w_in, b_in, attn_sinks, hgrn_lb, hgrn_norm_g, w_up_attn, w_up_hgrn, w_out,
           norm_ffn_g, w_router, b_router, w_moe1, b_moe1, w_moe2, b_moe2, norm_final_g):
    B, S, D = x.shape
    N = B * S
    assert D == D_MODEL and S % SEQ_TILE == 0 and SEQ_TILE % WINDOW == 0
    assert N % COMBINE_TILE == 0 and N % (SC_WORKERS * SC_CHUNK) == 0 and norm_mix_g.shape[0] == 1

    perm = _qk_column_permutation()
    cos_t, sin_t = _rope_tables(S)
    h1, xp, idx, rank, gates, cnt = _mixer_call(
        x, attn_sinks[0], cos_t, sin_t, norm_mix_g,
        w_in[0][:, perm].astype(_BF), b_in[:, perm], hgrn_lb, hgrn_norm_g, jnp.asarray(_hgrn_level_table()),
        w_up_attn[0].astype(_BF), w_up_hgrn[0].astype(_BF), w_out[0].astype(_BF),
        norm_ffn_g, w_router[0].T, b_router[0][:, None])

    counts = cnt[:, 0].astype(jnp.int32)
    padded = ((counts + MOE_BLOCK - 1) // MOE_BLOCK) * MOE_BLOCK
    pad_ends = jnp.cumsum(padded)
    pad_starts = pad_ends - padded
    n_blocks = (N * TOP_K) // MOE_BLOCK + N_EXPERTS
    dest = rank
    for e in range(N_EXPERTS):
        dest = dest + jnp.where(idx == e, pad_starts[e], 0)
    blk_start = jnp.arange(n_blocks, dtype=jnp.int32) * MOE_BLOCK
    blk_e = jnp.minimum(jnp.sum((pad_ends[None, :] <= blk_start[:, None]).astype(jnp.int32), axis=1),
                        N_EXPERTS - 1)
    nb_used = (pad_ends[-1:] // MOE_BLOCK).astype(jnp.int32)

    xs = _sc_scatter_rows(xp.reshape(N, D // 2), dest, n_blocks * MOE_BLOCK)
    ys = _expert_call(blk_e, nb_used, xs, w_moe1[0], b_moe1[0][:, None, :], w_moe2[0], b_moe2[0][:, None, :])
    yg = _sc_gather_rows(ys, dest.reshape(TOP_K * N))
    out = _combine_call(h1.reshape(N, D), gates.T, norm_final_g[None, :], yg)
    return out.reshape(B, S, D)
```

```python
import functools

import numpy as np
import jax
import jax.numpy as jnp
from jax import lax
from jax.experimental import pallas as pl
from jax.experimental.pallas import tpu as pltpu
from jax.experimental.pallas import tpu_sc as plsc

D_MODEL = 1024
HEAD_DIM = 64
N_Q_HEADS = 8
N_KV_HEADS = 2
ATTN_WIDTH = N_Q_HEADS * HEAD_DIM
KV_WIDTH = N_KV_HEADS * HEAD_DIM
WINDOW = 128
ROPE_THETA = 10000.0
HGRN_HEADS = 4
HGRN_DK = 128
HGRN_WIDTH = HGRN_HEADS * HGRN_DK
N_EXPERTS = 32
TOP_K = 4
D_EXPERT = 1024
SWIGLU_ALPHA = 1.702
SWIGLU_LIMIT = 7.0
MOE_BLOCK = 512
NORM_EPS = 1e-5

_OFF_Q = 0
_OFF_K = _OFF_Q + ATTN_WIDTH
_OFF_V = _OFF_K + KV_WIDTH
_OFF_HQ = _OFF_V + KV_WIDTH
_OFF_HF = _OFF_HQ + HGRN_WIDTH
_OFF_HI = _OFF_HF + HGRN_WIDTH
_OFF_HG = _OFF_HI + HGRN_WIDTH
_OFF_GA = _OFF_HG + HGRN_WIDTH
_OFF_GH = _OFF_GA + D_MODEL
IN_COLS = _OFF_GH + D_MODEL

LANES = 128
SUBLANES = 8
SEQ_TILE = 512
HGRN_CHUNK = 256
COMBINE_TILE = 512
BATCH_PARTS = 2
WEIGHT_CAST_ROWS = 128
SC_CORES = 2
SC_WORKERS = 32
SC_CHUNK = 128
VMEM_LIMIT = 56 * 1024 * 1024

_BF = jnp.bfloat16
_F32 = jnp.float32


def _dot(a, b):
    return jnp.dot(a, b, preferred_element_type=_F32)


def _dot_nt(a, b, precision=None):
    return lax.dot_general(a, b, (((1,), (1,)), ((), ())), precision=precision,
                           preferred_element_type=_F32)


def _dot_tn(a, b):
    return lax.dot_general(a, b, (((0,), (0,)), ((), ())), preferred_element_type=_F32)


def _rowsum(x):
    return jnp.sum(x, axis=1, keepdims=True)


def _rms(x, g):
    ms = _rowsum(x * x) * (1.0 / x.shape[1])
    return x * lax.rsqrt(ms + NORM_EPS) * g


def _pack_bf16_pairs(x):
    n = x.shape[1] // 2
    lo = lax.bitcast_convert_type(x[:, :n].astype(_BF).astype(_F32), jnp.uint32)
    hi = lax.bitcast_convert_type(x[:, n:].astype(_BF).astype(_F32), jnp.uint32)
    return (lo >> 16) | (hi & jnp.uint32(0xFFFF0000))


def _unpack_bf16_pairs(u):
    lo = lax.bitcast_convert_type(u << 16, _F32)
    hi = lax.bitcast_convert_type(u & jnp.uint32(0xFFFF0000), _F32)
    return jnp.concatenate([lo, hi], axis=1)


def _mixer_kernel(sinks_ref, x_ref, cos_ref, sin_ref, gmix_ref, w_in_ref, b_in_ref, lb_ref,
                  hgn_ref, lvl_ref, w_ua_ref, w_uh_ref, w_out_ref, gffn_ref, w_rt_ref, b_r_ref,
                  h1_ref, xp_ref, idx_ref, rank_ref, gate_ref, cnt_ref,
                  kc_ref, vc_ref, st_ref, cnt_sc, ya_ref, o_ref):
    T = SEQ_TILE
    b = pl.program_id(0)
    s = pl.program_id(1)

    @pl.when(jnp.logical_and(b == 0, s == 0))
    def _():
        cnt_sc[...] = jnp.zeros_like(cnt_sc)

    @pl.when(s == 0)
    def _():
        kc_ref[...] = jnp.zeros_like(kc_ref)
        vc_ref[...] = jnp.zeros_like(vc_ref)
        st_ref[...] = jnp.zeros_like(st_ref)

    x = x_ref[0]
    xn = _rms(x, gmix_ref[...]).astype(_BF)

    def proj(off, width):
        return _dot(xn, w_in_ref[:, off:off + width]) + b_in_ref[:, off:off + width]

    cos = cos_ref[...]
    sin = sin_ref[...]

    def rope(t):
        return t * cos + pltpu.roll(t, 64, axis=1) * sin

    zq = proj(_OFF_Q, ATTN_WIDTH)
    scale = HEAD_DIM ** -0.5
    q_pairs = [(rope(zq[:, LANES * p:LANES * (p + 1)]) * scale).astype(_BF) for p in range(4)]
    k_rot = rope(proj(_OFF_K, KV_WIDTH))
    v_new = proj(_OFF_V, KV_WIDTH)

    k_ext = jnp.concatenate([kc_ref[...], k_rot], axis=0)
    v_ext = jnp.concatenate([vc_ref[...], v_new], axis=0)
    kc_ref[...] = k_rot[T - WINDOW:, :]
    vc_ref[...] = v_new[T - WINDOW:, :]

    lane = lax.broadcasted_iota(jnp.int32, (1, LANES), 1)
    slot_a = (lane % 64) < 32
    lane_lo = lane < 64
    k_r32 = pltpu.roll(k_ext, 32, axis=1)
    k_r96 = pltpu.roll(k_ext, 96, axis=1)
    v_r64 = pltpu.roll(v_ext, 64, axis=1)
    zero = jnp.zeros_like(k_ext)
    k_var = [(jnp.where(slot_a, k_ext, zero).astype(_BF), jnp.where(slot_a, zero, k_r32).astype(_BF)),
             (jnp.where(slot_a, k_r96, zero).astype(_BF), jnp.where(slot_a, zero, k_ext).astype(_BF))]
    v_var = [(jnp.where(lane_lo, v_ext, zero).astype(_BF), jnp.where(lane_lo, zero, v_r64).astype(_BF)),
             (jnp.where(lane_lo, v_r64, zero).astype(_BF), jnp.where(lane_lo, zero, v_ext).astype(_BF))]

    qi = lax.broadcasted_iota(jnp.int32, (2 * WINDOW, 2 * WINDOW), 0) % WINDOW
    kj = lax.broadcasted_iota(jnp.int32, (2 * WINDOW, 2 * WINDOW), 1)
    band = jnp.logical_and(kj > qi, kj <= qi + WINDOW)
    row_top = lax.broadcasted_iota(jnp.int32, (2 * WINDOW, 1), 0) < WINDOW
    neg_inf = jnp.float32(-jnp.inf)

    for n in range(T // WINDOW):
        if n == 0:
            ok = jnp.logical_and(band, jnp.logical_or(kj >= WINDOW, s > 0))
        else:
            ok = band
        r0 = n * WINDOW
        for j in range(N_KV_HEADS):
            q2 = jnp.concatenate([q_pairs[2 * j][r0:r0 + WINDOW], q_pairs[2 * j + 1][r0:r0 + WINDOW]], axis=0)
            kcat = jnp.concatenate([k_var[j][0][r0:r0 + 2 * WINDOW], k_var[j][1][r0:r0 + 2 * WINDOW]], axis=0)
            vcat = jnp.concatenate([v_var[j][0][r0:r0 + 2 * WINDOW], v_var[j][1][r0:r0 + 2 * WINDOW]], axis=0)
            sc = _dot_nt(q2, kcat)
            ps, rs = [], []
            for half in range(2):
                sh = jnp.where(ok, sc[:, 2 * WINDOW * half:2 * WINDOW * (half + 1)], neg_inf)
                snk = jnp.where(row_top, sinks_ref[4 * j + half], sinks_ref[4 * j + 2 + half])
                m = jnp.maximum(jnp.max(sh, axis=1, keepdims=True), snk)
                p = jnp.exp(sh - m)
                den = _rowsum(p) + jnp.exp(snk - m)
                ps.append(p.astype(_BF))
                rs.append(1.0 / den)
            o = _dot(jnp.concatenate(ps, axis=1), vcat)
            o = o * jnp.where(lane_lo, rs[0], rs[1])
            ya_ref[r0:r0 + WINDOW, LANES * (2 * j):LANES * (2 * j + 1)] = o[:WINDOW]
            ya_ref[r0:r0 + WINDOW, LANES * (2 * j + 1):LANES * (2 * j + 2)] = o[WINDOW:]

    HW = HGRN_WIDTH
    a0 = lb_ref[0:1, :]
    a1 = lb_ref[1:2, :]
    am = jnp.maximum(a0, a1)
    e0 = jnp.exp(a0 - am)
    lb = e0 / (e0 + jnp.exp(a1 - am))
    f = lb + (1.0 - lb) * jax.nn.sigmoid(proj(_OFF_HF, HW))
    kk = 1.0 - f
    g = jnp.log(f)
    hq = proj(_OFF_HQ, HW)
    hv = proj(_OFF_HI, HW)

    row = lax.broadcasted_iota(jnp.int32, (T, 1), 0)
    C = HGRN_CHUNK

    def rdown(t, d):
        return pltpu.roll(t, d, axis=0)

    def rup(t, d):
        return pltpu.roll(t, T - d, axis=0)

    def head(t, h):
        return t[:, HGRN_DK * h:HGRN_DK * (h + 1)]

    qk0 = hq * kk
    o_parts = [_rowsum(head(qk0, h)) * head(hv, h) for h in range(HGRN_HEADS)]

    small_levels = []
    bsz = 1
    lb_cum, lb_tot = g, g
    while bsz < SUBLANES:
        odd = (row % (2 * bsz)) >= bsz
        ex = jnp.exp(jnp.where(odd, lb_cum, lb_tot - lb_cum))
        w = jnp.where(odd, hq, kk) * ex
        small_levels.append((jnp.where(odd, w, 0.0).astype(_BF), jnp.where(odd, 0.0, w).astype(_BF)))
        prev_tot = rdown(lb_tot, bsz)
        lb_cum = lb_cum + jnp.where(odd, prev_tot, 0.0)
        lb_tot = lb_tot + jnp.where(odd, prev_tot, rup(lb_tot, bsz))
        bsz *= 2

    cum = [lb_cum[r:r + bsz] for r in range(0, T, bsz)]
    tot = [lb_tot[r:r + 1] for r in range(0, T, bsz)]
    kk_b = kk.astype(_BF)
    big_levels = []
    while bsz < C:
        q_rows, k_rows = [], []
        for j, r in enumerate(range(0, T, bsz)):
            if j % 2:
                q_rows.append(hq[r:r + bsz] * jnp.exp(cum[j]))
                k_rows.append(kk_b[r:r + bsz])
            else:
                k_rows.append((kk[r:r + bsz] * jnp.exp(tot[j] - cum[j])).astype(_BF))
        big_levels.append((bsz, jnp.concatenate(q_rows, axis=0).astype(_BF), jnp.concatenate(k_rows, axis=0)))
        cum = [jnp.concatenate([cum[j], cum[j + 1] + tot[j]], axis=0) for j in range(0, len(cum), 2)]
        tot = [tot[j] + tot[j + 1] for j in range(0, len(tot), 2)]
        bsz *= 2
    hv_b = hv.astype(_BF)
    lvl = lvl_ref[...]

    for c in range(T // C):
        rs_ = slice(c * C, (c + 1) * C)
        rh_ = slice(c * C // 2, (c + 1) * C // 2)
        q_in = (hq[rs_] * jnp.exp(cum[c])).astype(_BF)
        k_st = (kk[rs_] * jnp.exp(tot[c] - cum[c])).astype(_BF)
        dec = jnp.exp(tot[c])
        for h in range(HGRN_HEADS):
            cs_ = slice(HGRN_DK * h, HGRN_DK * (h + 1))
            st = st_ref[h]
            vc = hv_b[rs_, cs_]
            inter = _dot_nt(q_in[:, cs_], st.astype(_BF))
            amat = jnp.zeros((C, C), _F32)
            for li, (ql, kl) in enumerate(small_levels):
                amat = jnp.where(lvl == li, _dot_nt(ql[rs_, cs_], kl[rs_, cs_]), amat)
            for li, (bs, ql, kl) in enumerate(big_levels, start=len(small_levels)):
                p = _dot_nt(ql[rh_, cs_], kl[rs_, cs_])
                blocks = []
                for jb, r in enumerate(range(0, C, bs)):
                    if jb % 2:
                        blocks.append(jnp.where(lvl[r:r + bs] == li, p[(r - bs) // 2:(r + bs) // 2], amat[r:r + bs]))
                    else:
                        blocks.append(amat[r:r + bs])
                amat = jnp.concatenate(blocks, axis=0)
            intra = _dot(amat.astype(_BF), vc)
            st_ref[h] = st * dec[:, cs_] + _dot_tn(vc, k_st[:, cs_])
            o_ref[rs_, cs_] = inter + intra + o_parts[h][rs_]

    o = o_ref[...]
    hg = proj(_OFF_HG, HW)
    yh_parts = []
    for h in range(HGRN_HEADS):
        oh = head(o, h)
        ms = _rowsum(oh * oh) * (1.0 / HGRN_DK)
        yh_parts.append(oh * lax.rsqrt(ms + NORM_EPS))
    yh = jnp.concatenate(yh_parts, axis=1) * hgn_ref[...]
    yh = (yh * (hg * jax.nn.sigmoid(hg))).astype(_BF)

    up_a = _dot(ya_ref[...].astype(_BF), w_ua_ref[...])
    up_h = _dot(yh, w_uh_ref[...])
    merged = jax.nn.sigmoid(proj(_OFF_GA, D_MODEL)) * up_a + jax.nn.sigmoid(proj(_OFF_GH, D_MODEL)) * up_h
    h1 = x + _dot(merged.astype(_BF), w_out_ref[...])
    h1_ref[0] = h1

    xn2 = _rms(h1, gffn_ref[...])
    xp_ref[0] = _pack_bf16_pairs(xn2)
    x_hi = xn2.astype(_BF)
    x_lo = (xn2 - x_hi.astype(_F32)).astype(_BF)
    w_rt = w_rt_ref[...]
    w_hi = w_rt.astype(_BF)
    w_lo = (w_rt - w_hi.astype(_F32)).astype(_BF)
    logits = (_dot_nt(w_hi, x_hi) + (_dot_nt(w_hi, x_lo) + _dot_nt(w_lo, x_hi))) + b_r_ref[...]
    eidx = lax.broadcasted_iota(jnp.int32, (N_EXPERTS, T), 0)
    vals, sels, ohs = [], [], []
    l = logits
    for _ in range(TOP_K):
        m = jnp.max(l, axis=0, keepdims=True)
        sel = jnp.min(jnp.where(l == m, eidx, N_EXPERTS), axis=0, keepdims=True)
        oh = eidx == sel
        vals.append(m)
        sels.append(sel)
        ohs.append(oh)
        l = jnp.where(oh, neg_inf, l)
    es = [jnp.exp(v - vals[0]) for v in vals]
    den = es[0] + es[1] + es[2] + es[3]
    chosen = jnp.zeros((N_EXPERTS, T), _F32)
    for oh in ohs:
        chosen = chosen + jnp.where(oh, 1.0, 0.0)
    ui = lax.broadcasted_iota(jnp.int32, (T, T), 0)
    uj = lax.broadcasted_iota(jnp.int32, (T, T), 1)
    upper = jnp.where(ui < uj, 1.0, 0.0).astype(_BF)
    before = _dot(chosen.astype(_BF), upper) + cnt_sc[:, 0:1]
    for k in range(TOP_K):
        idx_ref[k:k + 1, :] = sels[k]
        gate_ref[k:k + 1, :] = es[k] / den
        rank_ref[k:k + 1, :] = jnp.sum(jnp.where(ohs[k], before, 0.0), axis=0, keepdims=True).astype(jnp.int32)
    cnt_sc[...] = cnt_sc[...] + _rowsum(chosen)
    cnt_ref[...] = cnt_sc[...]


def _const_spec(shape):
    return pl.BlockSpec(shape, lambda b, s: (0,) * len(shape), pipeline_mode=pl.Buffered(1))


def _mixer_call(batch0, B, x, sinks, cos_t, sin_t, gmix, w_in, b_in, lb, hgn, lvl, w_ua, w_uh, w_out, gffn, w_rt, b_r):
    S = x.shape[1]
    T = SEQ_TILE
    N = B * S
    nS = S // T
    tok_spec = lambda w: pl.BlockSpec((4, T), lambda b, s: (0, b * nS + s))
    out_shape = (
        jax.ShapeDtypeStruct((B, S, D_MODEL), _F32),
        jax.ShapeDtypeStruct((B, S, D_MODEL // 2), jnp.uint32),
        jax.ShapeDtypeStruct((TOP_K, N), jnp.int32),
        jax.ShapeDtypeStruct((TOP_K, N), jnp.int32),
        jax.ShapeDtypeStruct((TOP_K, N), _F32),
        jax.ShapeDtypeStruct((N_EXPERTS, LANES), _F32),
    )
    in_specs = [
        pl.BlockSpec(memory_space=pltpu.SMEM),
        pl.BlockSpec((1, T, D_MODEL), lambda b, s: (batch0 + b, s, 0)),
        pl.BlockSpec((T, LANES), lambda b, s: (s, 0)),
        pl.BlockSpec((T, LANES), lambda b, s: (s, 0)),
        _const_spec((1, D_MODEL)),
        _const_spec((D_MODEL, IN_COLS)),
        _const_spec((1, IN_COLS)),
        _const_spec((2, HGRN_WIDTH)),
        _const_spec((1, HGRN_WIDTH)),
        _const_spec((HGRN_CHUNK, HGRN_CHUNK)),
        _const_spec((ATTN_WIDTH, D_MODEL)),
        _const_spec((HGRN_WIDTH, D_MODEL)),
        _const_spec((D_MODEL, D_MODEL)),
        _const_spec((1, D_MODEL)),
        _const_spec((N_EXPERTS, D_MODEL)),
        _const_spec((N_EXPERTS, 1)),
    ]
    out_specs = (
        pl.BlockSpec((1, T, D_MODEL), lambda b, s: (b, s, 0)),
        pl.BlockSpec((1, T, D_MODEL // 2), lambda b, s: (b, s, 0)),
        tok_spec(0), tok_spec(0), tok_spec(0),
        pl.BlockSpec((N_EXPERTS, LANES), lambda b, s: (0, 0)),
    )
    scratch = [
        pltpu.VMEM((WINDOW, KV_WIDTH), _F32),
        pltpu.VMEM((WINDOW, KV_WIDTH), _F32),
        pltpu.VMEM((HGRN_HEADS, HGRN_DK, HGRN_DK), _F32),
        pltpu.VMEM((N_EXPERTS, LANES), _F32),
        pltpu.VMEM((T, ATTN_WIDTH), _F32),
        pltpu.VMEM((T, HGRN_WIDTH), _F32),
    ]
    return pl.pallas_call(
        _mixer_kernel,
        grid=(B, nS),
        in_specs=in_specs,
        out_specs=out_specs,
        out_shape=out_shape,
        scratch_shapes=scratch,
        compiler_params=pltpu.CompilerParams(
            dimension_semantics=("arbitrary", "arbitrary"), vmem_limit_bytes=VMEM_LIMIT),
        name="mixer",
    )(sinks, x, cos_t, sin_t, gmix, w_in, b_in, lb, hgn, lvl, w_ua, w_uh, w_out, gffn, w_rt, b_r)


def _sc_mesh():
    return plsc.VectorSubcoreMesh(core_axis_name="c", subcore_axis_name="s")


def _sc_worker_id():
    return lax.axis_index("s") * SC_CORES + lax.axis_index("c")


def _sc_scatter_rows(x, dest, n_rows):
    N, C = x.shape
    K = dest.shape[0]
    n_chunks = N // SC_CHUNK
    per_worker = n_chunks // SC_WORKERS
    dest3 = dest.reshape(K, n_chunks, SC_CHUNK)

    @functools.partial(
        pl.kernel, out_type=jax.ShapeDtypeStruct((n_rows, C), x.dtype), mesh=_sc_mesh(),
        scratch_types=[pltpu.VMEM((K, SC_CHUNK), jnp.int32), pltpu.VMEM((SC_CHUNK, C), x.dtype),
                       pltpu.SemaphoreType.DMA])
    def scatter(x_hbm, i_hbm, o_hbm, idx_v, rows_v, sem):
        base = _sc_worker_id() * per_worker

        @pl.loop(0, per_worker)
        def _(j):
            c = base + j
            off = pl.multiple_of(c * SC_CHUNK, SC_CHUNK)
            for k in range(K):
                pltpu.sync_copy(i_hbm.at[k, c], idx_v.at[k])
            pltpu.sync_copy(x_hbm.at[pl.ds(off, SC_CHUNK)], rows_v)
            for k in range(K):
                pltpu.async_copy(rows_v, o_hbm.at[idx_v.at[k]], sem).wait()

    return scatter(x, dest3)


def _sc_gather_rows(table, idx):
    M = idx.shape[0]
    C = table.shape[1]
    per_worker = M // SC_WORKERS
    n_chunks = per_worker // SC_CHUNK

    @functools.partial(
        pl.kernel, out_type=jax.ShapeDtypeStruct((M, C), table.dtype), mesh=_sc_mesh(),
        scratch_types=[pltpu.VMEM((SC_CHUNK,), jnp.int32), pltpu.VMEM((SC_CHUNK, C), table.dtype),
                       pltpu.SemaphoreType.DMA])
    def gather(t_hbm, i_hbm, o_hbm, idx_v, rows_v, sem):
        base = _sc_worker_id() * per_worker

        @pl.loop(0, n_chunks)
        def _(j):
            off = pl.multiple_of(base + j * SC_CHUNK, SC_CHUNK)
            pltpu.sync_copy(i_hbm.at[pl.ds(off, SC_CHUNK)], idx_v)
            pltpu.async_copy(t_hbm.at[idx_v], rows_v, sem).wait()
            pltpu.sync_copy(rows_v, o_hbm.at[pl.ds(off, SC_CHUNK)])

    return gather(table, idx)


def _expert_kernel(blk_e_ref, nb_ref, xs_ref, w1_ref, b1_ref, w2_ref, b2_ref, ys_ref, w1b_ref, w2b_ref):
    i = pl.program_id(0)
    active = i < nb_ref[0]
    new_expert = jnp.logical_or(i == 0, blk_e_ref[i] != blk_e_ref[jnp.maximum(i - 1, 0)])

    @pl.when(jnp.logical_and(active, new_expert))
    def _():
        for r in range(0, D_MODEL, WEIGHT_CAST_ROWS):
            w1b_ref[r:r + WEIGHT_CAST_ROWS, :] = w1_ref[0, r:r + WEIGHT_CAST_ROWS, :].astype(_BF)
        for r in range(0, D_EXPERT, WEIGHT_CAST_ROWS):
            w2b_ref[r:r + WEIGHT_CAST_ROWS, :] = w2_ref[0, r:r + WEIGHT_CAST_ROWS, :].astype(_BF)

    @pl.when(active)
    def _():
        xb = _unpack_bf16_pairs(xs_ref[...]).astype(_BF)
        h = _dot(xb, w1b_ref[...]) + b1_ref[0]
        glu = jnp.minimum(h[:, :D_EXPERT], SWIGLU_LIMIT)
        lin = jnp.clip(h[:, D_EXPERT:], -SWIGLU_LIMIT, SWIGLU_LIMIT)
        act = glu * jax.nn.sigmoid(SWIGLU_ALPHA * glu) * (lin + 1.0)
        y = _dot(act.astype(_BF), w2b_ref[...]) + b2_ref[0]
        ys_ref[...] = _pack_bf16_pairs(y)

    @pl.when(jnp.logical_not(active))
    def _():
        ys_ref[...] = jnp.zeros_like(ys_ref)


def _expert_call(blk_e, nb_used, xs, w1, b1, w2, b2):
    n_blocks = xs.shape[0] // MOE_BLOCK
    M = MOE_BLOCK
    grid_spec = pltpu.PrefetchScalarGridSpec(
        num_scalar_prefetch=2,
        grid=(n_blocks,),
        in_specs=[
            pl.BlockSpec((M, D_MODEL // 2), lambda i, be, nb: (jnp.minimum(i, nb[0] - 1), 0)),
            pl.BlockSpec((1, D_MODEL, 2 * D_EXPERT), lambda i, be, nb: (be[i], 0, 0)),
            pl.BlockSpec((1, 1, 2 * D_EXPERT), lambda i, be, nb: (be[i], 0, 0)),
            pl.BlockSpec((1, D_EXPERT, D_MODEL), lambda i, be, nb: (be[i], 0, 0)),
            pl.BlockSpec((1, 1, D_MODEL), lambda i, be, nb: (be[i], 0, 0)),
        ],
        out_specs=pl.BlockSpec((M, D_MODEL // 2), lambda i, be, nb: (i, 0)),
        scratch_shapes=[pltpu.VMEM((D_MODEL, 2 * D_EXPERT), _BF), pltpu.VMEM((D_EXPERT, D_MODEL), _BF)],
    )
    return pl.pallas_call(
        _expert_kernel,
        grid_spec=grid_spec,
        out_shape=jax.ShapeDtypeStruct(xs.shape, jnp.uint32),
        compiler_params=pltpu.CompilerParams(
            dimension_semantics=("arbitrary",), vmem_limit_bytes=VMEM_LIMIT),
        name="experts",
    )(blk_e, nb_used, xs, w1, b1, w2, b2)


def _combine_kernel(h1_ref, gate_ref, gfin_ref, y0_ref, y1_ref, y2_ref, y3_ref, out_ref):
    acc = h1_ref[...]
    gates = gate_ref[...]
    for k, y_ref in enumerate((y0_ref, y1_ref, y2_ref, y3_ref)):
        acc = acc + gates[:, k:k + 1] * _unpack_bf16_pairs(y_ref[...])
    out_ref[...] = _rms(acc, gfin_ref[...])


def _combine_call(h1, gates_t, gfin, yg, out_prev, tile0, n_total):
    N = h1.shape[0]
    T = COMBINE_TILE
    nT = N // T
    y_specs = [pl.BlockSpec((T, D_MODEL // 2), functools.partial(lambda i, k: (k * nT + i, 0), k=k))
               for k in range(TOP_K)]
    in_specs = [
        pl.BlockSpec((T, D_MODEL), lambda i: (i, 0)),
        pl.BlockSpec((T, TOP_K), lambda i: (i, 0)),
        pl.BlockSpec((1, D_MODEL), lambda i: (0, 0)),
    ] + y_specs
    args = [h1, gates_t, gfin, yg, yg, yg, yg]
    body, aliases = _combine_kernel, {}
    if out_prev is not None:
        in_specs.append(pl.BlockSpec(memory_space=pl.ANY))
        args.append(out_prev)
        aliases = {len(args) - 1: 0}
        body = lambda *refs: _combine_kernel(*refs[:7], refs[8])
    return pl.pallas_call(
        body,
        grid=(nT,),
        in_specs=in_specs,
        out_specs=pl.BlockSpec((T, D_MODEL), lambda i: (tile0 + i, 0)),
        out_shape=jax.ShapeDtypeStruct((n_total, D_MODEL), _F32),
        input_output_aliases=aliases,
        compiler_params=pltpu.CompilerParams(dimension_semantics=("arbitrary",)),
        name="combine",
    )(*args)


def _qk_column_permutation():
    half = HEAD_DIM // 2
    r = np.arange(half)

    def pair(base, ha, hb):
        return np.concatenate([base + ha * HEAD_DIM + r, base + hb * HEAD_DIM + r,
                               base + ha * HEAD_DIM + half + r, base + hb * HEAD_DIM + half + r])

    q = np.concatenate([pair(_OFF_Q, 2 * p, 2 * p + 1) for p in range(N_Q_HEADS // 2)])
    k = pair(_OFF_K, 0, 1)
    return np.concatenate([q, k, np.arange(_OFF_V, IN_COLS)])


def _hgrn_level_table():
    t = np.arange(HGRN_CHUNK)[:, None]
    u = np.arange(HGRN_CHUNK)[None, :]
    top = np.floor(np.log2(np.maximum(t ^ u, 1))).astype(np.int32)
    return np.where(t > u, top, -1).astype(np.int32)


def _rope_tables(S):
    half = HEAD_DIM // 2
    inv_freq = ROPE_THETA ** (-(jnp.arange(half, dtype=_F32) * 2.0 / HEAD_DIM))
    ang = jnp.arange(S, dtype=_F32)[:, None] * inv_freq[None, :]
    cos = jnp.tile(jnp.cos(ang), (1, 4))
    sin = jnp.tile(jnp.sin(ang), (1, 4))
    sign = jnp.where(jnp.arange(LANES) < 64, -1.0, 1.0).astype(_F32)
    return cos, sin * sign[None, :]


def kernel(x, norm_mix_g, w_in, b_in, attn_sinks, hgrn_lb, hgrn_norm_g, w_up_attn, w_up_hgrn, w_out,
           norm_ffn_g, w_router, b_router, w_moe1, b_moe1, w_moe2, b_moe2, norm_final_g):
    B, S, D = x.shape
    N = B * S
    assert D == D_MODEL and S % SEQ_TILE == 0 and SEQ_TILE % WINDOW == 0
    assert B % BATCH_PARTS == 0 and (N // BATCH_PARTS) % (SC_WORKERS * SC_CHUNK) == 0
    assert (N // BATCH_PARTS) % COMBINE_TILE == 0 and norm_mix_g.shape[0] == 1

    perm = _qk_column_permutation()
    cos_t, sin_t = _rope_tables(S)
    weights = (attn_sinks[0], cos_t, sin_t, norm_mix_g,
               w_in[0][:, perm].astype(_BF), b_in[:, perm], hgrn_lb, hgrn_norm_g, jnp.asarray(_hgrn_level_table()),
               w_up_attn[0].astype(_BF), w_up_hgrn[0].astype(_BF), w_out[0].astype(_BF),
               norm_ffn_g, w_router[0].T, b_router[0][:, None])

    Bp = B // BATCH_PARTS
    Np = Bp * S
    n_blocks = (Np * TOP_K) // MOE_BLOCK + N_EXPERTS
    out = None
    for part in range(BATCH_PARTS):
        h1, xp, idx, rank, gates, cnt = _mixer_call(part * Bp, Bp, x, *weights)

        counts = cnt[:, 0].astype(jnp.int32)
        padded = ((counts + MOE_BLOCK - 1) // MOE_BLOCK) * MOE_BLOCK
        pad_ends = jnp.cumsum(padded)
        pad_starts = pad_ends - padded
        dest = rank
        for e in range(N_EXPERTS):
            dest = dest + jnp.where(idx == e, pad_starts[e], 0)
        blk_start = jnp.arange(n_blocks, dtype=jnp.int32) * MOE_BLOCK
        blk_e = jnp.minimum(jnp.sum((pad_ends[None, :] <= blk_start[:, None]).astype(jnp.int32), axis=1),
                            N_EXPERTS - 1)
        nb_used = (pad_ends[-1:] // MOE_BLOCK).astype(jnp.int32)

        xs = _sc_scatter_rows(xp.reshape(Np, D // 2), dest, n_blocks * MOE_BLOCK)
        ys = _expert_call(blk_e, nb_used, xs, w_moe1[0], b_moe1[0][:, None, :], w_moe2[0], b_moe2[0][:, None, :])
        yg = _sc_gather_rows(ys, dest.reshape(TOP_K * Np))
        out = _combine_call(h1.reshape(Np, D), gates.T, norm_final_g[None, :], yg, out,
                            part * (Np // COMBINE_TILE), N)
    return out.reshape(B, S, D)
```

```python
import functools

import numpy as np
import jax
import jax.numpy as jnp
from jax import lax
from jax.experimental import pallas as pl
from jax.experimental.pallas import tpu as pltpu
from jax.experimental.pallas import tpu_sc as plsc

D_MODEL = 1024
HEAD_DIM = 64
N_Q_HEADS = 8
N_KV_HEADS = 2
ATTN_WIDTH = N_Q_HEADS * HEAD_DIM
KV_WIDTH = N_KV_HEADS * HEAD_DIM
WINDOW = 128
ROPE_THETA = 10000.0
HGRN_HEADS = 4
HGRN_DK = 128
HGRN_WIDTH = HGRN_HEADS * HGRN_DK
N_EXPERTS = 32
TOP_K = 4
D_EXPERT = 1024
SWIGLU_ALPHA = 1.702
SWIGLU_LIMIT = 7.0
MOE_BLOCK = 512
NORM_EPS = 1e-5

_OFF_Q = 0
_OFF_K = _OFF_Q + ATTN_WIDTH
_OFF_V = _OFF_K + KV_WIDTH
_OFF_HQ = _OFF_V + KV_WIDTH
_OFF_HF = _OFF_HQ + HGRN_WIDTH
_OFF_HI = _OFF_HF + HGRN_WIDTH
_OFF_HG = _OFF_HI + HGRN_WIDTH
_OFF_GA = _OFF_HG + HGRN_WIDTH
_OFF_GH = _OFF_GA + D_MODEL
IN_COLS = _OFF_GH + D_MODEL

LANES = 128
SUBLANES = 8
SEQ_TILE = 512
HGRN_CHUNK = 256
GATE_JOB_COLS = 256
COMBINE_TILE = 512
DEST_TILE = 8192
BATCH_PARTS = 2
WEIGHT_CAST_ROWS = 128
SC_CORES = 2
SC_WORKERS = 32
SC_CHUNK = 128
VMEM_LIMIT = 56 * 1024 * 1024

_BF = jnp.bfloat16
_F32 = jnp.float32


def _dot(a, b):
    return jnp.dot(a, b, preferred_element_type=_F32)


def _dot_nt(a, b, precision=None):
    return lax.dot_general(a, b, (((1,), (1,)), ((), ())), precision=precision,
                           preferred_element_type=_F32)


def _dot_tn(a, b):
    return lax.dot_general(a, b, (((0,), (0,)), ((), ())), preferred_element_type=_F32)


def _rowsum(x):
    return jnp.sum(x, axis=1, keepdims=True)


def _rms(x, g):
    ms = _rowsum(x * x) * (1.0 / x.shape[1])
    return x * lax.rsqrt(ms + NORM_EPS) * g


def _pack_bf16_pairs(x):
    n = x.shape[1] // 2
    lo = lax.bitcast_convert_type(x[:, :n].astype(_BF).astype(_F32), jnp.uint32)
    hi = lax.bitcast_convert_type(x[:, n:].astype(_BF).astype(_F32), jnp.uint32)
    return (lo >> 16) | (hi & jnp.uint32(0xFFFF0000))


def _unpack_bf16_pairs(u):
    lo = lax.bitcast_convert_type(u << 16, _F32)
    hi = lax.bitcast_convert_type(u & jnp.uint32(0xFFFF0000), _F32)
    return jnp.concatenate([lo, hi], axis=1)


def _mixer_kernel(sinks_ref, x_ref, cos_ref, sin_ref, gmix_ref, w_in_ref, b_in_ref, lb_ref,
                  hgn_ref, lvl_ref, w_ua_ref, w_uh_ref, w_out_ref, gffn_ref, w_rt_ref, b_r_ref,
                  h1_ref, xp_ref, idx_ref, rank_ref, gate_ref, cnt_ref,
                  kc_ref, vc_ref, st_ref, cnt_sc, ya_ref, o_ref):
    T = SEQ_TILE
    b = pl.program_id(0)
    s = pl.program_id(1)

    @pl.when(jnp.logical_and(b == 0, s == 0))
    def _():
        cnt_sc[...] = jnp.zeros_like(cnt_sc)

    @pl.when(s == 0)
    def _():
        kc_ref[...] = jnp.zeros_like(kc_ref)
        vc_ref[...] = jnp.zeros_like(vc_ref)
        st_ref[...] = jnp.zeros_like(st_ref)

    x = x_ref[0]
    xn = _rms(x, gmix_ref[...]).astype(_BF)

    def proj(off, width):
        return _dot(xn, w_in_ref[:, off:off + width]) + b_in_ref[:, off:off + width]

    cos = cos_ref[...]
    sin = sin_ref[...]

    def rope(t):
        return t * cos + pltpu.roll(t, 64, axis=1) * sin

    zq = proj(_OFF_Q, ATTN_WIDTH)
    scale = HEAD_DIM ** -0.5
    q_pairs = [(rope(zq[:, LANES * p:LANES * (p + 1)]) * scale).astype(_BF) for p in range(4)]
    k_rot = rope(proj(_OFF_K, KV_WIDTH))
    v_new = proj(_OFF_V, KV_WIDTH)

    k_ext = jnp.concatenate([kc_ref[...], k_rot], axis=0)
    v_ext = jnp.concatenate([vc_ref[...], v_new], axis=0)
    kc_ref[...] = k_rot[T - WINDOW:, :]
    vc_ref[...] = v_new[T - WINDOW:, :]

    lane = lax.broadcasted_iota(jnp.int32, (1, LANES), 1)
    slot_a = (lane % 64) < 32
    lane_lo = lane < 64
    k_r32 = pltpu.roll(k_ext, 32, axis=1)
    k_r96 = pltpu.roll(k_ext, 96, axis=1)
    v_r64 = pltpu.roll(v_ext, 64, axis=1)
    zero = jnp.zeros_like(k_ext)
    k_var = [(jnp.where(slot_a, k_ext, zero).astype(_BF), jnp.where(slot_a, zero, k_r32).astype(_BF)),
             (jnp.where(slot_a, k_r96, zero).astype(_BF), jnp.where(slot_a, zero, k_ext).astype(_BF))]
    v_var = [(jnp.where(lane_lo, v_ext, zero).astype(_BF), jnp.where(lane_lo, zero, v_r64).astype(_BF)),
             (jnp.where(lane_lo, v_r64, zero).astype(_BF), jnp.where(lane_lo, zero, v_ext).astype(_BF))]

    qi = lax.broadcasted_iota(jnp.int32, (2 * WINDOW, 2 * WINDOW), 0) % WINDOW
    kj = lax.broadcasted_iota(jnp.int32, (2 * WINDOW, 2 * WINDOW), 1)
    band = jnp.logical_and(kj > qi, kj <= qi + WINDOW)
    row_top = lax.broadcasted_iota(jnp.int32, (2 * WINDOW, 1), 0) < WINDOW
    neg_inf = jnp.float32(-jnp.inf)

    gate_cols = [(_OFF_GA + c, GATE_JOB_COLS) for c in range(0, 2 * D_MODEL, GATE_JOB_COLS)]
    gate_parts = []
    jobs_per_unit = -(-len(gate_cols) // ((T // WINDOW) * N_KV_HEADS))

    for n in range(T // WINDOW):
        if n == 0:
            ok = jnp.logical_and(band, jnp.logical_or(kj >= WINDOW, s > 0))
        else:
            ok = band
        r0 = n * WINDOW
        for j in range(N_KV_HEADS):
            q2 = jnp.concatenate([q_pairs[2 * j][r0:r0 + WINDOW], q_pairs[2 * j + 1][r0:r0 + WINDOW]], axis=0)
            kcat = jnp.concatenate([k_var[j][0][r0:r0 + 2 * WINDOW], k_var[j][1][r0:r0 + 2 * WINDOW]], axis=0)
            vcat = jnp.concatenate([v_var[j][0][r0:r0 + 2 * WINDOW], v_var[j][1][r0:r0 + 2 * WINDOW]], axis=0)
            sc = _dot_nt(q2, kcat)
            ps, rs = [], []
            for half in range(2):
                sh = jnp.where(ok, sc[:, 2 * WINDOW * half:2 * WINDOW * (half + 1)], neg_inf)
                snk = jnp.where(row_top, sinks_ref[4 * j + half], sinks_ref[4 * j + 2 + half])
                m = jnp.maximum(jnp.max(sh, axis=1, keepdims=True), snk)
                p = jnp.exp(sh - m)
                den = _rowsum(p) + jnp.exp(snk - m)
                ps.append(p.astype(_BF))
                rs.append(1.0 / den)
            o = _dot(jnp.concatenate(ps, axis=1), vcat)
            o = o * jnp.where(lane_lo, rs[0], rs[1])
            ya_ref[r0:r0 + WINDOW, LANES * (2 * j):LANES * (2 * j + 1)] = o[:WINDOW]
            ya_ref[r0:r0 + WINDOW, LANES * (2 * j + 1):LANES * (2 * j + 2)] = o[WINDOW:]
            for _ in range(min(jobs_per_unit, len(gate_cols) - len(gate_parts))):
                gate_parts.append(proj(*gate_cols[len(gate_parts)]))

    up_a = _dot(ya_ref[...].astype(_BF), w_ua_ref[...])

    HW = HGRN_WIDTH
    a0 = lb_ref[0:1, :]
    a1 = lb_ref[1:2, :]
    am = jnp.maximum(a0, a1)
    e0 = jnp.exp(a0 - am)
    lb = e0 / (e0 + jnp.exp(a1 - am))
    f = lb + (1.0 - lb) * jax.nn.sigmoid(proj(_OFF_HF, HW))
    kk = 1.0 - f
    g = jnp.log(f)
    hq = proj(_OFF_HQ, HW)
    hv = proj(_OFF_HI, HW)

    row = lax.broadcasted_iota(jnp.int32, (T, 1), 0)
    C = HGRN_CHUNK

    def rdown(t, d):
        return pltpu.roll(t, d, axis=0)

    def rup(t, d):
        return pltpu.roll(t, T - d, axis=0)

    def head(t, h):
        return t[:, HGRN_DK * h:HGRN_DK * (h + 1)]

    qk0 = hq * kk
    o_parts = [_rowsum(head(qk0, h)) * head(hv, h) for h in range(HGRN_HEADS)]

    small_levels = []
    bsz = 1
    lb_cum, lb_tot = g, g
    while bsz < SUBLANES:
        odd = (row % (2 * bsz)) >= bsz
        ex = jnp.exp(jnp.where(odd, lb_cum, lb_tot - lb_cum))
        w = jnp.where(odd, hq, kk) * ex
        small_levels.append((jnp.where(odd, w, 0.0).astype(_BF), jnp.where(odd, 0.0, w).astype(_BF)))
        prev_tot = rdown(lb_tot, bsz)
        lb_cum = lb_cum + jnp.where(odd, prev_tot, 0.0)
        lb_tot = lb_tot + jnp.where(odd, prev_tot, rup(lb_tot, bsz))
        bsz *= 2

    cum = [lb_cum[r:r + bsz] for r in range(0, T, bsz)]
    tot = [lb_tot[r:r + 1] for r in range(0, T, bsz)]
    kk_b = kk.astype(_BF)
    big_levels = []
    while bsz < C:
        q_rows, k_rows = [], []
        for j, r in enumerate(range(0, T, bsz)):
            if j % 2:
                q_rows.append(hq[r:r + bsz] * jnp.exp(cum[j]))
                k_rows.append(kk_b[r:r + bsz])
            else:
                k_rows.append((kk[r:r + bsz] * jnp.exp(tot[j] - cum[j])).astype(_BF))
        big_levels.append((bsz, jnp.concatenate(q_rows, axis=0).astype(_BF), jnp.concatenate(k_rows, axis=0)))
        cum = [jnp.concatenate([cum[j], cum[j + 1] + tot[j]], axis=0) for j in range(0, len(cum), 2)]
        tot = [tot[j] + tot[j + 1] for j in range(0, len(tot), 2)]
        bsz *= 2
    hv_b = hv.astype(_BF)
    lvl = lvl_ref[...]

    for c in range(T // C):
        rs_ = slice(c * C, (c + 1) * C)
        rh_ = slice(c * C // 2, (c + 1) * C // 2)
        q_in = (hq[rs_] * jnp.exp(cum[c])).astype(_BF)
        k_st = (kk[rs_] * jnp.exp(tot[c] - cum[c])).astype(_BF)
        dec = jnp.exp(tot[c])
        for h in range(HGRN_HEADS):
            cs_ = slice(HGRN_DK * h, HGRN_DK * (h + 1))
            st = st_ref[h]
            vc = hv_b[rs_, cs_]
            inter = _dot_nt(q_in[:, cs_], st.astype(_BF))
            amat = jnp.zeros((C, C), _F32)
            for li, (ql, kl) in enumerate(small_levels):
                amat = jnp.where(lvl == li, _dot_nt(ql[rs_, cs_], kl[rs_, cs_]), amat)
            for li, (bs, ql, kl) in enumerate(big_levels, start=len(small_levels)):
                p = _dot_nt(ql[rh_, cs_], kl[rs_, cs_])
                blocks = []
                for jb, r in enumerate(range(0, C, bs)):
                    if jb % 2:
                        blocks.append(jnp.where(lvl[r:r + bs] == li, p[(r - bs) // 2:(r + bs) // 2], amat[r:r + bs]))
                    else:
                        blocks.append(amat[r:r + bs])
                amat = jnp.concatenate(blocks, axis=0)
            intra = _dot(amat.astype(_BF), vc)
            st_ref[h] = st * dec[:, cs_] + _dot_tn(vc, k_st[:, cs_])
            o_ref[rs_, cs_] = inter + intra + o_parts[h][rs_]

    o = o_ref[...]
    hg = proj(_OFF_HG, HW)
    yh_parts = []
    for h in range(HGRN_HEADS):
        oh = head(o, h)
        ms = _rowsum(oh * oh) * (1.0 / HGRN_DK)
        yh_parts.append(oh * lax.rsqrt(ms + NORM_EPS))
    yh = jnp.concatenate(yh_parts, axis=1) * hgn_ref[...]
    yh = (yh * (hg * jax.nn.sigmoid(hg))).astype(_BF)

    up_h = _dot(yh, w_uh_ref[...])
    half = len(gate_parts) // 2
    z_ga = jnp.concatenate(gate_parts[:half], axis=1)
    z_gh = jnp.concatenate(gate_parts[half:], axis=1)
    merged = jax.nn.sigmoid(z_ga) * up_a + jax.nn.sigmoid(z_gh) * up_h
    h1 = x + _dot(merged.astype(_BF), w_out_ref[...])
    h1_ref[0] = h1

    xn2 = _rms(h1, gffn_ref[...])
    xp_ref[0] = _pack_bf16_pairs(xn2)
    x_hi = xn2.astype(_BF)
    x_lo = (xn2 - x_hi.astype(_F32)).astype(_BF)
    w_rt = w_rt_ref[...]
    w_hi = w_rt.astype(_BF)
    w_lo = (w_rt - w_hi.astype(_F32)).astype(_BF)
    logits = (_dot_nt(w_hi, x_hi) + (_dot_nt(w_hi, x_lo) + _dot_nt(w_lo, x_hi))) + b_r_ref[...]
    eidx = lax.broadcasted_iota(jnp.int32, (N_EXPERTS, T), 0)
    vals, sels, ohs = [], [], []
    l = logits
    for _ in range(TOP_K):
        m = jnp.max(l, axis=0, keepdims=True)
        sel = jnp.min(jnp.where(l == m, eidx, N_EXPERTS), axis=0, keepdims=True)
        oh = eidx == sel
        vals.append(m)
        sels.append(sel)
        ohs.append(oh)
        l = jnp.where(oh, neg_inf, l)
    es = [jnp.exp(v - vals[0]) for v in vals]
    den = es[0] + es[1] + es[2] + es[3]
    chosen = jnp.zeros((N_EXPERTS, T), _F32)
    for oh in ohs:
        chosen = chosen + jnp.where(oh, 1.0, 0.0)
    ui = lax.broadcasted_iota(jnp.int32, (T, T), 0)
    uj = lax.broadcasted_iota(jnp.int32, (T, T), 1)
    upper = jnp.where(ui < uj, 1.0, 0.0).astype(_BF)
    before = _dot(chosen.astype(_BF), upper) + cnt_sc[:, 0:1]
    for k in range(TOP_K):
        idx_ref[k:k + 1, :] = sels[k]
        gate_ref[k:k + 1, :] = es[k] / den
        rank_ref[k:k + 1, :] = jnp.sum(jnp.where(ohs[k], before, 0.0), axis=0, keepdims=True).astype(jnp.int32)
    cnt_sc[...] = cnt_sc[...] + _rowsum(chosen)
    cnt_ref[...] = cnt_sc[...]


def _const_spec(shape):
    return pl.BlockSpec(shape, lambda b, s: (0,) * len(shape), pipeline_mode=pl.Buffered(1))


def _mixer_call(batch0, B, x, sinks, cos_t, sin_t, gmix, w_in, b_in, lb, hgn, lvl, w_ua, w_uh, w_out, gffn, w_rt, b_r):
    S = x.shape[1]
    T = SEQ_TILE
    N = B * S
    nS = S // T
    tok_spec = lambda w: pl.BlockSpec((4, T), lambda b, s: (0, b * nS + s))
    out_shape = (
        jax.ShapeDtypeStruct((B, S, D_MODEL), _F32),
        jax.ShapeDtypeStruct((B, S, D_MODEL // 2), jnp.uint32),
        jax.ShapeDtypeStruct((TOP_K, N), jnp.int32),
        jax.ShapeDtypeStruct((TOP_K, N), jnp.int32),
        jax.ShapeDtypeStruct((TOP_K, N), _F32),
        jax.ShapeDtypeStruct((N_EXPERTS, LANES), _F32),
    )
    in_specs = [
        pl.BlockSpec(memory_space=pltpu.SMEM),
        pl.BlockSpec((1, T, D_MODEL), lambda b, s: (batch0 + b, s, 0)),
        pl.BlockSpec((T, LANES), lambda b, s: (s, 0)),
        pl.BlockSpec((T, LANES), lambda b, s: (s, 0)),
        _const_spec((1, D_MODEL)),
        _const_spec((D_MODEL, IN_COLS)),
        _const_spec((1, IN_COLS)),
        _const_spec((2, HGRN_WIDTH)),
        _const_spec((1, HGRN_WIDTH)),
        _const_spec((HGRN_CHUNK, HGRN_CHUNK)),
        _const_spec((ATTN_WIDTH, D_MODEL)),
        _const_spec((HGRN_WIDTH, D_MODEL)),
        _const_spec((D_MODEL, D_MODEL)),
        _const_spec((1, D_MODEL)),
        _const_spec((N_EXPERTS, D_MODEL)),
        _const_spec((N_EXPERTS, 1)),
    ]
    out_specs = (
        pl.BlockSpec((1, T, D_MODEL), lambda b, s: (b, s, 0)),
        pl.BlockSpec((1, T, D_MODEL // 2), lambda b, s: (b, s, 0)),
        tok_spec(0), tok_spec(0), tok_spec(0),
        pl.BlockSpec((N_EXPERTS, LANES), lambda b, s: (0, 0)),
    )
    scratch = [
        pltpu.VMEM((WINDOW, KV_WIDTH), _F32),
        pltpu.VMEM((WINDOW, KV_WIDTH), _F32),
        pltpu.VMEM((HGRN_HEADS, HGRN_DK, HGRN_DK), _F32),
        pltpu.VMEM((N_EXPERTS, LANES), _F32),
        pltpu.VMEM((T, ATTN_WIDTH), _F32),
        pltpu.VMEM((T, HGRN_WIDTH), _F32),
    ]
    return pl.pallas_call(
        _mixer_kernel,
        grid=(B, nS),
        in_specs=in_specs,
        out_specs=out_specs,
        out_shape=out_shape,
        scratch_shapes=scratch,
        compiler_params=pltpu.CompilerParams(
            dimension_semantics=("arbitrary", "arbitrary"), vmem_limit_bytes=VMEM_LIMIT),
        name="mixer",
    )(sinks, x, cos_t, sin_t, gmix, w_in, b_in, lb, hgn, lvl, w_ua, w_uh, w_out, gffn, w_rt, b_r)


def _dest_kernel(start_ref, idx_ref, rank_ref, dest_ref):
    idx = idx_ref[...]
    dest = rank_ref[...]
    for e in range(N_EXPERTS):
        dest = dest + jnp.where(idx == e, start_ref[e], 0)
    dest_ref[...] = dest


def _dest_call(pad_starts, idx, rank):
    K, N = idx.shape
    spec = pl.BlockSpec((K, DEST_TILE), lambda i: (0, i))
    return pl.pallas_call(
        _dest_kernel,
        grid=(N // DEST_TILE,),
        in_specs=[pl.BlockSpec(memory_space=pltpu.SMEM), spec, spec],
        out_specs=spec,
        out_shape=jax.ShapeDtypeStruct((K, N), jnp.int32),
        compiler_params=pltpu.CompilerParams(dimension_semantics=("arbitrary",)),
        name="dest",
    )(pad_starts, idx, rank)


def _sc_mesh():
    return plsc.VectorSubcoreMesh(core_axis_name="c", subcore_axis_name="s")


def _sc_worker_id():
    return lax.axis_index("s") * SC_CORES + lax.axis_index("c")


def _sc_scatter_rows(x, dest, n_rows):
    N, C = x.shape
    K = dest.shape[0]
    n_chunks = N // SC_CHUNK
    per_worker = n_chunks // SC_WORKERS
    dest3 = dest.reshape(K, n_chunks, SC_CHUNK)

    @functools.partial(
        pl.kernel, out_type=jax.ShapeDtypeStruct((n_rows, C), x.dtype), mesh=_sc_mesh(),
        scratch_types=[pltpu.VMEM((K, SC_CHUNK), jnp.int32), pltpu.VMEM((SC_CHUNK, C), x.dtype),
                       pltpu.SemaphoreType.DMA])
    def scatter(x_hbm, i_hbm, o_hbm, idx_v, rows_v, sem):
        base = _sc_worker_id() * per_worker

        @pl.loop(0, per_worker)
        def _(j):
            c = base + j
            off = pl.multiple_of(c * SC_CHUNK, SC_CHUNK)
            for k in range(K):
                pltpu.sync_copy(i_hbm.at[k, c], idx_v.at[k])
            pltpu.sync_copy(x_hbm.at[pl.ds(off, SC_CHUNK)], rows_v)
            for k in range(K):
                pltpu.async_copy(rows_v, o_hbm.at[idx_v.at[k]], sem).wait()

    return scatter(x, dest3)


def _sc_gather_rows(table, idx):
    M = idx.shape[0]
    C = table.shape[1]
    per_worker = M // SC_WORKERS
    n_chunks = per_worker // SC_CHUNK

    @functools.partial(
        pl.kernel, out_type=jax.ShapeDtypeStruct((M, C), table.dtype), mesh=_sc_mesh(),
        scratch_types=[pltpu.VMEM((SC_CHUNK,), jnp.int32), pltpu.VMEM((SC_CHUNK, C), table.dtype),
                       pltpu.SemaphoreType.DMA])
    def gather(t_hbm, i_hbm, o_hbm, idx_v, rows_v, sem):
        base = _sc_worker_id() * per_worker

        @pl.loop(0, n_chunks)
        def _(j):
            off = pl.multiple_of(base + j * SC_CHUNK, SC_CHUNK)
            pltpu.sync_copy(i_hbm.at[pl.ds(off, SC_CHUNK)], idx_v)
            pltpu.async_copy(t_hbm.at[idx_v], rows_v, sem).wait()
            pltpu.sync_copy(rows_v, o_hbm.at[pl.ds(off, SC_CHUNK)])

    return gather(table, idx)


def _expert_kernel(blk_e_ref, nb_ref, xs_ref, w1_ref, b1_ref, w2_ref, b2_ref, ys_ref, w1b_ref, w2b_ref):
    i = pl.program_id(0)
    active = i < nb_ref[0]
    new_expert = jnp.logical_or(i == 0, blk_e_ref[i] != blk_e_ref[jnp.maximum(i - 1, 0)])

    @pl.when(jnp.logical_and(active, new_expert))
    def _():
        for r in range(0, D_MODEL, WEIGHT_CAST_ROWS):
            w1b_ref[r:r + WEIGHT_CAST_ROWS, :] = w1_ref[0, r:r + WEIGHT_CAST_ROWS, :].astype(_BF)
        for r in range(0, D_EXPERT, WEIGHT_CAST_ROWS):
            w2b_ref[r:r + WEIGHT_CAST_ROWS, :] = w2_ref[0, r:r + WEIGHT_CAST_ROWS, :].astype(_BF)

    @pl.when(active)
    def _():
        xb = _unpack_bf16_pairs(xs_ref[...]).astype(_BF)
        h = _dot(xb, w1b_ref[...]) + b1_ref[0]
        glu = jnp.minimum(h[:, :D_EXPERT], SWIGLU_LIMIT)
        lin = jnp.clip(h[:, D_EXPERT:], -SWIGLU_LIMIT, SWIGLU_LIMIT)
        act = glu * jax.nn.sigmoid(SWIGLU_ALPHA * glu) * (lin + 1.0)
        y = _dot(act.astype(_BF), w2b_ref[...]) + b2_ref[0]
        ys_ref[...] = _pack_bf16_pairs(y)

    @pl.when(jnp.logical_not(active))
    def _():
        ys_ref[...] = jnp.zeros_like(ys_ref)


def _expert_call(blk_e, nb_used, xs, w1, b1, w2, b2):
    n_blocks = xs.shape[0] // MOE_BLOCK
    M = MOE_BLOCK
    grid_spec = pltpu.PrefetchScalarGridSpec(
        num_scalar_prefetch=2,
        grid=(n_blocks,),
        in_specs=[
            pl.BlockSpec((M, D_MODEL // 2), lambda i, be, nb: (jnp.minimum(i, nb[0] - 1), 0)),
            pl.BlockSpec((1, D_MODEL, 2 * D_EXPERT), lambda i, be, nb: (be[i], 0, 0)),
            pl.BlockSpec((1, 1, 2 * D_EXPERT), lambda i, be, nb: (be[i], 0, 0)),
            pl.BlockSpec((1, D_EXPERT, D_MODEL), lambda i, be, nb: (be[i], 0, 0)),
            pl.BlockSpec((1, 1, D_MODEL), lambda i, be, nb: (be[i], 0, 0)),
        ],
        out_specs=pl.BlockSpec((M, D_MODEL // 2), lambda i, be, nb: (i, 0)),
        scratch_shapes=[pltpu.VMEM((D_MODEL, 2 * D_EXPERT), _BF), pltpu.VMEM((D_EXPERT, D_MODEL), _BF)],
    )
    return pl.pallas_call(
        _expert_kernel,
        grid_spec=grid_spec,
        out_shape=jax.ShapeDtypeStruct(xs.shape, jnp.uint32),
        compiler_params=pltpu.CompilerParams(
            dimension_semantics=("arbitrary",), vmem_limit_bytes=VMEM_LIMIT),
        name="experts",
    )(blk_e, nb_used, xs, w1, b1, w2, b2)


def _combine_kernel(h1_ref, gate_ref, gfin_ref, y0_ref, y1_ref, y2_ref, y3_ref, out_ref):
    acc = h1_ref[...]
    gates = gate_ref[...]
    for k, y_ref in enumerate((y0_ref, y1_ref, y2_ref, y3_ref)):
        acc = acc + gates[:, k:k + 1] * _unpack_bf16_pairs(y_ref[...])
    out_ref[...] = _rms(acc, gfin_ref[...])


def _combine_call(h1, gates_t, gfin, yg, out_prev, tile0, n_total):
    N = h1.shape[0]
    T = COMBINE_TILE
    nT = N // T
    y_specs = [pl.BlockSpec((T, D_MODEL // 2), functools.partial(lambda i, k: (k * nT + i, 0), k=k))
               for k in range(TOP_K)]
    in_specs = [
        pl.BlockSpec((T, D_MODEL), lambda i: (i, 0)),
        pl.BlockSpec((T, TOP_K), lambda i: (i, 0)),
        pl.BlockSpec((1, D_MODEL), lambda i: (0, 0)),
    ] + y_specs
    args = [h1, gates_t, gfin, yg, yg, yg, yg]
    body, aliases = _combine_kernel, {}
    if out_prev is not None:
        in_specs.append(pl.BlockSpec(memory_space=pl.ANY))
        args.append(out_prev)
        aliases = {len(args) - 1: 0}
        body = lambda *refs: _combine_kernel(*refs[:7], refs[8])
    return pl.pallas_call(
        body,
        grid=(nT,),
        in_specs=in_specs,
        out_specs=pl.BlockSpec((T, D_MODEL), lambda i: (tile0 + i, 0)),
        out_shape=jax.ShapeDtypeStruct((n_total, D_MODEL), _F32),
        input_output_aliases=aliases,
        compiler_params=pltpu.CompilerParams(dimension_semantics=("arbitrary",)),
        name="combine",
    )(*args)


def _qk_column_permutation():
    half = HEAD_DIM // 2
    r = np.arange(half)

    def pair(base, ha, hb):
        return np.concatenate([base + ha * HEAD_DIM + r, base + hb * HEAD_DIM + r,
                               base + ha * HEAD_DIM + half + r, base + hb * HEAD_DIM + half + r])

    q = np.concatenate([pair(_OFF_Q, 2 * p, 2 * p + 1) for p in range(N_Q_HEADS // 2)])
    k = pair(_OFF_K, 0, 1)
    return np.concatenate([q, k, np.arange(_OFF_V, IN_COLS)])


def _hgrn_level_table():
    t = np.arange(HGRN_CHUNK)[:, None]
    u = np.arange(HGRN_CHUNK)[None, :]
    top = np.floor(np.log2(np.maximum(t ^ u, 1))).astype(np.int32)
    return np.where(t > u, top, -1).astype(np.int32)


def _rope_tables(S):
    half = HEAD_DIM // 2
    inv_freq = ROPE_THETA ** (-(jnp.arange(half, dtype=_F32) * 2.0 / HEAD_DIM))
    ang = jnp.arange(S, dtype=_F32)[:, None] * inv_freq[None, :]
    cos = jnp.tile(jnp.cos(ang), (1, 4))
    sin = jnp.tile(jnp.sin(ang), (1, 4))
    sign = jnp.where(jnp.arange(LANES) < 64, -1.0, 1.0).astype(_F32)
    return cos, sin * sign[None, :]


def kernel(x, norm_mix_g, w_in, b_in, attn_sinks, hgrn_lb, hgrn_norm_g, w_up_attn, w_up_hgrn, w_out,
           norm_ffn_g, w_router, b_router, w_moe1, b_moe1, w_moe2, b_moe2, norm_final_g):
    B, S, D = x.shape
    N = B * S
    assert D == D_MODEL and S % SEQ_TILE == 0 and SEQ_TILE % WINDOW == 0
    assert B % BATCH_PARTS == 0 and (N // BATCH_PARTS) % (SC_WORKERS * SC_CHUNK) == 0
    assert (N // BATCH_PARTS) % COMBINE_TILE == 0 and (N // BATCH_PARTS) % DEST_TILE == 0
    assert norm_mix_g.shape[0] == 1

    perm = _qk_column_permutation()
    cos_t, sin_t = _rope_tables(S)
    weights = (attn_sinks[0], cos_t, sin_t, norm_mix_g,
               w_in[0][:, perm].astype(_BF), b_in[:, perm], hgrn_lb, hgrn_norm_g, jnp.asarray(_hgrn_level_table()),
               w_up_attn[0].astype(_BF), w_up_hgrn[0].astype(_BF), w_out[0].astype(_BF),
               norm_ffn_g, w_router[0].T, b_router[0][:, None])

    Bp = B // BATCH_PARTS
    Np = Bp * S
    n_blocks = (Np * TOP_K) // MOE_BLOCK + N_EXPERTS
    out = None
    for part in range(BATCH_PARTS):
        h1, xp, idx, rank, gates, cnt = _mixer_call(part * Bp, Bp, x, *weights)

        counts = cnt[:, 0].astype(jnp.int32)
        padded = ((counts + MOE_BLOCK - 1) // MOE_BLOCK) * MOE_BLOCK
        pad_ends = jnp.cumsum(padded)
        pad_starts = pad_ends - padded
        dest = _dest_call(pad_starts, idx, rank)
        blk_start = jnp.arange(n_blocks, dtype=jnp.int32) * MOE_BLOCK
        blk_e = jnp.minimum(jnp.sum((pad_ends[None, :] <= blk_start[:, None]).astype(jnp.int32), axis=1),
                            N_EXPERTS - 1)
        nb_used = (pad_ends[-1:] // MOE_BLOCK).astype(jnp.int32)

        xs = _sc_scatter_rows(xp.reshape(Np, D // 2), dest, n_blocks * MOE_BLOCK)
        ys = _expert_call(blk_e, nb_used, xs, w_moe1[0], b_moe1[0][:, None, :], w_moe2[0], b_moe2[0][:, None, :])
        yg = _sc_gather_rows(ys, dest.reshape(TOP_K * Np))
        out = _combine_call(h1.reshape(Np, D), gates.T, norm_final_g[None, :], yg, out,
                            part * (Np // COMBINE_TILE), N)
    return out.reshape(B, S, D)
```

```python
import functools

import numpy as np
import jax
import jax.numpy as jnp
from jax import lax
from jax.experimental import pallas as pl
from jax.experimental.pallas import tpu as pltpu
from jax.experimental.pallas import tpu_sc as plsc

D_MODEL = 1024
HEAD_DIM = 64
N_Q_HEADS = 8
N_KV_HEADS = 2
ATTN_WIDTH = N_Q_HEADS * HEAD_DIM
KV_WIDTH = N_KV_HEADS * HEAD_DIM
WINDOW = 128
ROPE_THETA = 10000.0
HGRN_HEADS = 4
HGRN_DK = 128
HGRN_WIDTH = HGRN_HEADS * HGRN_DK
N_EXPERTS = 32
TOP_K = 4
D_EXPERT = 1024
SWIGLU_ALPHA = 1.702
SWIGLU_LIMIT = 7.0
MOE_BLOCK = 512
NORM_EPS = 1e-5

_OFF_Q = 0
_OFF_K = _OFF_Q + ATTN_WIDTH
_OFF_V = _OFF_K + KV_WIDTH
_OFF_HQ = _OFF_V + KV_WIDTH
_OFF_HF = _OFF_HQ + HGRN_WIDTH
_OFF_HI = _OFF_HF + HGRN_WIDTH
_OFF_HG = _OFF_HI + HGRN_WIDTH
_OFF_GA = _OFF_HG + HGRN_WIDTH
_OFF_GH = _OFF_GA + D_MODEL
IN_COLS = _OFF_GH + D_MODEL

LANES = 128
SUBLANES = 8
SEQ_TILE = 512
HGRN_CHUNK = 256
GATE_JOB_COLS = 256
COMBINE_TILE = 512
DEST_TILE = 8192
BATCH_PARTS = 2
WEIGHT_CAST_ROWS = 128
SC_CORES = 2
SC_WORKERS = 32
SC_CHUNK = 128
VMEM_LIMIT = 56 * 1024 * 1024

_BF = jnp.bfloat16
_F32 = jnp.float32


def _dot(a, b):
    return jnp.dot(a, b, preferred_element_type=_F32)


def _dot_nt(a, b, precision=None):
    return lax.dot_general(a, b, (((1,), (1,)), ((), ())), precision=precision,
                           preferred_element_type=_F32)


def _dot_tn(a, b):
    return lax.dot_general(a, b, (((0,), (0,)), ((), ())), preferred_element_type=_F32)


def _rowsum(x):
    return jnp.sum(x, axis=1, keepdims=True)


def _rms(x, g):
    ms = _rowsum(x * x) * (1.0 / x.shape[1])
    return x * lax.rsqrt(ms + NORM_EPS) * g


def _pack_bf16_pairs(x):
    n = x.shape[1] // 2
    lo = lax.bitcast_convert_type(x[:, :n].astype(_BF).astype(_F32), jnp.uint32)
    hi = lax.bitcast_convert_type(x[:, n:].astype(_BF).astype(_F32), jnp.uint32)
    return (lo >> 16) | (hi & jnp.uint32(0xFFFF0000))


def _unpack_bf16_pairs(u):
    lo = lax.bitcast_convert_type(u << 16, _F32)
    hi = lax.bitcast_convert_type(u & jnp.uint32(0xFFFF0000), _F32)
    return jnp.concatenate([lo, hi], axis=1)


def _mixer_kernel(sinks_ref, x_ref, cos_ref, sin_ref, gmix_ref, w_in_ref, b_in_ref, lb_ref,
                  hgn_ref, lvl_ref, w_ua_ref, w_uh_ref, w_out_ref, gffn_ref, w_rt_ref, b_r_ref,
                  h1_ref, xp_ref, idx_ref, rank_ref, gate_ref, cnt_ref,
                  kc_ref, vc_ref, st_ref, cnt_sc, ya_ref, o_ref):
    T = SEQ_TILE
    b = pl.program_id(0)
    s = pl.program_id(1)

    @pl.when(jnp.logical_and(b == 0, s == 0))
    def _():
        cnt_sc[...] = jnp.zeros_like(cnt_sc)

    @pl.when(s == 0)
    def _():
        kc_ref[...] = jnp.zeros_like(kc_ref)
        vc_ref[...] = jnp.zeros_like(vc_ref)
        st_ref[...] = jnp.zeros_like(st_ref)

    x = x_ref[0]
    xn = _rms(x, gmix_ref[...]).astype(_BF)

    def proj(off, width):
        return _dot(xn, w_in_ref[:, off:off + width]) + b_in_ref[:, off:off + width]

    cos = cos_ref[...]
    sin = sin_ref[...]

    def rope(t):
        return t * cos + pltpu.roll(t, 64, axis=1) * sin

    zq = proj(_OFF_Q, ATTN_WIDTH)
    scale = HEAD_DIM ** -0.5
    q_pairs = [(rope(zq[:, LANES * p:LANES * (p + 1)]) * scale).astype(_BF) for p in range(4)]
    k_rot = rope(proj(_OFF_K, KV_WIDTH))
    v_new = proj(_OFF_V, KV_WIDTH)

    k_ext = jnp.concatenate([kc_ref[...], k_rot], axis=0)
    v_ext = jnp.concatenate([vc_ref[...], v_new], axis=0)
    kc_ref[...] = k_rot[T - WINDOW:, :]
    vc_ref[...] = v_new[T - WINDOW:, :]

    lane = lax.broadcasted_iota(jnp.int32, (1, LANES), 1)
    slot_a = (lane % 64) < 32
    lane_lo = lane < 64
    k_r32 = pltpu.roll(k_ext, 32, axis=1)
    k_r96 = pltpu.roll(k_ext, 96, axis=1)
    v_r64 = pltpu.roll(v_ext, 64, axis=1)
    zero = jnp.zeros_like(k_ext)
    k_var = [(jnp.where(slot_a, k_ext, zero).astype(_BF), jnp.where(slot_a, zero, k_r32).astype(_BF)),
             (jnp.where(slot_a, k_r96, zero).astype(_BF), jnp.where(slot_a, zero, k_ext).astype(_BF))]
    v_var = [(jnp.where(lane_lo, v_ext, zero).astype(_BF), jnp.where(lane_lo, zero, v_r64).astype(_BF)),
             (jnp.where(lane_lo, v_r64, zero).astype(_BF), jnp.where(lane_lo, zero, v_ext).astype(_BF))]

    qi = lax.broadcasted_iota(jnp.int32, (2 * WINDOW, 2 * WINDOW), 0) % WINDOW
    kj = lax.broadcasted_iota(jnp.int32, (2 * WINDOW, 2 * WINDOW), 1)
    band = jnp.logical_and(kj > qi, kj <= qi + WINDOW)
    row_top = lax.broadcasted_iota(jnp.int32, (2 * WINDOW, 1), 0) < WINDOW
    neg_inf = jnp.float32(-jnp.inf)

    gate_cols = [(_OFF_GA + c, GATE_JOB_COLS) for c in range(0, 2 * D_MODEL, GATE_JOB_COLS)]
    gate_parts = []
    jobs_per_unit = -(-len(gate_cols) // ((T // WINDOW) * N_KV_HEADS))

    for n in range(T // WINDOW):
        if n == 0:
            ok = jnp.logical_and(band, jnp.logical_or(kj >= WINDOW, s > 0))
        else:
            ok = band
        r0 = n * WINDOW
        for j in range(N_KV_HEADS):
            q2 = jnp.concatenate([q_pairs[2 * j][r0:r0 + WINDOW], q_pairs[2 * j + 1][r0:r0 + WINDOW]], axis=0)
            kcat = jnp.concatenate([k_var[j][0][r0:r0 + 2 * WINDOW], k_var[j][1][r0:r0 + 2 * WINDOW]], axis=0)
            vcat = jnp.concatenate([v_var[j][0][r0:r0 + 2 * WINDOW], v_var[j][1][r0:r0 + 2 * WINDOW]], axis=0)
            sc = _dot_nt(q2, kcat)
            ps, rs = [], []
            for half in range(2):
                sh = jnp.where(ok, sc[:, 2 * WINDOW * half:2 * WINDOW * (half + 1)], neg_inf)
                snk = jnp.where(row_top, sinks_ref[4 * j + half], sinks_ref[4 * j + 2 + half])
                m = jnp.maximum(jnp.max(sh, axis=1, keepdims=True), snk)
                p = jnp.exp(sh - m)
                den = _rowsum(p) + jnp.exp(snk - m)
                ps.append(p.astype(_BF))
                rs.append(1.0 / den)
            o = _dot(jnp.concatenate(ps, axis=1), vcat)
            o = o * jnp.where(lane_lo, rs[0], rs[1])
            ya_ref[r0:r0 + WINDOW, LANES * (2 * j):LANES * (2 * j + 1)] = o[:WINDOW]
            ya_ref[r0:r0 + WINDOW, LANES * (2 * j + 1):LANES * (2 * j + 2)] = o[WINDOW:]
            for _ in range(min(jobs_per_unit, len(gate_cols) - len(gate_parts))):
                gate_parts.append(proj(*gate_cols[len(gate_parts)]))

    up_a = _dot(ya_ref[...].astype(_BF), w_ua_ref[...])

    HW = HGRN_WIDTH
    a0 = lb_ref[0:1, :]
    a1 = lb_ref[1:2, :]
    am = jnp.maximum(a0, a1)
    e0 = jnp.exp(a0 - am)
    lb = e0 / (e0 + jnp.exp(a1 - am))
    f = lb + (1.0 - lb) * jax.nn.sigmoid(proj(_OFF_HF, HW))
    kk = 1.0 - f
    g = jnp.log2(f)
    hq = proj(_OFF_HQ, HW)
    hv = proj(_OFF_HI, HW)

    row = lax.broadcasted_iota(jnp.int32, (T, 1), 0)
    C = HGRN_CHUNK

    def rdown(t, d):
        return pltpu.roll(t.reshape(T // SUBLANES, SUBLANES, HW), d, axis=1).reshape(T, HW)

    def rup(t, d):
        return pltpu.roll(t.reshape(T // SUBLANES, SUBLANES, HW), SUBLANES - d, axis=1).reshape(T, HW)

    def head(t, h):
        return t[:, HGRN_DK * h:HGRN_DK * (h + 1)]

    qk0 = hq * kk
    o_parts = [_rowsum(head(qk0, h)) * head(hv, h) for h in range(HGRN_HEADS)]

    small_levels = []
    bsz = 1
    lb_cum, lb_tot = g, g
    while bsz < SUBLANES:
        odd = (row % (2 * bsz)) >= bsz
        ex = jnp.exp2(jnp.where(odd, lb_cum, lb_tot - lb_cum))
        small_levels.append(((hq * ex).astype(_BF), (kk * ex).astype(_BF)))
        prev_tot = rdown(lb_tot, bsz)
        lb_cum = lb_cum + jnp.where(odd, prev_tot, 0.0)
        lb_tot = lb_tot + jnp.where(odd, prev_tot, rup(lb_tot, bsz))
        bsz *= 2

    cum = [lb_cum[r:r + bsz] for r in range(0, T, bsz)]
    tot = [lb_tot[r:r + 1] for r in range(0, T, bsz)]
    kk_b = kk.astype(_BF)
    big_levels = []
    while bsz < C:
        q_rows, k_rows = [], []
        for j, r in enumerate(range(0, T, bsz)):
            if j % 2:
                q_rows.append(hq[r:r + bsz] * jnp.exp2(cum[j]))
                k_rows.append(kk_b[r:r + bsz])
            else:
                k_rows.append((kk[r:r + bsz] * jnp.exp2(tot[j] - cum[j])).astype(_BF))
        big_levels.append((bsz, jnp.concatenate(q_rows, axis=0).astype(_BF), jnp.concatenate(k_rows, axis=0)))
        cum = [jnp.concatenate([cum[j], cum[j + 1] + tot[j]], axis=0) for j in range(0, len(cum), 2)]
        tot = [tot[j] + tot[j + 1] for j in range(0, len(tot), 2)]
        bsz *= 2
    hv_b = hv.astype(_BF)
    lvl = lvl_ref[...]

    for c in range(T // C):
        rs_ = slice(c * C, (c + 1) * C)
        rh_ = slice(c * C // 2, (c + 1) * C // 2)
        q_in = (hq[rs_] * jnp.exp2(cum[c])).astype(_BF)
        k_st = (kk[rs_] * jnp.exp2(tot[c] - cum[c])).astype(_BF)
        dec = jnp.exp2(tot[c])
        for h in range(HGRN_HEADS):
            cs_ = slice(HGRN_DK * h, HGRN_DK * (h + 1))
            st = st_ref[h]
            vc = hv_b[rs_, cs_]
            inter = _dot_nt(q_in[:, cs_], st.astype(_BF))
            amat = jnp.zeros((C, C), _F32)
            for li, (ql, kl) in enumerate(small_levels):
                amat = jnp.where(lvl == li, _dot_nt(ql[rs_, cs_], kl[rs_, cs_]), amat)
            for li, (bs, ql, kl) in enumerate(big_levels, start=len(small_levels)):
                p = _dot_nt(ql[rh_, cs_], kl[rs_, cs_])
                blocks = []
                for jb, r in enumerate(range(0, C, bs)):
                    if jb % 2:
                        blocks.append(jnp.where(lvl[r:r + bs] == li, p[(r - bs) // 2:(r + bs) // 2], amat[r:r + bs]))
                    else:
                        blocks.append(amat[r:r + bs])
                amat = jnp.concatenate(blocks, axis=0)
            intra = _dot(amat.astype(_BF), vc)
            st_ref[h] = st * dec[:, cs_] + _dot_tn(vc, k_st[:, cs_])
            o_ref[rs_, cs_] = inter + intra + o_parts[h][rs_]

    o = o_ref[...]
    hg = proj(_OFF_HG, HW)
    yh_parts = []
    for h in range(HGRN_HEADS):
        oh = head(o, h)
        ms = _rowsum(oh * oh) * (1.0 / HGRN_DK)
        yh_parts.append(oh * lax.rsqrt(ms + NORM_EPS))
    yh = jnp.concatenate(yh_parts, axis=1) * hgn_ref[...]
    yh = (yh * (hg * jax.nn.sigmoid(hg))).astype(_BF)

    up_h = _dot(yh, w_uh_ref[...])
    half = len(gate_parts) // 2
    z_ga = jnp.concatenate(gate_parts[:half], axis=1)
    z_gh = jnp.concatenate(gate_parts[half:], axis=1)
    merged = jax.nn.sigmoid(z_ga) * up_a + jax.nn.sigmoid(z_gh) * up_h
    h1 = x + _dot(merged.astype(_BF), w_out_ref[...])
    h1_ref[0] = h1

    xn2 = _rms(h1, gffn_ref[...])
    xp_ref[0] = _pack_bf16_pairs(xn2)
    x_hi = xn2.astype(_BF)
    x_lo = (xn2 - x_hi.astype(_F32)).astype(_BF)
    w_rt = w_rt_ref[...]
    w_hi = w_rt.astype(_BF)
    w_lo = (w_rt - w_hi.astype(_F32)).astype(_BF)
    logits = (_dot_nt(w_hi, x_hi) + (_dot_nt(w_hi, x_lo) + _dot_nt(w_lo, x_hi))) + b_r_ref[...]
    eidx = lax.broadcasted_iota(jnp.int32, (N_EXPERTS, T), 0)
    vals, sels, ohs = [], [], []
    l = logits
    for _ in range(TOP_K):
        m = jnp.max(l, axis=0, keepdims=True)
        sel = jnp.min(jnp.where(l == m, eidx, N_EXPERTS), axis=0, keepdims=True)
        oh = eidx == sel
        vals.append(m)
        sels.append(sel)
        ohs.append(oh)
        l = jnp.where(oh, neg_inf, l)
    es = [jnp.exp(v - vals[0]) for v in vals]
    den = es[0] + es[1] + es[2] + es[3]
    chosen = jnp.zeros((N_EXPERTS, T), _F32)
    for oh in ohs:
        chosen = chosen + jnp.where(oh, 1.0, 0.0)
    ui = lax.broadcasted_iota(jnp.int32, (T, T), 0)
    uj = lax.broadcasted_iota(jnp.int32, (T, T), 1)
    upper = jnp.where(ui < uj, 1.0, 0.0).astype(_BF)
    before = _dot(chosen.astype(_BF), upper) + cnt_sc[:, 0:1]
    for k in range(TOP_K):
        idx_ref[k:k + 1, :] = sels[k]
        gate_ref[k:k + 1, :] = es[k] / den
        rank_ref[k:k + 1, :] = jnp.sum(jnp.where(ohs[k], before, 0.0), axis=0, keepdims=True).astype(jnp.int32)
    cnt_sc[...] = cnt_sc[...] + _rowsum(chosen)
    cnt_ref[...] = cnt_sc[...]


def _const_spec(shape):
    return pl.BlockSpec(shape, lambda b, s: (0,) * len(shape), pipeline_mode=pl.Buffered(1))


def _mixer_call(batch0, B, x, sinks, cos_t, sin_t, gmix, w_in, b_in, lb, hgn, lvl, w_ua, w_uh, w_out, gffn, w_rt, b_r):
    S = x.shape[1]
    T = SEQ_TILE
    N = B * S
    nS = S // T
    tok_spec = lambda w: pl.BlockSpec((4, T), lambda b, s: (0, b * nS + s))
    out_shape = (
        jax.ShapeDtypeStruct((B, S, D_MODEL), _F32),
        jax.ShapeDtypeStruct((B, S, D_MODEL // 2), jnp.uint32),
        jax.ShapeDtypeStruct((TOP_K, N), jnp.int32),
        jax.ShapeDtypeStruct((TOP_K, N), jnp.int32),
        jax.ShapeDtypeStruct((TOP_K, N), _F32),
        jax.ShapeDtypeStruct((N_EXPERTS, LANES), _F32),
    )
    in_specs = [
        pl.BlockSpec(memory_space=pltpu.SMEM),
        pl.BlockSpec((1, T, D_MODEL), lambda b, s: (batch0 + b, s, 0)),
        pl.BlockSpec((T, LANES), lambda b, s: (s, 0)),
        pl.BlockSpec((T, LANES), lambda b, s: (s, 0)),
        _const_spec((1, D_MODEL)),
        _const_spec((D_MODEL, IN_COLS)),
        _const_spec((1, IN_COLS)),
        _const_spec((2, HGRN_WIDTH)),
        _const_spec((1, HGRN_WIDTH)),
        _const_spec((HGRN_CHUNK, HGRN_CHUNK)),
        _const_spec((ATTN_WIDTH, D_MODEL)),
        _const_spec((HGRN_WIDTH, D_MODEL)),
        _const_spec((D_MODEL, D_MODEL)),
        _const_spec((1, D_MODEL)),
        _const_spec((N_EXPERTS, D_MODEL)),
        _const_spec((N_EXPERTS, 1)),
    ]
    out_specs = (
        pl.BlockSpec((1, T, D_MODEL), lambda b, s: (b, s, 0)),
        pl.BlockSpec((1, T, D_MODEL // 2), lambda b, s: (b, s, 0)),
        tok_spec(0), tok_spec(0), tok_spec(0),
        pl.BlockSpec((N_EXPERTS, LANES), lambda b, s: (0, 0)),
    )
    scratch = [
        pltpu.VMEM((WINDOW, KV_WIDTH), _F32),
        pltpu.VMEM((WINDOW, KV_WIDTH), _F32),
        pltpu.VMEM((HGRN_HEADS, HGRN_DK, HGRN_DK), _F32),
        pltpu.VMEM((N_EXPERTS, LANES), _F32),
        pltpu.VMEM((T, ATTN_WIDTH), _F32),
        pltpu.VMEM((T, HGRN_WIDTH), _F32),
    ]
    return pl.pallas_call(
        _mixer_kernel,
        grid=(B, nS),
        in_specs=in_specs,
        out_specs=out_specs,
        out_shape=out_shape,
        scratch_shapes=scratch,
        compiler_params=pltpu.CompilerParams(
            dimension_semantics=("arbitrary", "arbitrary"), vmem_limit_bytes=VMEM_LIMIT),
        name="mixer",
    )(sinks, x, cos_t, sin_t, gmix, w_in, b_in, lb, hgn, lvl, w_ua, w_uh, w_out, gffn, w_rt, b_r)


def _dest_kernel(start_ref, idx_ref, rank_ref, dest_ref):
    idx = idx_ref[...]
    dest = rank_ref[...]
    for e in range(N_EXPERTS):
        dest = dest + jnp.where(idx == e, start_ref[e], 0)
    dest_ref[...] = dest


def _dest_call(pad_starts, idx, rank):
    K, N = idx.shape
    spec = pl.BlockSpec((K, DEST_TILE), lambda i: (0, i))
    return pl.pallas_call(
        _dest_kernel,
        grid=(N // DEST_TILE,),
        in_specs=[pl.BlockSpec(memory_space=pltpu.SMEM), spec, spec],
        out_specs=spec,
        out_shape=jax.ShapeDtypeStruct((K, N), jnp.int32),
        compiler_params=pltpu.CompilerParams(dimension_semantics=("arbitrary",)),
        name="dest",
    )(pad_starts, idx, rank)


def _sc_mesh():
    return plsc.VectorSubcoreMesh(core_axis_name="c", subcore_axis_name="s")


def _sc_worker_id():
    return lax.axis_index("s") * SC_CORES + lax.axis_index("c")


def _sc_scatter_rows(x, dest, n_rows):
    N, C = x.shape
    K = dest.shape[0]
    n_chunks = N // SC_CHUNK
    per_worker = n_chunks // SC_WORKERS
    dest3 = dest.reshape(K, n_chunks, SC_CHUNK)

    @functools.partial(
        pl.kernel, out_type=jax.ShapeDtypeStruct((n_rows, C), x.dtype), mesh=_sc_mesh(),
        scratch_types=[pltpu.VMEM((K, SC_CHUNK), jnp.int32), pltpu.VMEM((SC_CHUNK, C), x.dtype),
                       pltpu.SemaphoreType.DMA])
    def scatter(x_hbm, i_hbm, o_hbm, idx_v, rows_v, sem):
        base = _sc_worker_id() * per_worker

        @pl.loop(0, per_worker)
        def _(j):
            c = base + j
            off = pl.multiple_of(c * SC_CHUNK, SC_CHUNK)
            for k in range(K):
                pltpu.sync_copy(i_hbm.at[k, c], idx_v.at[k])
            pltpu.sync_copy(x_hbm.at[pl.ds(off, SC_CHUNK)], rows_v)
            for k in range(K):
                pltpu.async_copy(rows_v, o_hbm.at[idx_v.at[k]], sem).wait()

    return scatter(x, dest3)


def _sc_gather_rows(table, idx):
    M = idx.shape[0]
    C = table.shape[1]
    per_worker = M // SC_WORKERS
    n_chunks = per_worker // SC_CHUNK

    @functools.partial(
        pl.kernel, out_type=jax.ShapeDtypeStruct((M, C), table.dtype), mesh=_sc_mesh(),
        scratch_types=[pltpu.VMEM((SC_CHUNK,), jnp.int32), pltpu.VMEM((SC_CHUNK, C), table.dtype),
                       pltpu.SemaphoreType.DMA])
    def gather(t_hbm, i_hbm, o_hbm, idx_v, rows_v, sem):
        base = _sc_worker_id() * per_worker

        @pl.loop(0, n_chunks)
        def _(j):
            off = pl.multiple_of(base + j * SC_CHUNK, SC_CHUNK)
            pltpu.sync_copy(i_hbm.at[pl.ds(off, SC_CHUNK)], idx_v)
            pltpu.async_copy(t_hbm.at[idx_v], rows_v, sem).wait()
            pltpu.sync_copy(rows_v, o_hbm.at[pl.ds(off, SC_CHUNK)])

    return gather(table, idx)


def _expert_kernel(blk_e_ref, rows_ref, nb_ref, xs_ref, w1_ref, b1_ref, w2_ref, b2_ref, ys_ref, w1b_ref, w2b_ref):
    del nb_ref
    i = pl.program_id(0)
    rows = rows_ref[i]
    new_expert = jnp.logical_or(i == 0, blk_e_ref[i] != blk_e_ref[jnp.maximum(i - 1, 0)])

    @pl.when(jnp.logical_and(rows > 0, new_expert))
    def _():
        for r in range(0, D_MODEL, WEIGHT_CAST_ROWS):
            w1b_ref[r:r + WEIGHT_CAST_ROWS, :] = w1_ref[0, r:r + WEIGHT_CAST_ROWS, :].astype(_BF)
        for r in range(0, D_EXPERT, WEIGHT_CAST_ROWS):
            w2b_ref[r:r + WEIGHT_CAST_ROWS, :] = w2_ref[0, r:r + WEIGHT_CAST_ROWS, :].astype(_BF)

    def mlp(m):
        xb = _unpack_bf16_pairs(xs_ref[0:m, :]).astype(_BF)
        h = _dot(xb, w1b_ref[...]) + b1_ref[0]
        glu = jnp.minimum(h[:, :D_EXPERT], SWIGLU_LIMIT)
        lin = jnp.clip(h[:, D_EXPERT:], -SWIGLU_LIMIT, SWIGLU_LIMIT)
        act = glu * jax.nn.sigmoid(SWIGLU_ALPHA * glu) * (lin + 1.0)
        y = _dot(act.astype(_BF), w2b_ref[...]) + b2_ref[0]
        ys_ref[0:m, :] = _pack_bf16_pairs(y)

    half = MOE_BLOCK // 2

    @pl.when(rows > half)
    def _():
        mlp(MOE_BLOCK)

    @pl.when(jnp.logical_and(rows > 0, rows <= half))
    def _():
        mlp(half)
        ys_ref[half:, :] = jnp.zeros((MOE_BLOCK - half, D_MODEL // 2), jnp.uint32)

    @pl.when(rows == 0)
    def _():
        ys_ref[...] = jnp.zeros_like(ys_ref)


def _expert_call(blk_e, blk_rows, nb_used, xs, w1, b1, w2, b2):
    n_blocks = xs.shape[0] // MOE_BLOCK
    M = MOE_BLOCK
    grid_spec = pltpu.PrefetchScalarGridSpec(
        num_scalar_prefetch=3,
        grid=(n_blocks,),
        in_specs=[
            pl.BlockSpec((M, D_MODEL // 2), lambda i, be, br, nb: (jnp.minimum(i, nb[0] - 1), 0)),
            pl.BlockSpec((1, D_MODEL, 2 * D_EXPERT), lambda i, be, br, nb: (be[i], 0, 0)),
            pl.BlockSpec((1, 1, 2 * D_EXPERT), lambda i, be, br, nb: (be[i], 0, 0)),
            pl.BlockSpec((1, D_EXPERT, D_MODEL), lambda i, be, br, nb: (be[i], 0, 0)),
            pl.BlockSpec((1, 1, D_MODEL), lambda i, be, br, nb: (be[i], 0, 0)),
        ],
        out_specs=pl.BlockSpec((M, D_MODEL // 2), lambda i, be, br, nb: (i, 0)),
        scratch_shapes=[pltpu.VMEM((D_MODEL, 2 * D_EXPERT), _BF), pltpu.VMEM((D_EXPERT, D_MODEL), _BF)],
    )
    return pl.pallas_call(
        _expert_kernel,
        grid_spec=grid_spec,
        out_shape=jax.ShapeDtypeStruct(xs.shape, jnp.uint32),
        compiler_params=pltpu.CompilerParams(
            dimension_semantics=("arbitrary",), vmem_limit_bytes=VMEM_LIMIT),
        name="experts",
    )(blk_e, blk_rows, nb_used, xs, w1, b1, w2, b2)


def _combine_kernel(h1_ref, gate_ref, gfin_ref, y0_ref, y1_ref, y2_ref, y3_ref, out_ref):
    acc = h1_ref[...]
    gates = gate_ref[...]
    for k, y_ref in enumerate((y0_ref, y1_ref, y2_ref, y3_ref)):
        acc = acc + gates[:, k:k + 1] * _unpack_bf16_pairs(y_ref[...])
    out_ref[...] = _rms(acc, gfin_ref[...])


def _combine_call(h1, gates_t, gfin, yg, out_prev, tile0, n_total):
    N = h1.shape[0]
    T = COMBINE_TILE
    nT = N // T
    y_specs = [pl.BlockSpec((T, D_MODEL // 2), functools.partial(lambda i, k: (k * nT + i, 0), k=k))
               for k in range(TOP_K)]
    in_specs = [
        pl.BlockSpec((T, D_MODEL), lambda i: (i, 0)),
        pl.BlockSpec((T, TOP_K), lambda i: (i, 0)),
        pl.BlockSpec((1, D_MODEL), lambda i: (0, 0)),
    ] + y_specs
    args = [h1, gates_t, gfin, yg, yg, yg, yg]
    body, aliases = _combine_kernel, {}
    if out_prev is not None:
        in_specs.append(pl.BlockSpec(memory_space=pl.ANY))
        args.append(out_prev)
        aliases = {len(args) - 1: 0}
        body = lambda *refs: _combine_kernel(*refs[:7], refs[8])
    return pl.pallas_call(
        body,
        grid=(nT,),
        in_specs=in_specs,
        out_specs=pl.BlockSpec((T, D_MODEL), lambda i: (tile0 + i, 0)),
        out_shape=jax.ShapeDtypeStruct((n_total, D_MODEL), _F32),
        input_output_aliases=aliases,
        compiler_params=pltpu.CompilerParams(dimension_semantics=("arbitrary",)),
        name="combine",
    )(*args)


def _qk_column_permutation():
    half = HEAD_DIM // 2
    r = np.arange(half)

    def pair(base, ha, hb):
        return np.concatenate([base + ha * HEAD_DIM + r, base + hb * HEAD_DIM + r,
                               base + ha * HEAD_DIM + half + r, base + hb * HEAD_DIM + half + r])

    q = np.concatenate([pair(_OFF_Q, 2 * p, 2 * p + 1) for p in range(N_Q_HEADS // 2)])
    k = pair(_OFF_K, 0, 1)
    return np.concatenate([q, k, np.arange(_OFF_V, IN_COLS)])


def _hgrn_level_table():
    t = np.arange(HGRN_CHUNK)[:, None]
    u = np.arange(HGRN_CHUNK)[None, :]
    top = np.floor(np.log2(np.maximum(t ^ u, 1))).astype(np.int32)
    return np.where(t > u, top, -1).astype(np.int32)


def _rope_tables(S):
    half = HEAD_DIM // 2
    inv_freq = ROPE_THETA ** (-(jnp.arange(half, dtype=_F32) * 2.0 / HEAD_DIM))
    ang = jnp.arange(S, dtype=_F32)[:, None] * inv_freq[None, :]
    cos = jnp.tile(jnp.cos(ang), (1, 4))
    sin = jnp.tile(jnp.sin(ang), (1, 4))
    sign = jnp.where(jnp.arange(LANES) < 64, -1.0, 1.0).astype(_F32)
    return cos, sin * sign[None, :]


def kernel(x, norm_mix_g, w_in, b_in, attn_sinks, hgrn_lb, hgrn_norm_g, w_up_attn, w_up_hgrn, w_out,
           norm_ffn_g, w_router, b_router, w_moe1, b_moe1, w_moe2, b_moe2, norm_final_g):
    B, S, D = x.shape
    N = B * S
    assert D == D_MODEL and S % SEQ_TILE == 0 and SEQ_TILE % WINDOW == 0
    assert B % BATCH_PARTS == 0 and (N // BATCH_PARTS) % (SC_WORKERS * SC_CHUNK) == 0
    assert (N // BATCH_PARTS) % COMBINE_TILE == 0 and (N // BATCH_PARTS) % DEST_TILE == 0
    assert norm_mix_g.shape[0] == 1

    perm = _qk_column_permutation()
    cos_t, sin_t = _rope_tables(S)
    weights = (attn_sinks[0], cos_t, sin_t, norm_mix_g,
               w_in[0][:, perm].astype(_BF), b_in[:, perm], hgrn_lb, hgrn_norm_g, jnp.asarray(_hgrn_level_table()),
               w_up_attn[0].astype(_BF), w_up_hgrn[0].astype(_BF), w_out[0].astype(_BF),
               norm_ffn_g, w_router[0].T, b_router[0][:, None])

    Bp = B // BATCH_PARTS
    Np = Bp * S
    n_blocks = (Np * TOP_K) // MOE_BLOCK + N_EXPERTS
    out = None
    for part in range(BATCH_PARTS):
        h1, xp, idx, rank, gates, cnt = _mixer_call(part * Bp, Bp, x, *weights)

        counts = cnt[:, 0].astype(jnp.int32)
        padded = ((counts + MOE_BLOCK - 1) // MOE_BLOCK) * MOE_BLOCK
        pad_ends = jnp.cumsum(padded)
        pad_starts = pad_ends - padded
        dest = _dest_call(pad_starts, idx, rank)
        blk_start = jnp.arange(n_blocks, dtype=jnp.int32) * MOE_BLOCK
        blk_e = jnp.minimum(jnp.sum((pad_ends[None, :] <= blk_start[:, None]).astype(jnp.int32), axis=1),
                            N_EXPERTS - 1)
        nb_used = (pad_ends[-1:] // MOE_BLOCK).astype(jnp.int32)
        blk_rows = jnp.clip((pad_starts + counts)[blk_e] - blk_start, 0, MOE_BLOCK)
        blk_rows = jnp.where(blk_start < pad_ends[-1], blk_rows, 0).astype(jnp.int32)

        xs = _sc_scatter_rows(xp.reshape(Np, D // 2), dest, n_blocks * MOE_BLOCK)
        ys = _expert_call(blk_e, blk_rows, nb_used, xs, w_moe1[0], b_moe1[0][:, None, :], w_moe2[0], b_moe2[0][:, None, :])
        yg = _sc_gather_rows(ys, dest.reshape(TOP_K * Np))
        out = _combine_call(h1.reshape(Np, D), gates.T, norm_final_g[None, :], yg, out,
                            part * (Np // COMBINE_TILE), N)
    return out.reshape(B, S, D)
```

```python
import functools

import numpy as np
import jax
import jax.numpy as jnp
from jax import lax
from jax.experimental import pallas as pl
from jax.experimental.pallas import tpu as pltpu
from jax.experimental.pallas import tpu_sc as plsc

D_MODEL = 1024
HEAD_DIM = 64
N_Q_HEADS = 8
N_KV_HEADS = 2
ATTN_WIDTH = N_Q_HEADS * HEAD_DIM
KV_WIDTH = N_KV_HEADS * HEAD_DIM
WINDOW = 128
ROPE_THETA = 10000.0
HGRN_HEADS = 4
HGRN_DK = 128
HGRN_WIDTH = HGRN_HEADS * HGRN_DK
N_EXPERTS = 32
TOP_K = 4
D_EXPERT = 1024
SWIGLU_ALPHA = 1.702
SWIGLU_LIMIT = 7.0
MOE_BLOCK = 512
NORM_EPS = 1e-5

_OFF_Q = 0
_OFF_K = _OFF_Q + ATTN_WIDTH
_OFF_V = _OFF_K + KV_WIDTH
_OFF_HQ = _OFF_V + KV_WIDTH
_OFF_HF = _OFF_HQ + HGRN_WIDTH
_OFF_HI = _OFF_HF + HGRN_WIDTH
_OFF_HG = _OFF_HI + HGRN_WIDTH
_OFF_GA = _OFF_HG + HGRN_WIDTH
_OFF_GH = _OFF_GA + D_MODEL
IN_COLS = _OFF_GH + D_MODEL

LANES = 128
SUBLANES = 8
SEQ_TILE = 512
HGRN_CHUNK = 256
GATE_JOB_COLS = 256
COMBINE_TILE = 512
DEST_TILE = 8192
BATCH_PARTS = 2
WEIGHT_CAST_ROWS = 128
SC_CORES = 2
SC_WORKERS = 32
SC_CHUNK = 128
VMEM_LIMIT = 56 * 1024 * 1024

_BF = jnp.bfloat16
_F32 = jnp.float32


def _dot(a, b):
    return jnp.dot(a, b, preferred_element_type=_F32)


def _dot_nt(a, b, precision=None):
    return lax.dot_general(a, b, (((1,), (1,)), ((), ())), precision=precision,
                           preferred_element_type=_F32)


def _dot_tn(a, b):
    return lax.dot_general(a, b, (((0,), (0,)), ((), ())), preferred_element_type=_F32)


def _rowsum(x):
    return jnp.sum(x, axis=1, keepdims=True)


def _rms(x, g):
    ms = _rowsum(x * x) * (1.0 / x.shape[1])
    return x * lax.rsqrt(ms + NORM_EPS) * g


def _pack_bf16_pairs(x):
    n = x.shape[1] // 2
    lo = lax.bitcast_convert_type(x[:, :n].astype(_BF).astype(_F32), jnp.uint32)
    hi = lax.bitcast_convert_type(x[:, n:].astype(_BF).astype(_F32), jnp.uint32)
    return (lo >> 16) | (hi & jnp.uint32(0xFFFF0000))


def _unpack_bf16_pairs(u):
    lo = lax.bitcast_convert_type(u << 16, _F32)
    hi = lax.bitcast_convert_type(u & jnp.uint32(0xFFFF0000), _F32)
    return jnp.concatenate([lo, hi], axis=1)


def _mixer_kernel(sinks_ref, x_ref, cos_ref, sin_ref, gmix_ref, w_in_ref, b_in_ref, lb_ref,
                  hgn_ref, lvl_ref, w_ua_ref, w_uh_ref, w_out_ref, gffn_ref, w_rt_ref, b_r_ref,
                  h1_ref, xp_ref, idx_ref, rank_ref, gate_ref, cnt_ref,
                  kc_ref, vc_ref, st_ref, cnt_sc, ya_ref, o_ref):
    T = SEQ_TILE
    b = pl.program_id(0)
    s = pl.program_id(1)

    @pl.when(jnp.logical_and(b == 0, s == 0))
    def _():
        cnt_sc[...] = jnp.zeros_like(cnt_sc)

    @pl.when(s == 0)
    def _():
        kc_ref[...] = jnp.zeros_like(kc_ref)
        vc_ref[...] = jnp.zeros_like(vc_ref)
        st_ref[...] = jnp.zeros_like(st_ref)

    x = x_ref[0]
    xn = _rms(x, gmix_ref[...]).astype(_BF)

    def proj(off, width):
        return _dot(xn, w_in_ref[:, off:off + width]) + b_in_ref[:, off:off + width]

    cos = cos_ref[...]
    sin = sin_ref[...]

    def rope(t):
        return t * cos + pltpu.roll(t, 64, axis=1) * sin

    zq = proj(_OFF_Q, ATTN_WIDTH)
    scale = HEAD_DIM ** -0.5
    q_pairs = [(rope(zq[:, LANES * p:LANES * (p + 1)]) * scale).astype(_BF) for p in range(4)]
    k_rot = rope(proj(_OFF_K, KV_WIDTH))
    v_new = proj(_OFF_V, KV_WIDTH)

    k_ext = jnp.concatenate([kc_ref[...], k_rot], axis=0)
    v_ext = jnp.concatenate([vc_ref[...], v_new], axis=0)
    kc_ref[...] = k_rot[T - WINDOW:, :]
    vc_ref[...] = v_new[T - WINDOW:, :]

    lane = lax.broadcasted_iota(jnp.int32, (1, LANES), 1)
    slot_a = (lane % 64) < 32
    lane_lo = lane < 64
    k_r32 = pltpu.roll(k_ext, 32, axis=1)
    k_r96 = pltpu.roll(k_ext, 96, axis=1)
    v_r64 = pltpu.roll(v_ext, 64, axis=1)
    zero = jnp.zeros_like(k_ext)
    k_var = [(jnp.where(slot_a, k_ext, zero).astype(_BF), jnp.where(slot_a, zero, k_r32).astype(_BF)),
             (jnp.where(slot_a, k_r96, zero).astype(_BF), jnp.where(slot_a, zero, k_ext).astype(_BF))]
    v_var = [(jnp.where(lane_lo, v_ext, zero).astype(_BF), jnp.where(lane_lo, zero, v_r64).astype(_BF)),
             (jnp.where(lane_lo, v_r64, zero).astype(_BF), jnp.where(lane_lo, zero, v_ext).astype(_BF))]

    qi = lax.broadcasted_iota(jnp.int32, (2 * WINDOW, 2 * WINDOW), 0) % WINDOW
    kj = lax.broadcasted_iota(jnp.int32, (2 * WINDOW, 2 * WINDOW), 1)
    band = jnp.logical_and(kj > qi, kj <= qi + WINDOW)
    row_top = lax.broadcasted_iota(jnp.int32, (2 * WINDOW, 1), 0) < WINDOW
    neg_inf = jnp.float32(-jnp.inf)

    gate_cols = [(_OFF_GA + c, GATE_JOB_COLS) for c in range(0, 2 * D_MODEL, GATE_JOB_COLS)]
    gate_parts = []
    jobs_per_unit = -(-len(gate_cols) // ((T // WINDOW) * N_KV_HEADS))

    for n in range(T // WINDOW):
        if n == 0:
            ok = jnp.logical_and(band, jnp.logical_or(kj >= WINDOW, s > 0))
        else:
            ok = band
        r0 = n * WINDOW
        for j in range(N_KV_HEADS):
            q2 = jnp.concatenate([q_pairs[2 * j][r0:r0 + WINDOW], q_pairs[2 * j + 1][r0:r0 + WINDOW]], axis=0)
            kcat = jnp.concatenate([k_var[j][0][r0:r0 + 2 * WINDOW], k_var[j][1][r0:r0 + 2 * WINDOW]], axis=0)
            vcat = jnp.concatenate([v_var[j][0][r0:r0 + 2 * WINDOW], v_var[j][1][r0:r0 + 2 * WINDOW]], axis=0)
            sc = _dot_nt(q2, kcat)
            ps, rs = [], []
            for half in range(2):
                sh = jnp.where(ok, sc[:, 2 * WINDOW * half:2 * WINDOW * (half + 1)], neg_inf)
                snk = jnp.where(row_top, sinks_ref[4 * j + half], sinks_ref[4 * j + 2 + half])
                m = jnp.maximum(jnp.max(sh, axis=1, keepdims=True), snk)
                p = jnp.exp(sh - m)
                den = _rowsum(p) + jnp.exp(snk - m)
                ps.append(p.astype(_BF))
                rs.append(1.0 / den)
            o = _dot(jnp.concatenate(ps, axis=1), vcat)
            o = o * jnp.where(lane_lo, rs[0], rs[1])
            ya_ref[r0:r0 + WINDOW, LANES * (2 * j):LANES * (2 * j + 1)] = o[:WINDOW]
            ya_ref[r0:r0 + WINDOW, LANES * (2 * j + 1):LANES * (2 * j + 2)] = o[WINDOW:]
            for _ in range(min(jobs_per_unit, len(gate_cols) - len(gate_parts))):
                gate_parts.append(proj(*gate_cols[len(gate_parts)]))

    up_a = _dot(ya_ref[...].astype(_BF), w_ua_ref[...])

    HW = HGRN_WIDTH
    a0 = lb_ref[0:1, :]
    a1 = lb_ref[1:2, :]
    am = jnp.maximum(a0, a1)
    e0 = jnp.exp(a0 - am)
    lb = e0 / (e0 + jnp.exp(a1 - am))
    f = lb + (1.0 - lb) * jax.nn.sigmoid(proj(_OFF_HF, HW))
    kk = 1.0 - f
    g = jnp.log2(f)
    hq = proj(_OFF_HQ, HW)
    hv = proj(_OFF_HI, HW)

    row = lax.broadcasted_iota(jnp.int32, (T, 1), 0)
    C = HGRN_CHUNK

    def rdown(t, d):
        return pltpu.roll(t.reshape(T // SUBLANES, SUBLANES, HW), d, axis=1).reshape(T, HW)

    def rup(t, d):
        return pltpu.roll(t.reshape(T // SUBLANES, SUBLANES, HW), SUBLANES - d, axis=1).reshape(T, HW)

    def head(t, h):
        return t[:, HGRN_DK * h:HGRN_DK * (h + 1)]

    qk0 = hq * kk
    o_parts = [_rowsum(head(qk0, h)) * head(hv, h) for h in range(HGRN_HEADS)]

    small_levels = []
    bsz = 1
    lb_cum, lb_tot = g, g
    while bsz < SUBLANES:
        odd = (row % (2 * bsz)) >= bsz
        ex = jnp.exp2(jnp.where(odd, lb_cum, lb_tot - lb_cum))
        small_levels.append(((hq * ex).astype(_BF), (kk * ex).astype(_BF)))
        prev_tot = rdown(lb_tot, bsz)
        lb_cum = lb_cum + jnp.where(odd, prev_tot, 0.0)
        lb_tot = lb_tot + jnp.where(odd, prev_tot, rup(lb_tot, bsz))
        bsz *= 2

    cum = [lb_cum[r:r + bsz] for r in range(0, T, bsz)]
    tot = [lb_tot[r:r + 1] for r in range(0, T, bsz)]
    kk_b = kk.astype(_BF)
    big_levels = []
    while bsz < C:
        q_rows, k_rows = [], []
        for j, r in enumerate(range(0, T, bsz)):
            if j % 2:
                q_rows.append(hq[r:r + bsz] * jnp.exp2(cum[j]))
                k_rows.append(kk_b[r:r + bsz])
            else:
                k_rows.append((kk[r:r + bsz] * jnp.exp2(tot[j] - cum[j])).astype(_BF))
        big_levels.append((bsz, jnp.concatenate(q_rows, axis=0).astype(_BF), jnp.concatenate(k_rows, axis=0)))
        cum = [jnp.concatenate([cum[j], cum[j + 1] + tot[j]], axis=0) for j in range(0, len(cum), 2)]
        tot = [tot[j] + tot[j + 1] for j in range(0, len(tot), 2)]
        bsz *= 2
    hv_b = hv.astype(_BF)
    lvl = lvl_ref[...]

    for c in range(T // C):
        rs_ = slice(c * C, (c + 1) * C)
        rh_ = slice(c * C // 2, (c + 1) * C // 2)
        q_in = (hq[rs_] * jnp.exp2(cum[c])).astype(_BF)
        k_st = (kk[rs_] * jnp.exp2(tot[c] - cum[c])).astype(_BF)
        dec = jnp.exp2(tot[c])
        for h in range(HGRN_HEADS):
            cs_ = slice(HGRN_DK * h, HGRN_DK * (h + 1))
            st = st_ref[h]
            vc = hv_b[rs_, cs_]
            inter = _dot_nt(q_in[:, cs_], st.astype(_BF))
            zero_tile = jnp.zeros((SUBLANES, LANES), _F32)
            tiles = [[zero_tile for _ in range(C // LANES)] for _ in range(C // SUBLANES)]

            def place(li, p, p_row, rb, col):
                r, ct = rb * SUBLANES, col // LANES
                cs = slice(ct * LANES, (ct + 1) * LANES)
                tiles[rb][ct] = jnp.where(lvl[r:r + SUBLANES, cs] == li, p[p_row:p_row + SUBLANES, cs], tiles[rb][ct])

            for li, (ql, kl) in enumerate(small_levels):
                p = _dot_nt(ql[rs_, cs_], kl[rs_, cs_])
                for rb in range(C // SUBLANES):
                    place(li, p, rb * SUBLANES, rb, rb * SUBLANES)
            for li, (bs, ql, kl) in enumerate(big_levels, start=len(small_levels)):
                p = _dot_nt(ql[rh_, cs_], kl[rs_, cs_])
                for rb in range(C // SUBLANES):
                    blk = (rb * SUBLANES) // bs
                    if blk % 2:
                        place(li, p, rb * SUBLANES - (blk + 1) // 2 * bs, rb, (blk - 1) * bs)
            amat = jnp.concatenate([jnp.concatenate(row_tiles, axis=1) for row_tiles in tiles], axis=0)
            intra = _dot(amat.astype(_BF), vc)
            st_ref[h] = st * dec[:, cs_] + _dot_tn(vc, k_st[:, cs_])
            o_ref[rs_, cs_] = inter + intra + o_parts[h][rs_]

    o = o_ref[...]
    hg = proj(_OFF_HG, HW)
    yh_parts = []
    for h in range(HGRN_HEADS):
        oh = head(o, h)
        ms = _rowsum(oh * oh) * (1.0 / HGRN_DK)
        yh_parts.append(oh * lax.rsqrt(ms + NORM_EPS))
    yh = jnp.concatenate(yh_parts, axis=1) * hgn_ref[...]
    yh = (yh * (hg * jax.nn.sigmoid(hg))).astype(_BF)

    up_h = _dot(yh, w_uh_ref[...])
    half = len(gate_parts) // 2
    z_ga = jnp.concatenate(gate_parts[:half], axis=1)
    z_gh = jnp.concatenate(gate_parts[half:], axis=1)
    merged = jax.nn.sigmoid(z_ga) * up_a + jax.nn.sigmoid(z_gh) * up_h
    h1 = x + _dot(merged.astype(_BF), w_out_ref[...])
    h1_ref[0] = h1

    xn2 = _rms(h1, gffn_ref[...])
    xp_ref[0] = _pack_bf16_pairs(xn2)
    x_hi = xn2.astype(_BF)
    x_lo = (xn2 - x_hi.astype(_F32)).astype(_BF)
    w_rt = w_rt_ref[...]
    w_hi = w_rt.astype(_BF)
    w_lo = (w_rt - w_hi.astype(_F32)).astype(_BF)
    logits = (_dot_nt(w_hi, x_hi) + (_dot_nt(w_hi, x_lo) + _dot_nt(w_lo, x_hi))) + b_r_ref[...]
    eidx = lax.broadcasted_iota(jnp.int32, (N_EXPERTS, T), 0)
    vals, sels, ohs = [], [], []
    l = logits
    for _ in range(TOP_K):
        m = jnp.max(l, axis=0, keepdims=True)
        sel = jnp.min(jnp.where(l == m, eidx, N_EXPERTS), axis=0, keepdims=True)
        oh = eidx == sel
        vals.append(m)
        sels.append(sel)
        ohs.append(oh)
        l = jnp.where(oh, neg_inf, l)
    es = [jnp.exp(v - vals[0]) for v in vals]
    den = es[0] + es[1] + es[2] + es[3]
    chosen = jnp.zeros((N_EXPERTS, T), _F32)
    for oh in ohs:
        chosen = chosen + jnp.where(oh, 1.0, 0.0)
    ui = lax.broadcasted_iota(jnp.int32, (T, T), 0)
    uj = lax.broadcasted_iota(jnp.int32, (T, T), 1)
    upper = jnp.where(ui < uj, 1.0, 0.0).astype(_BF)
    before = _dot(chosen.astype(_BF), upper) + cnt_sc[:, 0:1]
    for k in range(TOP_K):
        idx_ref[k:k + 1, :] = sels[k]
        gate_ref[k:k + 1, :] = es[k] / den
        rank_ref[k:k + 1, :] = jnp.sum(jnp.where(ohs[k], before, 0.0), axis=0, keepdims=True).astype(jnp.int32)
    cnt_sc[...] = cnt_sc[...] + _rowsum(chosen)
    cnt_ref[...] = cnt_sc[...]


def _const_spec(shape):
    return pl.BlockSpec(shape, lambda b, s: (0,) * len(shape), pipeline_mode=pl.Buffered(1))


def _mixer_call(batch0, B, x, sinks, cos_t, sin_t, gmix, w_in, b_in, lb, hgn, lvl, w_ua, w_uh, w_out, gffn, w_rt, b_r):
    S = x.shape[1]
    T = SEQ_TILE
    N = B * S
    nS = S // T
    tok_spec = lambda w: pl.BlockSpec((4, T), lambda b, s: (0, b * nS + s))
    out_shape = (
        jax.ShapeDtypeStruct((B, S, D_MODEL), _F32),
        jax.ShapeDtypeStruct((B, S, D_MODEL // 2), jnp.uint32),
        jax.ShapeDtypeStruct((TOP_K, N), jnp.int32),
        jax.ShapeDtypeStruct((TOP_K, N), jnp.int32),
        jax.ShapeDtypeStruct((TOP_K, N), _F32),
        jax.ShapeDtypeStruct((N_EXPERTS, LANES), _F32),
    )
    in_specs = [
        pl.BlockSpec(memory_space=pltpu.SMEM),
        pl.BlockSpec((1, T, D_MODEL), lambda b, s: (batch0 + b, s, 0)),
        pl.BlockSpec((T, LANES), lambda b, s: (s, 0)),
        pl.BlockSpec((T, LANES), lambda b, s: (s, 0)),
        _const_spec((1, D_MODEL)),
        _const_spec((D_MODEL, IN_COLS)),
        _const_spec((1, IN_COLS)),
        _const_spec((2, HGRN_WIDTH)),
        _const_spec((1, HGRN_WIDTH)),
        _const_spec((HGRN_CHUNK, HGRN_CHUNK)),
        _const_spec((ATTN_WIDTH, D_MODEL)),
        _const_spec((HGRN_WIDTH, D_MODEL)),
        _const_spec((D_MODEL, D_MODEL)),
        _const_spec((1, D_MODEL)),
        _const_spec((N_EXPERTS, D_MODEL)),
        _const_spec((N_EXPERTS, 1)),
    ]
    out_specs = (
        pl.BlockSpec((1, T, D_MODEL), lambda b, s: (b, s, 0)),
        pl.BlockSpec((1, T, D_MODEL // 2), lambda b, s: (b, s, 0)),
        tok_spec(0), tok_spec(0), tok_spec(0),
        pl.BlockSpec((N_EXPERTS, LANES), lambda b, s: (0, 0)),
    )
    scratch = [
        pltpu.VMEM((WINDOW, KV_WIDTH), _F32),
        pltpu.VMEM((WINDOW, KV_WIDTH), _F32),
        pltpu.VMEM((HGRN_HEADS, HGRN_DK, HGRN_DK), _F32),
        pltpu.VMEM((N_EXPERTS, LANES), _F32),
        pltpu.VMEM((T, ATTN_WIDTH), _F32),
        pltpu.VMEM((T, HGRN_WIDTH), _F32),
    ]
    return pl.pallas_call(
        _mixer_kernel,
        grid=(B, nS),
        in_specs=in_specs,
        out_specs=out_specs,
        out_shape=out_shape,
        scratch_shapes=scratch,
        compiler_params=pltpu.CompilerParams(
            dimension_semantics=("arbitrary", "arbitrary"), vmem_limit_bytes=VMEM_LIMIT),
        name="mixer",
    )(sinks, x, cos_t, sin_t, gmix, w_in, b_in, lb, hgn, lvl, w_ua, w_uh, w_out, gffn, w_rt, b_r)


def _dest_kernel(start_ref, idx_ref, rank_ref, dest_ref):
    idx = idx_ref[...]
    dest = rank_ref[...]
    for e in range(N_EXPERTS):
        dest = dest + jnp.where(idx == e, start_ref[e], 0)
    dest_ref[...] = dest


def _dest_call(pad_starts, idx, rank):
    K, N = idx.shape
    spec = pl.BlockSpec((K, DEST_TILE), lambda i: (0, i))
    return pl.pallas_call(
        _dest_kernel,
        grid=(N // DEST_TILE,),
        in_specs=[pl.BlockSpec(memory_space=pltpu.SMEM), spec, spec],
        out_specs=spec,
        out_shape=jax.ShapeDtypeStruct((K, N), jnp.int32),
        compiler_params=pltpu.CompilerParams(dimension_semantics=("arbitrary",)),
        name="dest",
    )(pad_starts, idx, rank)


def _sc_mesh():
    return plsc.VectorSubcoreMesh(core_axis_name="c", subcore_axis_name="s")


def _sc_worker_id():
    return lax.axis_index("s") * SC_CORES + lax.axis_index("c")


def _sc_scatter_rows(x, dest, n_rows):
    N, C = x.shape
    K = dest.shape[0]
    n_chunks = N // SC_CHUNK
    per_worker = n_chunks // SC_WORKERS
    dest3 = dest.reshape(K, n_chunks, SC_CHUNK)

    @functools.partial(
        pl.kernel, out_type=jax.ShapeDtypeStruct((n_rows, C), x.dtype), mesh=_sc_mesh(),
        scratch_types=[pltpu.VMEM((K, SC_CHUNK), jnp.int32), pltpu.VMEM((SC_CHUNK, C), x.dtype),
                       pltpu.SemaphoreType.DMA])
    def scatter(x_hbm, i_hbm, o_hbm, idx_v, rows_v, sem):
        base = _sc_worker_id() * per_worker

        @pl.loop(0, per_worker)
        def _(j):
            c = base + j
            off = pl.multiple_of(c * SC_CHUNK, SC_CHUNK)
            for k in range(K):
                pltpu.sync_copy(i_hbm.at[k, c], idx_v.at[k])
            pltpu.sync_copy(x_hbm.at[pl.ds(off, SC_CHUNK)], rows_v)
            for k in range(K):
                pltpu.async_copy(rows_v, o_hbm.at[idx_v.at[k]], sem).wait()

    return scatter(x, dest3)


def _sc_gather_rows(table, idx):
    M = idx.shape[0]
    C = table.shape[1]
    per_worker = M // SC_WORKERS
    n_chunks = per_worker // SC_CHUNK

    @functools.partial(
        pl.kernel, out_type=jax.ShapeDtypeStruct((M, C), table.dtype), mesh=_sc_mesh(),
        scratch_types=[pltpu.VMEM((SC_CHUNK,), jnp.int32), pltpu.VMEM((SC_CHUNK, C), table.dtype),
                       pltpu.SemaphoreType.DMA])
    def gather(t_hbm, i_hbm, o_hbm, idx_v, rows_v, sem):
        base = _sc_worker_id() * per_worker

        @pl.loop(0, n_chunks)
        def _(j):
            off = pl.multiple_of(base + j * SC_CHUNK, SC_CHUNK)
            pltpu.sync_copy(i_hbm.at[pl.ds(off, SC_CHUNK)], idx_v)
            pltpu.async_copy(t_hbm.at[idx_v], rows_v, sem).wait()
            pltpu.sync_copy(rows_v, o_hbm.at[pl.ds(off, SC_CHUNK)])

    return gather(table, idx)


def _expert_kernel(blk_e_ref, nb_ref, xs_ref, w1_ref, b1_ref, w2_ref, b2_ref, ys_ref, w1b_ref, w2b_ref):
    i = pl.program_id(0)
    active = i < nb_ref[0]
    new_expert = jnp.logical_or(i == 0, blk_e_ref[i] != blk_e_ref[jnp.maximum(i - 1, 0)])

    @pl.when(jnp.logical_and(active, new_expert))
    def _():
        for r in range(0, D_MODEL, WEIGHT_CAST_ROWS):
            w1b_ref[r:r + WEIGHT_CAST_ROWS, :] = w1_ref[0, r:r + WEIGHT_CAST_ROWS, :].astype(_BF)
        for r in range(0, D_EXPERT, WEIGHT_CAST_ROWS):
            w2b_ref[r:r + WEIGHT_CAST_ROWS, :] = w2_ref[0, r:r + WEIGHT_CAST_ROWS, :].astype(_BF)

    @pl.when(active)
    def _():
        xb = _unpack_bf16_pairs(xs_ref[...]).astype(_BF)
        h = _dot(xb, w1b_ref[...]) + b1_ref[0]
        glu = jnp.minimum(h[:, :D_EXPERT], SWIGLU_LIMIT)
        lin = jnp.clip(h[:, D_EXPERT:], -SWIGLU_LIMIT, SWIGLU_LIMIT)
        act = glu * jax.nn.sigmoid(SWIGLU_ALPHA * glu) * (lin + 1.0)
        y = _dot(act.astype(_BF), w2b_ref[...]) + b2_ref[0]
        ys_ref[...] = _pack_bf16_pairs(y)

    @pl.when(jnp.logical_not(active))
    def _():
        ys_ref[...] = jnp.zeros_like(ys_ref)


def _expert_call(blk_e, nb_used, xs, w1, b1, w2, b2):
    n_blocks = xs.shape[0] // MOE_BLOCK
    M = MOE_BLOCK
    grid_spec = pltpu.PrefetchScalarGridSpec(
        num_scalar_prefetch=2,
        grid=(n_blocks,),
        in_specs=[
            pl.BlockSpec((M, D_MODEL // 2), lambda i, be, nb: (jnp.minimum(i, nb[0] - 1), 0)),
            pl.BlockSpec((1, D_MODEL, 2 * D_EXPERT), lambda i, be, nb: (be[i], 0, 0)),
            pl.BlockSpec((1, 1, 2 * D_EXPERT), lambda i, be, nb: (be[i], 0, 0)),
            pl.BlockSpec((1, D_EXPERT, D_MODEL), lambda i, be, nb: (be[i], 0, 0)),
            pl.BlockSpec((1, 1, D_MODEL), lambda i, be, nb: (be[i], 0, 0)),
        ],
        out_specs=pl.BlockSpec((M, D_MODEL // 2), lambda i, be, nb: (i, 0)),
        scratch_shapes=[pltpu.VMEM((D_MODEL, 2 * D_EXPERT), _BF), pltpu.VMEM((D_EXPERT, D_MODEL), _BF)],
    )
    return pl.pallas_call(
        _expert_kernel,
        grid_spec=grid_spec,
        out_shape=jax.ShapeDtypeStruct(xs.shape, jnp.uint32),
        compiler_params=pltpu.CompilerParams(
            dimension_semantics=("arbitrary",), vmem_limit_bytes=VMEM_LIMIT),
        name="experts",
    )(blk_e, nb_used, xs, w1, b1, w2, b2)


def _combine_kernel(h1_ref, gate_ref, gfin_ref, y0_ref, y1_ref, y2_ref, y3_ref, out_ref):
    acc = h1_ref[...]
    gates = gate_ref[...]
    for k, y_ref in enumerate((y0_ref, y1_ref, y2_ref, y3_ref)):
        acc = acc + gates[:, k:k + 1] * _unpack_bf16_pairs(y_ref[...])
    out_ref[...] = _rms(acc, gfin_ref[...])


def _combine_call(h1, gates_t, gfin, yg, out_prev, tile0, n_total):
    N = h1.shape[0]
    T = COMBINE_TILE
    nT = N // T
    y_specs = [pl.BlockSpec((T, D_MODEL // 2), functools.partial(lambda i, k: (k * nT + i, 0), k=k))
               for k in range(TOP_K)]
    in_specs = [
        pl.BlockSpec((T, D_MODEL), lambda i: (i, 0)),
        pl.BlockSpec((T, TOP_K), lambda i: (i, 0)),
        pl.BlockSpec((1, D_MODEL), lambda i: (0, 0)),
    ] + y_specs
    args = [h1, gates_t, gfin, yg, yg, yg, yg]
    body, aliases = _combine_kernel, {}
    if out_prev is not None:
        in_specs.append(pl.BlockSpec(memory_space=pl.ANY))
        args.append(out_prev)
        aliases = {len(args) - 1: 0}
        body = lambda *refs: _combine_kernel(*refs[:7], refs[8])
    return pl.pallas_call(
        body,
        grid=(nT,),
        in_specs=in_specs,
        out_specs=pl.BlockSpec((T, D_MODEL), lambda i: (tile0 + i, 0)),
        out_shape=jax.ShapeDtypeStruct((n_total, D_MODEL), _F32),
        input_output_aliases=aliases,
        compiler_params=pltpu.CompilerParams(dimension_semantics=("arbitrary",)),
        name="combine",
    )(*args)


def _qk_column_permutation():
    half = HEAD_DIM // 2
    r = np.arange(half)

    def pair(base, ha, hb):
        return np.concatenate([base + ha * HEAD_DIM + r, base + hb * HEAD_DIM + r,
                               base + ha * HEAD_DIM + half + r, base + hb * HEAD_DIM + half + r])

    q = np.concatenate([pair(_OFF_Q, 2 * p, 2 * p + 1) for p in range(N_Q_HEADS // 2)])
    k = pair(_OFF_K, 0, 1)
    return np.concatenate([q, k, np.arange(_OFF_V, IN_COLS)])


def _hgrn_level_table():
    t = np.arange(HGRN_CHUNK)[:, None]
    u = np.arange(HGRN_CHUNK)[None, :]
    top = np.floor(np.log2(np.maximum(t ^ u, 1))).astype(np.int32)
    return np.where(t > u, top, -1).astype(np.int32)


def _rope_tables(S):
    half = HEAD_DIM // 2
    inv_freq = ROPE_THETA ** (-(jnp.arange(half, dtype=_F32) * 2.0 / HEAD_DIM))
    ang = jnp.arange(S, dtype=_F32)[:, None] * inv_freq[None, :]
    cos = jnp.tile(jnp.cos(ang), (1, 4))
    sin = jnp.tile(jnp.sin(ang), (1, 4))
    sign = jnp.where(jnp.arange(LANES) < 64, -1.0, 1.0).astype(_F32)
    return cos, sin * sign[None, :]


def kernel(x, norm_mix_g, w_in, b_in, attn_sinks, hgrn_lb, hgrn_norm_g, w_up_attn, w_up_hgrn, w_out,
           norm_ffn_g, w_router, b_router, w_moe1, b_moe1, w_moe2, b_moe2, norm_final_g):
    B, S, D = x.shape
    N = B * S
    assert D == D_MODEL and S % SEQ_TILE == 0 and SEQ_TILE % WINDOW == 0
    assert B % BATCH_PARTS == 0 and (N // BATCH_PARTS) % (SC_WORKERS * SC_CHUNK) == 0
    assert (N // BATCH_PARTS) % COMBINE_TILE == 0 and (N // BATCH_PARTS) % DEST_TILE == 0
    assert norm_mix_g.shape[0] == 1

    perm = _qk_column_permutation()
    cos_t, sin_t = _rope_tables(S)
    weights = (attn_sinks[0], cos_t, sin_t, norm_mix_g,
               w_in[0][:, perm].astype(_BF), b_in[:, perm], hgrn_lb, hgrn_norm_g, jnp.asarray(_hgrn_level_table()),
               w_up_attn[0].astype(_BF), w_up_hgrn[0].astype(_BF), w_out[0].astype(_BF),
               norm_ffn_g, w_router[0].T, b_router[0][:, None])

    Bp = B // BATCH_PARTS
    Np = Bp * S
    n_blocks = (Np * TOP_K) // MOE_BLOCK + N_EXPERTS
    out = None
    for part in range(BATCH_PARTS):
        h1, xp, idx, rank, gates, cnt = _mixer_call(part * Bp, Bp, x, *weights)

        counts = cnt[:, 0].astype(jnp.int32)
        padded = ((counts + MOE_BLOCK - 1) // MOE_BLOCK) * MOE_BLOCK
        pad_ends = jnp.cumsum(padded)
        pad_starts = pad_ends - padded
        dest = _dest_call(pad_starts, idx, rank)
        blk_start = jnp.arange(n_blocks, dtype=jnp.int32) * MOE_BLOCK
        blk_e = jnp.minimum(jnp.sum((pad_ends[None, :] <= blk_start[:, None]).astype(jnp.int32), axis=1),
                            N_EXPERTS - 1)
        nb_used = (pad_ends[-1:] // MOE_BLOCK).astype(jnp.int32)

        xs = _sc_scatter_rows(xp.reshape(Np, D // 2), dest, n_blocks * MOE_BLOCK)
        ys = _expert_call(blk_e, nb_used, xs, w_moe1[0], b_moe1[0][:, None, :], w_moe2[0], b_moe2[0][:, None, :])
        yg = _sc_gather_rows(ys, dest.reshape(TOP_K * Np))
        out = _combine_call(h1.reshape(Np, D), gates.T, norm_final_g[None, :], yg, out,
                            part * (Np // COMBINE_TILE), N)
    return out.reshape(B, S, D)
```

```python
import functools

import numpy as np
import jax
import jax.numpy as jnp
from jax import lax
from jax.experimental import pallas as pl
from jax.experimental.pallas import tpu as pltpu
from jax.experimental.pallas import tpu_sc as plsc

D_MODEL = 1024
HEAD_DIM = 64
ROT_HALF = HEAD_DIM // 2
N_Q_HEADS = 8
N_KV_HEADS = 2
GROUP = N_Q_HEADS // N_KV_HEADS
ATTN_WIDTH = N_Q_HEADS * HEAD_DIM
KV_WIDTH = N_KV_HEADS * HEAD_DIM
WINDOW = 128
ROPE_THETA = 10000.0
HGRN_HEADS = 4
HGRN_DK = 128
HGRN_WIDTH = HGRN_HEADS * HGRN_DK
N_EXPERTS = 32
TOP_K = 4
D_EXPERT = 1024
SWIGLU_ALPHA = 1.702
SWIGLU_LIMIT = 7.0
MOE_BLOCK = 512
NORM_EPS = 1e-5

_OFF_Q = 0
_OFF_K = _OFF_Q + ATTN_WIDTH
_OFF_V = _OFF_K + KV_WIDTH
_OFF_HQ = _OFF_V + KV_WIDTH
_OFF_HF = _OFF_HQ + HGRN_WIDTH
_OFF_HI = _OFF_HF + HGRN_WIDTH
_OFF_HG = _OFF_HI + HGRN_WIDTH
_OFF_GA = _OFF_HG + HGRN_WIDTH
_OFF_GH = _OFF_GA + D_MODEL
IN_COLS = _OFF_GH + D_MODEL

LANES = 128
SUBLANES = 8
SEQ_TILE = 512
HGRN_CHUNK = 256
GATE_JOB_COLS = 256
COMBINE_TILE = 512
DEST_TILE = 8192
BATCH_PARTS = 2
WEIGHT_CAST_ROWS = 128
SC_CORES = 2
SC_WORKERS = 32
SC_CHUNK = 128
VMEM_LIMIT = 56 * 1024 * 1024

_BF = jnp.bfloat16
_F32 = jnp.float32


def _dot(a, b):
    return jnp.dot(a, b, preferred_element_type=_F32)


def _dot_nt(a, b):
    return lax.dot_general(a, b, (((1,), (1,)), ((), ())), preferred_element_type=_F32)


def _dot_tn(a, b):
    return lax.dot_general(a, b, (((0,), (0,)), ((), ())), preferred_element_type=_F32)


def _rowsum(x):
    return jnp.sum(x, axis=1, keepdims=True)


def _rms(x, g):
    ms = _rowsum(x * x) * (1.0 / x.shape[1])
    return x * lax.rsqrt(ms + NORM_EPS) * g


def _pack_bf16_pairs(x):
    n = x.shape[1] // 2
    lo = lax.bitcast_convert_type(x[:, :n].astype(_BF).astype(_F32), jnp.uint32)
    hi = lax.bitcast_convert_type(x[:, n:].astype(_BF).astype(_F32), jnp.uint32)
    return (lo >> 16) | (hi & jnp.uint32(0xFFFF0000))


def _unpack_bf16_pairs(u):
    lo = lax.bitcast_convert_type(u << 16, _F32)
    hi = lax.bitcast_convert_type(u & jnp.uint32(0xFFFF0000), _F32)
    return jnp.concatenate([lo, hi], axis=1)


def _mixer_kernel(sinks_ref, x_ref, cos_ref, sin_ref, gmix_ref, w_in_ref, b_in_ref, lb_ref,
                  hgn_ref, lvl_ref, w_ua_ref, w_uh_ref, w_out_ref, gffn_ref, w_rt_ref, b_r_ref,
                  h1_ref, xp_ref, idx_ref, rank_ref, gate_ref, cnt_ref,
                  kc_ref, vc_ref, st_ref, cnt_sc, ya_ref, o_ref):
    T = SEQ_TILE
    b = pl.program_id(0)
    s = pl.program_id(1)

    @pl.when(jnp.logical_and(b == 0, s == 0))
    def _():
        cnt_sc[...] = jnp.zeros_like(cnt_sc)

    @pl.when(s == 0)
    def _():
        kc_ref[...] = jnp.zeros_like(kc_ref)
        vc_ref[...] = jnp.zeros_like(vc_ref)
        st_ref[...] = jnp.zeros_like(st_ref)

    x = x_ref[0]
    xn = _rms(x, gmix_ref[...]).astype(_BF)

    def proj(off, width):
        return _dot(xn, w_in_ref[:, off:off + width]) + b_in_ref[:, off:off + width]

    cos = cos_ref[...]
    sin = sin_ref[...]

    def rope(t):
        return t * cos + pltpu.roll(t, HEAD_DIM, axis=1) * sin

    zq = proj(_OFF_Q, ATTN_WIDTH)
    scale = HEAD_DIM ** -0.5
    q_pairs = [(rope(zq[:, LANES * p:LANES * (p + 1)]) * scale).astype(_BF) for p in range(N_Q_HEADS // 2)]
    k_rot = rope(proj(_OFF_K, KV_WIDTH))
    v_new = proj(_OFF_V, KV_WIDTH)

    k_ext = jnp.concatenate([kc_ref[...], k_rot], axis=0)
    v_ext = jnp.concatenate([vc_ref[...], v_new], axis=0)
    kc_ref[...] = k_rot[T - WINDOW:, :]
    vc_ref[...] = v_new[T - WINDOW:, :]

    lane = lax.broadcasted_iota(jnp.int32, (1, LANES), 1)
    slot_a = (lane % HEAD_DIM) < ROT_HALF
    lane_lo = lane < HEAD_DIM
    k_r32 = pltpu.roll(k_ext, ROT_HALF, axis=1)
    k_r96 = pltpu.roll(k_ext, LANES - ROT_HALF, axis=1)
    v_r64 = pltpu.roll(v_ext, HEAD_DIM, axis=1)
    zero = jnp.zeros_like(k_ext)
    k_var = [(jnp.where(slot_a, k_ext, zero).astype(_BF), jnp.where(slot_a, zero, k_r32).astype(_BF)),
             (jnp.where(slot_a, k_r96, zero).astype(_BF), jnp.where(slot_a, zero, k_ext).astype(_BF))]
    v_var = [(jnp.where(lane_lo, v_ext, zero).astype(_BF), jnp.where(lane_lo, zero, v_r64).astype(_BF)),
             (jnp.where(lane_lo, v_r64, zero).astype(_BF), jnp.where(lane_lo, zero, v_ext).astype(_BF))]

    qi = lax.broadcasted_iota(jnp.int32, (2 * WINDOW, 2 * WINDOW), 0) % WINDOW
    kj = lax.broadcasted_iota(jnp.int32, (2 * WINDOW, 2 * WINDOW), 1)
    band = jnp.logical_and(kj > qi, kj <= qi + WINDOW)
    row_top = lax.broadcasted_iota(jnp.int32, (2 * WINDOW, 1), 0) < WINDOW
    neg_inf = jnp.float32(-jnp.inf)

    gate_cols = [(_OFF_GA + c, GATE_JOB_COLS) for c in range(0, 2 * D_MODEL, GATE_JOB_COLS)]
    gate_parts = []
    jobs_per_unit = -(-len(gate_cols) // ((T // WINDOW) * N_KV_HEADS))

    for n in range(T // WINDOW):
        if n == 0:
            ok = jnp.logical_and(band, jnp.logical_or(kj >= WINDOW, s > 0))
        else:
            ok = band
        r0 = n * WINDOW
        for j in range(N_KV_HEADS):
            q2 = jnp.concatenate([q_pairs[2 * j][r0:r0 + WINDOW], q_pairs[2 * j + 1][r0:r0 + WINDOW]], axis=0)
            kcat = jnp.concatenate([k_var[j][0][r0:r0 + 2 * WINDOW], k_var[j][1][r0:r0 + 2 * WINDOW]], axis=0)
            vcat = jnp.concatenate([v_var[j][0][r0:r0 + 2 * WINDOW], v_var[j][1][r0:r0 + 2 * WINDOW]], axis=0)
            sc = _dot_nt(q2, kcat)
            ps, rs = [], []
            for half in range(2):
                sh = jnp.where(ok, sc[:, 2 * WINDOW * half:2 * WINDOW * (half + 1)], neg_inf)
                snk = jnp.where(row_top, sinks_ref[GROUP * j + half], sinks_ref[GROUP * j + 2 + half])
                m = jnp.maximum(jnp.max(sh, axis=1, keepdims=True), snk)
                p = jnp.exp(sh - m)
                den = _rowsum(p) + jnp.exp(snk - m)
                ps.append(p.astype(_BF))
                rs.append(1.0 / den)
            o = _dot(jnp.concatenate(ps, axis=1), vcat)
            o = o * jnp.where(lane_lo, rs[0], rs[1])
            ya_ref[r0:r0 + WINDOW, LANES * (2 * j):LANES * (2 * j + 1)] = o[:WINDOW]
            ya_ref[r0:r0 + WINDOW, LANES * (2 * j + 1):LANES * (2 * j + 2)] = o[WINDOW:]
            for _ in range(min(jobs_per_unit, len(gate_cols) - len(gate_parts))):
                gate_parts.append(proj(*gate_cols[len(gate_parts)]))

    up_a = _dot(ya_ref[...].astype(_BF), w_ua_ref[...])

    HW = HGRN_WIDTH
    a0 = lb_ref[0:1, :]
    a1 = lb_ref[1:2, :]
    am = jnp.maximum(a0, a1)
    e0 = jnp.exp(a0 - am)
    lb = e0 / (e0 + jnp.exp(a1 - am))
    f = lb + (1.0 - lb) * jax.nn.sigmoid(proj(_OFF_HF, HW))
    kk = 1.0 - f
    g = jnp.log2(f)
    hq = proj(_OFF_HQ, HW)
    hv = proj(_OFF_HI, HW)

    row = lax.broadcasted_iota(jnp.int32, (T, 1), 0)
    C = HGRN_CHUNK

    def rdown(t, d):
        return pltpu.roll(t.reshape(T // SUBLANES, SUBLANES, HW), d, axis=1).reshape(T, HW)

    def rup(t, d):
        return pltpu.roll(t.reshape(T // SUBLANES, SUBLANES, HW), SUBLANES - d, axis=1).reshape(T, HW)

    def head(t, h):
        return t[:, HGRN_DK * h:HGRN_DK * (h + 1)]

    qk0 = hq * kk
    o_parts = [_rowsum(head(qk0, h)) * head(hv, h) for h in range(HGRN_HEADS)]

    small_levels = []
    bsz = 1
    lb_cum, lb_tot = g, g
    while bsz < SUBLANES:
        odd = (row % (2 * bsz)) >= bsz
        ex = jnp.exp2(jnp.where(odd, lb_cum, lb_tot - lb_cum))
        small_levels.append(((hq * ex).astype(_BF), (kk * ex).astype(_BF)))
        prev_tot = rdown(lb_tot, bsz)
        lb_cum = lb_cum + jnp.where(odd, prev_tot, 0.0)
        lb_tot = lb_tot + jnp.where(odd, prev_tot, rup(lb_tot, bsz))
        bsz *= 2

    cum = [lb_cum[r:r + bsz] for r in range(0, T, bsz)]
    tot = [lb_tot[r:r + 1] for r in range(0, T, bsz)]
    kk_b = kk.astype(_BF)
    big_levels = []
    while bsz < C:
        q_rows, k_rows = [], []
        for j, r in enumerate(range(0, T, bsz)):
            if j % 2:
                q_rows.append(hq[r:r + bsz] * jnp.exp2(cum[j]))
                k_rows.append(kk_b[r:r + bsz])
            else:
                k_rows.append((kk[r:r + bsz] * jnp.exp2(tot[j] - cum[j])).astype(_BF))
        big_levels.append((bsz, jnp.concatenate(q_rows, axis=0).astype(_BF), jnp.concatenate(k_rows, axis=0)))
        cum = [jnp.concatenate([cum[j], cum[j + 1] + tot[j]], axis=0) for j in range(0, len(cum), 2)]
        tot = [tot[j] + tot[j + 1] for j in range(0, len(tot), 2)]
        bsz *= 2
    hv_b = hv.astype(_BF)
    lvl = lvl_ref[...]

    for c in range(T // C):
        rs_ = slice(c * C, (c + 1) * C)
        rh_ = slice(c * C // 2, (c + 1) * C // 2)
        q_in = (hq[rs_] * jnp.exp2(cum[c])).astype(_BF)
        k_st = (kk[rs_] * jnp.exp2(tot[c] - cum[c])).astype(_BF)
        dec = jnp.exp2(tot[c])
        for h in range(HGRN_HEADS):
            cs_ = slice(HGRN_DK * h, HGRN_DK * (h + 1))
            st = st_ref[h]
            vc = hv_b[rs_, cs_]
            inter = _dot_nt(q_in[:, cs_], st.astype(_BF))
            zero_tile = jnp.zeros((SUBLANES, LANES), _F32)
            tiles = [[zero_tile for _ in range(C // LANES)] for _ in range(C // SUBLANES)]

            def place(li, p, p_row, rb, col):
                r, ct = rb * SUBLANES, col // LANES
                cs = slice(ct * LANES, (ct + 1) * LANES)
                tiles[rb][ct] = jnp.where(lvl[r:r + SUBLANES, cs] == li, p[p_row:p_row + SUBLANES, cs], tiles[rb][ct])

            for li, (ql, kl) in enumerate(small_levels):
                p = _dot_nt(ql[rs_, cs_], kl[rs_, cs_])
                for rb in range(C // SUBLANES):
                    place(li, p, rb * SUBLANES, rb, rb * SUBLANES)
            for li, (bs, ql, kl) in enumerate(big_levels, start=len(small_levels)):
                p = _dot_nt(ql[rh_, cs_], kl[rs_, cs_])
                for rb in range(C // SUBLANES):
                    blk = (rb * SUBLANES) // bs
                    if blk % 2:
                        place(li, p, rb * SUBLANES - (blk + 1) // 2 * bs, rb, (blk - 1) * bs)
            amat = jnp.concatenate([jnp.concatenate(row_tiles, axis=1) for row_tiles in tiles], axis=0)
            intra = _dot(amat.astype(_BF), vc)
            st_ref[h] = st * dec[:, cs_] + _dot_tn(vc, k_st[:, cs_])
            o_ref[rs_, cs_] = inter + intra + o_parts[h][rs_]

    o = o_ref[...]
    hg = proj(_OFF_HG, HW)
    yh_parts = []
    for h in range(HGRN_HEADS):
        oh = head(o, h)
        ms = _rowsum(oh * oh) * (1.0 / HGRN_DK)
        yh_parts.append(oh * lax.rsqrt(ms + NORM_EPS))
    yh = jnp.concatenate(yh_parts, axis=1) * hgn_ref[...]
    yh = (yh * (hg * jax.nn.sigmoid(hg))).astype(_BF)

    up_h = _dot(yh, w_uh_ref[...])
    half = len(gate_parts) // 2
    z_ga = jnp.concatenate(gate_parts[:half], axis=1)
    z_gh = jnp.concatenate(gate_parts[half:], axis=1)
    merged = jax.nn.sigmoid(z_ga) * up_a + jax.nn.sigmoid(z_gh) * up_h
    h1 = x + _dot(merged.astype(_BF), w_out_ref[...])
    h1_ref[0] = h1

    xn2 = _rms(h1, gffn_ref[...])
    xp_ref[0] = _pack_bf16_pairs(xn2)
    x_hi = xn2.astype(_BF)
    x_lo = (xn2 - x_hi.astype(_F32)).astype(_BF)
    w_rt = w_rt_ref[...]
    w_hi = w_rt.astype(_BF)
    w_lo = (w_rt - w_hi.astype(_F32)).astype(_BF)
    logits = (_dot_nt(w_hi, x_hi) + (_dot_nt(w_hi, x_lo) + _dot_nt(w_lo, x_hi))) + b_r_ref[...]
    eidx = lax.broadcasted_iota(jnp.int32, (N_EXPERTS, T), 0)
    vals, sels, ohs = [], [], []
    l = logits
    for _ in range(TOP_K):
        m = jnp.max(l, axis=0, keepdims=True)
        sel = jnp.min(jnp.where(l == m, eidx, N_EXPERTS), axis=0, keepdims=True)
        oh = eidx == sel
        vals.append(m)
        sels.append(sel)
        ohs.append(oh)
        l = jnp.where(oh, neg_inf, l)
    es = [jnp.exp(v - vals[0]) for v in vals]
    den = es[0] + es[1] + es[2] + es[3]
    chosen = jnp.zeros((N_EXPERTS, T), _F32)
    for oh in ohs:
        chosen = chosen + jnp.where(oh, 1.0, 0.0)
    ui = lax.broadcasted_iota(jnp.int32, (T, T), 0)
    uj = lax.broadcasted_iota(jnp.int32, (T, T), 1)
    upper = jnp.where(ui < uj, 1.0, 0.0).astype(_BF)
    before = _dot(chosen.astype(_BF), upper) + cnt_sc[:, 0:1]
    for k in range(TOP_K):
        idx_ref[k:k + 1, :] = sels[k]
        gate_ref[k:k + 1, :] = es[k] / den
        rank_ref[k:k + 1, :] = jnp.sum(jnp.where(ohs[k], before, 0.0), axis=0, keepdims=True).astype(jnp.int32)
    cnt_sc[...] = cnt_sc[...] + _rowsum(chosen)
    cnt_ref[...] = cnt_sc[...]


def _const_spec(shape):
    return pl.BlockSpec(shape, lambda b, s: (0,) * len(shape), pipeline_mode=pl.Buffered(1))


def _mixer_call(batch0, B, x, sinks, cos_t, sin_t, gmix, w_in, b_in, lb, hgn, lvl, w_ua, w_uh, w_out, gffn, w_rt, b_r):
    S = x.shape[1]
    T = SEQ_TILE
    N = B * S
    nS = S // T
    tok_spec = pl.BlockSpec((TOP_K, T), lambda b, s: (0, b * nS + s))
    out_shape = (
        jax.ShapeDtypeStruct((B, S, D_MODEL), _F32),
        jax.ShapeDtypeStruct((B, S, D_MODEL // 2), jnp.uint32),
        jax.ShapeDtypeStruct((TOP_K, N), jnp.int32),
        jax.ShapeDtypeStruct((TOP_K, N), jnp.int32),
        jax.ShapeDtypeStruct((TOP_K, N), _F32),
        jax.ShapeDtypeStruct((N_EXPERTS, LANES), _F32),
    )
    in_specs = [
        pl.BlockSpec(memory_space=pltpu.SMEM),
        pl.BlockSpec((1, T, D_MODEL), lambda b, s: (batch0 + b, s, 0)),
        pl.BlockSpec((T, LANES), lambda b, s: (s, 0)),
        pl.BlockSpec((T, LANES), lambda b, s: (s, 0)),
        _const_spec((1, D_MODEL)),
        _const_spec((D_MODEL, IN_COLS)),
        _const_spec((1, IN_COLS)),
        _const_spec((2, HGRN_WIDTH)),
        _const_spec((1, HGRN_WIDTH)),
        _const_spec((HGRN_CHUNK, HGRN_CHUNK)),
        _const_spec((ATTN_WIDTH, D_MODEL)),
        _const_spec((HGRN_WIDTH, D_MODEL)),
        _const_spec((D_MODEL, D_MODEL)),
        _const_spec((1, D_MODEL)),
        _const_spec((N_EXPERTS, D_MODEL)),
        _const_spec((N_EXPERTS, 1)),
    ]
    out_specs = (
        pl.BlockSpec((1, T, D_MODEL), lambda b, s: (b, s, 0)),
        pl.BlockSpec((1, T, D_MODEL // 2), lambda b, s: (b, s, 0)),
        tok_spec, tok_spec, tok_spec,
        pl.BlockSpec((N_EXPERTS, LANES), lambda b, s: (0, 0)),
    )
    scratch = [
        pltpu.VMEM((WINDOW, KV_WIDTH), _F32),
        pltpu.VMEM((WINDOW, KV_WIDTH), _F32),
        pltpu.VMEM((HGRN_HEADS, HGRN_DK, HGRN_DK), _F32),
        pltpu.VMEM((N_EXPERTS, LANES), _F32),
        pltpu.VMEM((T, ATTN_WIDTH), _F32),
        pltpu.VMEM((T, HGRN_WIDTH), _F32),
    ]
    return pl.pallas_call(
        _mixer_kernel,
        grid=(B, nS),
        in_specs=in_specs,
        out_specs=out_specs,
        out_shape=out_shape,
        scratch_shapes=scratch,
        compiler_params=pltpu.CompilerParams(
            dimension_semantics=("arbitrary", "arbitrary"), vmem_limit_bytes=VMEM_LIMIT),
        name="mixer",
    )(sinks, x, cos_t, sin_t, gmix, w_in, b_in, lb, hgn, lvl, w_ua, w_uh, w_out, gffn, w_rt, b_r)


def _dest_kernel(start_ref, idx_ref, rank_ref, dest_ref):
    idx = idx_ref[...]
    dest = rank_ref[...]
    for e in range(N_EXPERTS):
        dest = dest + jnp.where(idx == e, start_ref[e], 0)
    dest_ref[...] = dest


def _dest_call(pad_starts, idx, rank):
    K, N = idx.shape
    spec = pl.BlockSpec((K, DEST_TILE), lambda i: (0, i))
    return pl.pallas_call(
        _dest_kernel,
        grid=(N // DEST_TILE,),
        in_specs=[pl.BlockSpec(memory_space=pltpu.SMEM), spec, spec],
        out_specs=spec,
        out_shape=jax.ShapeDtypeStruct((K, N), jnp.int32),
        compiler_params=pltpu.CompilerParams(dimension_semantics=("arbitrary",)),
        name="dest",
    )(pad_starts, idx, rank)


def _sc_mesh():
    return plsc.VectorSubcoreMesh(core_axis_name="c", subcore_axis_name="s")


def _sc_worker_id():
    return lax.axis_index("s") * SC_CORES + lax.axis_index("c")


def _sc_scatter_rows(x, dest, n_rows):
    N, C = x.shape
    K = dest.shape[0]
    n_chunks = N // SC_CHUNK
    per_worker = n_chunks // SC_WORKERS
    dest3 = dest.reshape(K, n_chunks, SC_CHUNK)

    @functools.partial(
        pl.kernel, out_type=jax.ShapeDtypeStruct((n_rows, C), x.dtype), mesh=_sc_mesh(),
        scratch_types=[pltpu.VMEM((K, SC_CHUNK), jnp.int32), pltpu.VMEM((SC_CHUNK, C), x.dtype),
                       pltpu.SemaphoreType.DMA])
    def scatter(x_hbm, i_hbm, o_hbm, idx_v, rows_v, sem):
        base = _sc_worker_id() * per_worker

        @pl.loop(0, per_worker)
        def _(j):
            c = base + j
            off = pl.multiple_of(c * SC_CHUNK, SC_CHUNK)
            for k in range(K):
                pltpu.sync_copy(i_hbm.at[k, c], idx_v.at[k])
            pltpu.sync_copy(x_hbm.at[pl.ds(off, SC_CHUNK)], rows_v)
            for k in range(K):
                pltpu.async_copy(rows_v, o_hbm.at[idx_v.at[k]], sem).wait()

    return scatter(x, dest3)


def _sc_gather_rows(table, idx):
    M = idx.shape[0]
    C = table.shape[1]
    per_worker = M // SC_WORKERS
    n_chunks = per_worker // SC_CHUNK

    @functools.partial(
        pl.kernel, out_type=jax.ShapeDtypeStruct((M, C), table.dtype), mesh=_sc_mesh(),
        scratch_types=[pltpu.VMEM((SC_CHUNK,), jnp.int32), pltpu.VMEM((SC_CHUNK, C), table.dtype),
                       pltpu.SemaphoreType.DMA])
    def gather(t_hbm, i_hbm, o_hbm, idx_v, rows_v, sem):
        base = _sc_worker_id() * per_worker

        @pl.loop(0, n_chunks)
        def _(j):
            off = pl.multiple_of(base + j * SC_CHUNK, SC_CHUNK)
            pltpu.sync_copy(i_hbm.at[pl.ds(off, SC_CHUNK)], idx_v)
            pltpu.async_copy(t_hbm.at[idx_v], rows_v, sem).wait()
            pltpu.sync_copy(rows_v, o_hbm.at[pl.ds(off, SC_CHUNK)])

    return gather(table, idx)


def _expert_kernel(blk_e_ref, nb_ref, xs_ref, w1_ref, b1_ref, w2_ref, b2_ref, ys_ref, w1b_ref, w2b_ref):
    i = pl.program_id(0)
    active = i < nb_ref[0]
    new_expert = jnp.logical_or(i == 0, blk_e_ref[i] != blk_e_ref[jnp.maximum(i - 1, 0)])

    @pl.when(jnp.logical_and(active, new_expert))
    def _():
        for r in range(0, D_MODEL, WEIGHT_CAST_ROWS):
            w1b_ref[r:r + WEIGHT_CAST_ROWS, :] = w1_ref[0, r:r + WEIGHT_CAST_ROWS, :].astype(_BF)
        for r in range(0, D_EXPERT, WEIGHT_CAST_ROWS):
            w2b_ref[r:r + WEIGHT_CAST_ROWS, :] = w2_ref[0, r:r + WEIGHT_CAST_ROWS, :].astype(_BF)

    @pl.when(active)
    def _():
        xb = _unpack_bf16_pairs(xs_ref[...]).astype(_BF)
        h = _dot(xb, w1b_ref[...]) + b1_ref[0]
        glu = jnp.minimum(h[:, :D_EXPERT], SWIGLU_LIMIT)
        lin = jnp.clip(h[:, D_EXPERT:], -SWIGLU_LIMIT, SWIGLU_LIMIT)
        act = glu * jax.nn.sigmoid(SWIGLU_ALPHA * glu) * (lin + 1.0)
        y = _dot(act.astype(_BF), w2b_ref[...]) + b2_ref[0]
        ys_ref[...] = _pack_bf16_pairs(y)

    @pl.when(jnp.logical_not(active))
    def _():
        ys_ref[...] = jnp.zeros_like(ys_ref)


def _expert_call(blk_e, nb_used, xs, w1, b1, w2, b2):
    n_blocks = xs.shape[0] // MOE_BLOCK
    M = MOE_BLOCK
    grid_spec = pltpu.PrefetchScalarGridSpec(
        num_scalar_prefetch=2,
        grid=(n_blocks,),
        in_specs=[
            pl.BlockSpec((M, D_MODEL // 2), lambda i, be, nb: (jnp.minimum(i, nb[0] - 1), 0)),
            pl.BlockSpec((1, D_MODEL, 2 * D_EXPERT), lambda i, be, nb: (be[i], 0, 0)),
            pl.BlockSpec((1, 1, 2 * D_EXPERT), lambda i, be, nb: (be[i], 0, 0)),
            pl.BlockSpec((1, D_EXPERT, D_MODEL), lambda i, be, nb: (be[i], 0, 0)),
            pl.BlockSpec((1, 1, D_MODEL), lambda i, be, nb: (be[i], 0, 0)),
        ],
        out_specs=pl.BlockSpec((M, D_MODEL // 2), lambda i, be, nb: (i, 0)),
        scratch_shapes=[pltpu.VMEM((D_MODEL, 2 * D_EXPERT), _BF), pltpu.VMEM((D_EXPERT, D_MODEL), _BF)],
    )
    return pl.pallas_call(
        _expert_kernel,
        grid_spec=grid_spec,
        out_shape=jax.ShapeDtypeStruct(xs.shape, jnp.uint32),
        compiler_params=pltpu.CompilerParams(
            dimension_semantics=("arbitrary",), vmem_limit_bytes=VMEM_LIMIT),
        name="experts",
    )(blk_e, nb_used, xs, w1, b1, w2, b2)


def _combine_kernel(h1_ref, gate_ref, gfin_ref, y0_ref, y1_ref, y2_ref, y3_ref, *out_refs):
    out_ref = out_refs[-1]
    acc = h1_ref[...]
    gates = gate_ref[...]
    for k, y_ref in enumerate((y0_ref, y1_ref, y2_ref, y3_ref)):
        acc = acc + gates[:, k:k + 1] * _unpack_bf16_pairs(y_ref[...])
    out_ref[...] = _rms(acc, gfin_ref[...])


def _combine_call(h1, gates_t, gfin, yg, out_prev, tile0, n_total):
    N = h1.shape[0]
    T = COMBINE_TILE
    nT = N // T
    y_specs = [pl.BlockSpec((T, D_MODEL // 2), functools.partial(lambda i, k: (k * nT + i, 0), k=k))
               for k in range(TOP_K)]
    in_specs = [
        pl.BlockSpec((T, D_MODEL), lambda i: (i, 0)),
        pl.BlockSpec((T, TOP_K), lambda i: (i, 0)),
        pl.BlockSpec((1, D_MODEL), lambda i: (0, 0)),
    ] + y_specs
    args = [h1, gates_t, gfin, yg, yg, yg, yg]
    aliases = {}
    if out_prev is not None:
        in_specs.append(pl.BlockSpec(memory_space=pl.ANY))
        args.append(out_prev)
        aliases = {len(args) - 1: 0}
    return pl.pallas_call(
        _combine_kernel,
        grid=(nT,),
        in_specs=in_specs,
        out_specs=pl.BlockSpec((T, D_MODEL), lambda i: (tile0 + i, 0)),
        out_shape=jax.ShapeDtypeStruct((n_total, D_MODEL), _F32),
        input_output_aliases=aliases,
        compiler_params=pltpu.CompilerParams(dimension_semantics=("arbitrary",)),
        name="combine",
    )(*args)


def _qk_column_permutation():
    half = HEAD_DIM // 2
    r = np.arange(half)

    def pair(base, ha, hb):
        return np.concatenate([base + ha * HEAD_DIM + r, base + hb * HEAD_DIM + r,
                               base + ha * HEAD_DIM + half + r, base + hb * HEAD_DIM + half + r])

    q = np.concatenate([pair(_OFF_Q, 2 * p, 2 * p + 1) for p in range(N_Q_HEADS // 2)])
    k = pair(_OFF_K, 0, 1)
    return np.concatenate([q, k, np.arange(_OFF_V, IN_COLS)])


def _hgrn_level_table():
    t = np.arange(HGRN_CHUNK)[:, None]
    u = np.arange(HGRN_CHUNK)[None, :]
    top = np.floor(np.log2(np.maximum(t ^ u, 1))).astype(np.int32)
    return np.where(t > u, top, -1).astype(np.int32)


def _rope_tables(S):
    half = HEAD_DIM // 2
    inv_freq = ROPE_THETA ** (-(jnp.arange(half, dtype=_F32) * 2.0 / HEAD_DIM))
    ang = jnp.arange(S, dtype=_F32)[:, None] * inv_freq[None, :]
    cos = jnp.tile(jnp.cos(ang), (1, 4))
    sin = jnp.tile(jnp.sin(ang), (1, 4))
    sign = jnp.where(jnp.arange(LANES) < HEAD_DIM, -1.0, 1.0).astype(_F32)
    return cos, sin * sign[None, :]


def kernel(x, norm_mix_g, w_in, b_in, attn_sinks, hgrn_lb, hgrn_norm_g, w_up_attn, w_up_hgrn, w_out,
           norm_ffn_g, w_router, b_router, w_moe1, b_moe1, w_moe2, b_moe2, norm_final_g):
    B, S, D = x.shape
    N = B * S
    assert D == D_MODEL and S % SEQ_TILE == 0 and SEQ_TILE % WINDOW == 0
    assert B % BATCH_PARTS == 0 and (N // BATCH_PARTS) % (SC_WORKERS * SC_CHUNK) == 0
    assert (N // BATCH_PARTS) % COMBINE_TILE == 0 and (N // BATCH_PARTS) % DEST_TILE == 0
    assert norm_mix_g.shape[0] == 1 and hgrn_lb.shape[0] == 2
    assert GROUP == 4 and N_KV_HEADS == 2 and 2 * HEAD_DIM == LANES

    perm = _qk_column_permutation()
    cos_t, sin_t = _rope_tables(S)
    weights = (attn_sinks[0], cos_t, sin_t, norm_mix_g,
               w_in[0][:, perm].astype(_BF), b_in[:, perm], hgrn_lb, hgrn_norm_g, jnp.asarray(_hgrn_level_table()),
               w_up_attn[0].astype(_BF), w_up_hgrn[0].astype(_BF), w_out[0].astype(_BF),
               norm_ffn_g, w_router[0].T, b_router[0][:, None])

    Bp = B // BATCH_PARTS
    Np = Bp * S
    n_blocks = (Np * TOP_K) // MOE_BLOCK + N_EXPERTS
    out = None
    for part in range(BATCH_PARTS):
        h1, xp, idx, rank, gates, cnt = _mixer_call(part * Bp, Bp, x, *weights)

        counts = cnt[:, 0].astype(jnp.int32)
        padded = ((counts + MOE_BLOCK - 1) // MOE_BLOCK) * MOE_BLOCK
        pad_ends = jnp.cumsum(padded)
        pad_starts = pad_ends - padded
        dest = _dest_call(pad_starts, idx, rank)
        blk_start = jnp.arange(n_blocks, dtype=jnp.int32) * MOE_BLOCK
        blk_e = jnp.minimum(jnp.sum((pad_ends[None, :] <= blk_start[:, None]).astype(jnp.int32), axis=1),
                            N_EXPERTS - 1)
        nb_used = (pad_ends[-1:] // MOE_BLOCK).astype(jnp.int32)

        xs = _sc_scatter_rows(xp.reshape(Np, D // 2), dest, n_blocks * MOE_BLOCK)
        ys = _expert_call(blk_e, nb_used, xs, w_moe1[0], b_moe1[0][:, None, :], w_moe2[0], b_moe2[0][:, None, :])
        yg = _sc_gather_rows(ys, dest.reshape(TOP_K * Np))
        out = _combine_call(h1.reshape(Np, D), gates.T, norm_final_g[None, :], yg, out,
                            part * (Np // COMBINE_TILE), N)
    return out.reshape(B, S, D)
```

```python
import functools

import numpy as np
import jax
import jax.numpy as jnp
from jax import lax
from jax.experimental import pallas as pl
from jax.experimental.pallas import tpu as pltpu
from jax.experimental.pallas import tpu_sc as plsc

D_MODEL = 1024
HEAD_DIM = 64
ROT_HALF = HEAD_DIM // 2
N_Q_HEADS = 8
N_KV_HEADS = 2
GROUP = N_Q_HEADS // N_KV_HEADS
ATTN_WIDTH = N_Q_HEADS * HEAD_DIM
KV_WIDTH = N_KV_HEADS * HEAD_DIM
WINDOW = 128
ROPE_THETA = 10000.0
HGRN_HEADS = 4
HGRN_DK = 128
HGRN_WIDTH = HGRN_HEADS * HGRN_DK
N_EXPERTS = 32
TOP_K = 4
D_EXPERT = 1024
SWIGLU_ALPHA = 1.702
SWIGLU_LIMIT = 7.0
MOE_BLOCK = 512
NORM_EPS = 1e-5

_OFF_Q = 0
_OFF_K = _OFF_Q + ATTN_WIDTH
_OFF_V = _OFF_K + KV_WIDTH
_OFF_HQ = _OFF_V + KV_WIDTH
_OFF_HF = _OFF_HQ + HGRN_WIDTH
_OFF_HI = _OFF_HF + HGRN_WIDTH
_OFF_HG = _OFF_HI + HGRN_WIDTH
_OFF_GA = _OFF_HG + HGRN_WIDTH
_OFF_GH = _OFF_GA + D_MODEL
IN_COLS = _OFF_GH + D_MODEL

LANES = 128
SUBLANES = 8
SEQ_TILE = 512
HGRN_CHUNK = 256
GATE_JOB_COLS = 256
COMBINE_TILE = 512
DEST_TILE = 8192
BATCH_PARTS = 2
WEIGHT_CAST_ROWS = 128
SC_CORES = 2
SC_WORKERS = 32
SC_CHUNK = 128
VMEM_LIMIT = 56 * 1024 * 1024

_BF = jnp.bfloat16
_F32 = jnp.float32


def _dot(a, b):
    return jnp.dot(a, b, preferred_element_type=_F32)


def _dot_nt(a, b):
    return lax.dot_general(a, b, (((1,), (1,)), ((), ())), preferred_element_type=_F32)


def _dot_tn(a, b):
    return lax.dot_general(a, b, (((0,), (0,)), ((), ())), preferred_element_type=_F32)


def _rowsum(x):
    return jnp.sum(x, axis=1, keepdims=True)


def _rms(x, g):
    ms = _rowsum(x * x) * (1.0 / x.shape[1])
    return x * lax.rsqrt(ms + NORM_EPS) * g


def _pack_bf16_pairs(x):
    n = x.shape[1] // 2
    lo = lax.bitcast_convert_type(x[:, :n].astype(_BF).astype(_F32), jnp.uint32)
    hi = lax.bitcast_convert_type(x[:, n:].astype(_BF).astype(_F32), jnp.uint32)
    return (lo >> 16) | (hi & jnp.uint32(0xFFFF0000))


def _unpack_bf16_pairs(u):
    lo = lax.bitcast_convert_type(u << 16, _F32)
    hi = lax.bitcast_convert_type(u & jnp.uint32(0xFFFF0000), _F32)
    return jnp.concatenate([lo, hi], axis=1)


def _mixer_kernel(sinks_ref, x_ref, cos_ref, sin_ref, gmix_ref, w_in_ref, b_in_ref, lb_ref,
                  hgn_ref, lvl_ref, w_ua_ref, w_uh_ref, w_out_ref, gffn_ref, w_rt_ref, b_r_ref,
                  h1_ref, xp_ref, idx_ref, rank_ref, gate_ref, cnt_ref,
                  kc_ref, vc_ref, st_ref, cnt_sc, ya_ref, o_ref):
    T = SEQ_TILE
    b = pl.program_id(0)
    s = pl.program_id(1)

    @pl.when(jnp.logical_and(b == 0, s == 0))
    def _():
        cnt_sc[...] = jnp.zeros_like(cnt_sc)

    @pl.when(s == 0)
    def _():
        kc_ref[...] = jnp.zeros_like(kc_ref)
        vc_ref[...] = jnp.zeros_like(vc_ref)
        st_ref[...] = jnp.zeros_like(st_ref)

    x = x_ref[0]
    xn = _rms(x, gmix_ref[...]).astype(_BF)

    def proj(off, width):
        return _dot(xn, w_in_ref[:, off:off + width]) + b_in_ref[:, off:off + width]

    cos = cos_ref[...]
    sin = sin_ref[...]

    lane = lax.broadcasted_iota(jnp.int32, (1, LANES), 1)
    first_half = (lane % HEAD_DIM) < ROT_HALF

    def rope(t):
        swapped = jnp.where(first_half, pltpu.roll(t, LANES - ROT_HALF, axis=1), pltpu.roll(t, ROT_HALF, axis=1))
        return t * cos + swapped * sin

    zq = proj(_OFF_Q, ATTN_WIDTH)
    scale = HEAD_DIM ** -0.5
    q_pairs = [(rope(zq[:, LANES * p:LANES * (p + 1)]) * scale).astype(_BF) for p in range(N_Q_HEADS // 2)]
    k_rot = rope(proj(_OFF_K, KV_WIDTH))
    v_new = proj(_OFF_V, KV_WIDTH)

    k_ext = jnp.concatenate([kc_ref[...], k_rot], axis=0)
    v_ext = jnp.concatenate([vc_ref[...], v_new], axis=0)
    kc_ref[...] = k_rot[T - WINDOW:, :]
    vc_ref[...] = v_new[T - WINDOW:, :]

    lane_lo = lane < HEAD_DIM
    k_r64 = pltpu.roll(k_ext, HEAD_DIM, axis=1)
    v_r64 = pltpu.roll(v_ext, HEAD_DIM, axis=1)
    zero = jnp.zeros_like(k_ext)
    k_var = [(jnp.where(lane_lo, k_ext, zero).astype(_BF), jnp.where(lane_lo, zero, k_r64).astype(_BF)),
             (jnp.where(lane_lo, k_r64, zero).astype(_BF), jnp.where(lane_lo, zero, k_ext).astype(_BF))]
    v_var = [(jnp.where(lane_lo, v_ext, zero).astype(_BF), jnp.where(lane_lo, zero, v_r64).astype(_BF)),
             (jnp.where(lane_lo, v_r64, zero).astype(_BF), jnp.where(lane_lo, zero, v_ext).astype(_BF))]

    qi = lax.broadcasted_iota(jnp.int32, (2 * WINDOW, 2 * WINDOW), 0) % WINDOW
    kj = lax.broadcasted_iota(jnp.int32, (2 * WINDOW, 2 * WINDOW), 1)
    band = jnp.logical_and(kj > qi, kj <= qi + WINDOW)
    row_top = lax.broadcasted_iota(jnp.int32, (2 * WINDOW, 1), 0) < WINDOW
    neg_inf = jnp.float32(-jnp.inf)

    gate_cols = [(_OFF_GA + c, GATE_JOB_COLS) for c in range(0, 2 * D_MODEL, GATE_JOB_COLS)]
    gate_parts = []
    jobs_per_unit = -(-len(gate_cols) // ((T // WINDOW) * N_KV_HEADS))

    for n in range(T // WINDOW):
        if n == 0:
            ok = jnp.logical_and(band, jnp.logical_or(kj >= WINDOW, s > 0))
        else:
            ok = band
        r0 = n * WINDOW
        for j in range(N_KV_HEADS):
            q2 = jnp.concatenate([q_pairs[2 * j][r0:r0 + WINDOW], q_pairs[2 * j + 1][r0:r0 + WINDOW]], axis=0)
            kcat = jnp.concatenate([k_var[j][0][r0:r0 + 2 * WINDOW], k_var[j][1][r0:r0 + 2 * WINDOW]], axis=0)
            vcat = jnp.concatenate([v_var[j][0][r0:r0 + 2 * WINDOW], v_var[j][1][r0:r0 + 2 * WINDOW]], axis=0)
            sc = _dot_nt(q2, kcat)
            ps, rs = [], []
            for half in range(2):
                sh = jnp.where(ok, sc[:, 2 * WINDOW * half:2 * WINDOW * (half + 1)], neg_inf)
                snk = jnp.where(row_top, sinks_ref[GROUP * j + half], sinks_ref[GROUP * j + 2 + half])
                m = jnp.maximum(jnp.max(sh, axis=1, keepdims=True), snk)
                p = jnp.exp(sh - m)
                den = _rowsum(p) + jnp.exp(snk - m)
                ps.append(p.astype(_BF))
                rs.append(1.0 / den)
            o = _dot(jnp.concatenate(ps, axis=1), vcat)
            o = o * jnp.where(lane_lo, rs[0], rs[1])
            ya_ref[r0:r0 + WINDOW, LANES * (2 * j):LANES * (2 * j + 1)] = o[:WINDOW]
            ya_ref[r0:r0 + WINDOW, LANES * (2 * j + 1):LANES * (2 * j + 2)] = o[WINDOW:]
            for _ in range(min(jobs_per_unit, len(gate_cols) - len(gate_parts))):
                gate_parts.append(proj(*gate_cols[len(gate_parts)]))

    up_a = _dot(ya_ref[...].astype(_BF), w_ua_ref[...])

    HW = HGRN_WIDTH
    a0 = lb_ref[0:1, :]
    a1 = lb_ref[1:2, :]
    am = jnp.maximum(a0, a1)
    e0 = jnp.exp(a0 - am)
    lb = e0 / (e0 + jnp.exp(a1 - am))
    f = lb + (1.0 - lb) * jax.nn.sigmoid(proj(_OFF_HF, HW))
    kk = 1.0 - f
    g = jnp.log2(f)
    hq = proj(_OFF_HQ, HW)
    hv = proj(_OFF_HI, HW)

    row = lax.broadcasted_iota(jnp.int32, (T, 1), 0)
    C = HGRN_CHUNK

    def rdown(t, d):
        return pltpu.roll(t.reshape(T // SUBLANES, SUBLANES, HW), d, axis=1).reshape(T, HW)

    def rup(t, d):
        return pltpu.roll(t.reshape(T // SUBLANES, SUBLANES, HW), SUBLANES - d, axis=1).reshape(T, HW)

    def head(t, h):
        return t[:, HGRN_DK * h:HGRN_DK * (h + 1)]

    qk0 = hq * kk
    o_parts = [_rowsum(head(qk0, h)) * head(hv, h) for h in range(HGRN_HEADS)]

    small_levels = []
    bsz = 1
    lb_cum, lb_tot = g, g
    while bsz < SUBLANES:
        odd = (row % (2 * bsz)) >= bsz
        ex = jnp.exp2(jnp.where(odd, lb_cum, lb_tot - lb_cum))
        small_levels.append(((hq * ex).astype(_BF), (kk * ex).astype(_BF)))
        prev_tot = rdown(lb_tot, bsz)
        lb_cum = lb_cum + jnp.where(odd, prev_tot, 0.0)
        lb_tot = lb_tot + jnp.where(odd, prev_tot, rup(lb_tot, bsz))
        bsz *= 2

    cum = [lb_cum[r:r + bsz] for r in range(0, T, bsz)]
    tot = [lb_tot[r:r + 1] for r in range(0, T, bsz)]
    kk_b = kk.astype(_BF)
    big_levels = []
    while bsz < C:
        q_rows, k_rows = [], []
        for j, r in enumerate(range(0, T, bsz)):
            if j % 2:
                q_rows.append(hq[r:r + bsz] * jnp.exp2(cum[j]))
                k_rows.append(kk_b[r:r + bsz])
            else:
                k_rows.append((kk[r:r + bsz] * jnp.exp2(tot[j] - cum[j])).astype(_BF))
        big_levels.append((bsz, jnp.concatenate(q_rows, axis=0).astype(_BF), jnp.concatenate(k_rows, axis=0)))
        cum = [jnp.concatenate([cum[j], cum[j + 1] + tot[j]], axis=0) for j in range(0, len(cum), 2)]
        tot = [tot[j] + tot[j + 1] for j in range(0, len(tot), 2)]
        bsz *= 2
    hv_b = hv.astype(_BF)
    lvl = lvl_ref[...]

    for c in range(T // C):
        rs_ = slice(c * C, (c + 1) * C)
        rh_ = slice(c * C // 2, (c + 1) * C // 2)
        q_in = (hq[rs_] * jnp.exp2(cum[c])).astype(_BF)
        k_st = (kk[rs_] * jnp.exp2(tot[c] - cum[c])).astype(_BF)
        dec = jnp.exp2(tot[c])
        for h in range(HGRN_HEADS):
            cs_ = slice(HGRN_DK * h, HGRN_DK * (h + 1))
            st = st_ref[h]
            vc = hv_b[rs_, cs_]
            inter = _dot_nt(q_in[:, cs_], st.astype(_BF))
            zero_tile = jnp.zeros((SUBLANES, LANES), _F32)
            tiles = [[zero_tile for _ in range(C // LANES)] for _ in range(C // SUBLANES)]

            def place(li, p, p_row, rb, col):
                r, ct = rb * SUBLANES, col // LANES
                cs = slice(ct * LANES, (ct + 1) * LANES)
                tiles[rb][ct] = jnp.where(lvl[r:r + SUBLANES, cs] == li, p[p_row:p_row + SUBLANES, cs], tiles[rb][ct])

            for li, (ql, kl) in enumerate(small_levels):
                p = _dot_nt(ql[rs_, cs_], kl[rs_, cs_])
                for rb in range(C // SUBLANES):
                    place(li, p, rb * SUBLANES, rb, rb * SUBLANES)
            for li, (bs, ql, kl) in enumerate(big_levels, start=len(small_levels)):
                p = _dot_nt(ql[rh_, cs_], kl[rs_, cs_])
                for rb in range(C // SUBLANES):
                    blk = (rb * SUBLANES) // bs
                    if blk % 2:
                        place(li, p, rb * SUBLANES - (blk + 1) // 2 * bs, rb, (blk - 1) * bs)
            amat = jnp.concatenate([jnp.concatenate(row_tiles, axis=1) for row_tiles in tiles], axis=0)
            intra = _dot(amat.astype(_BF), vc)
            st_ref[h] = st * dec[:, cs_] + _dot_tn(vc, k_st[:, cs_])
            o_ref[rs_, cs_] = inter + intra + o_parts[h][rs_]

    o = o_ref[...]
    hg = proj(_OFF_HG, HW)
    yh_parts = []
    for h in range(HGRN_HEADS):
        oh = head(o, h)
        ms = _rowsum(oh * oh) * (1.0 / HGRN_DK)
        yh_parts.append(oh * lax.rsqrt(ms + NORM_EPS))
    yh = jnp.concatenate(yh_parts, axis=1) * hgn_ref[...]
    yh = (yh * (hg * jax.nn.sigmoid(hg))).astype(_BF)

    up_h = _dot(yh, w_uh_ref[...])
    half = len(gate_parts) // 2
    z_ga = jnp.concatenate(gate_parts[:half], axis=1)
    z_gh = jnp.concatenate(gate_parts[half:], axis=1)
    merged = jax.nn.sigmoid(z_ga) * up_a + jax.nn.sigmoid(z_gh) * up_h
    h1 = x + _dot(merged.astype(_BF), w_out_ref[...])
    h1_ref[0] = h1

    xn2 = _rms(h1, gffn_ref[...])
    xp_ref[0] = _pack_bf16_pairs(xn2)
    x_hi = xn2.astype(_BF)
    x_lo = (xn2 - x_hi.astype(_F32)).astype(_BF)
    w_rt = w_rt_ref[...]
    w_hi = w_rt.astype(_BF)
    w_lo = (w_rt - w_hi.astype(_F32)).astype(_BF)
    logits = (_dot_nt(w_hi, x_hi) + (_dot_nt(w_hi, x_lo) + _dot_nt(w_lo, x_hi))) + b_r_ref[...]
    eidx = lax.broadcasted_iota(jnp.int32, (N_EXPERTS, T), 0)
    vals, sels, ohs = [], [], []
    l = logits
    for _ in range(TOP_K):
        m = jnp.max(l, axis=0, keepdims=True)
        sel = jnp.min(jnp.where(l == m, eidx, N_EXPERTS), axis=0, keepdims=True)
        oh = eidx == sel
        vals.append(m)
        sels.append(sel)
        ohs.append(oh)
        l = jnp.where(oh, neg_inf, l)
    es = [jnp.exp(v - vals[0]) for v in vals]
    den = es[0] + es[1] + es[2] + es[3]
    chosen = jnp.zeros((N_EXPERTS, T), _F32)
    for oh in ohs:
        chosen = chosen + jnp.where(oh, 1.0, 0.0)
    ui = lax.broadcasted_iota(jnp.int32, (T, T), 0)
    uj = lax.broadcasted_iota(jnp.int32, (T, T), 1)
    upper = jnp.where(ui < uj, 1.0, 0.0).astype(_BF)
    before = _dot(chosen.astype(_BF), upper) + cnt_sc[:, 0:1]
    for k in range(TOP_K):
        idx_ref[k:k + 1, :] = sels[k]
        gate_ref[k:k + 1, :] = es[k] / den
        rank_ref[k:k + 1, :] = jnp.sum(jnp.where(ohs[k], before, 0.0), axis=0, keepdims=True).astype(jnp.int32)
    cnt_sc[...] = cnt_sc[...] + _rowsum(chosen)
    cnt_ref[...] = cnt_sc[...]


def _const_spec(shape):
    return pl.BlockSpec(shape, lambda b, s: (0,) * len(shape), pipeline_mode=pl.Buffered(1))


def _mixer_call(batch0, B, x, sinks, cos_t, sin_t, gmix, w_in, b_in, lb, hgn, lvl, w_ua, w_uh, w_out, gffn, w_rt, b_r):
    S = x.shape[1]
    T = SEQ_TILE
    N = B * S
    nS = S // T
    tok_spec = pl.BlockSpec((TOP_K, T), lambda b, s: (0, b * nS + s))
    out_shape = (
        jax.ShapeDtypeStruct((B, S, D_MODEL), _F32),
        jax.ShapeDtypeStruct((B, S, D_MODEL // 2), jnp.uint32),
        jax.ShapeDtypeStruct((TOP_K, N), jnp.int32),
        jax.ShapeDtypeStruct((TOP_K, N), jnp.int32),
        jax.ShapeDtypeStruct((TOP_K, N), _F32),
        jax.ShapeDtypeStruct((N_EXPERTS, LANES), _F32),
    )
    in_specs = [
        pl.BlockSpec(memory_space=pltpu.SMEM),
        pl.BlockSpec((1, T, D_MODEL), lambda b, s: (batch0 + b, s, 0)),
        pl.BlockSpec((T, LANES), lambda b, s: (s, 0)),
        pl.BlockSpec((T, LANES), lambda b, s: (s, 0)),
        _const_spec((1, D_MODEL)),
        _const_spec((D_MODEL, IN_COLS)),
        _const_spec((1, IN_COLS)),
        _const_spec((2, HGRN_WIDTH)),
        _const_spec((1, HGRN_WIDTH)),
        _const_spec((HGRN_CHUNK, HGRN_CHUNK)),
        _const_spec((ATTN_WIDTH, D_MODEL)),
        _const_spec((HGRN_WIDTH, D_MODEL)),
        _const_spec((D_MODEL, D_MODEL)),
        _const_spec((1, D_MODEL)),
        _const_spec((N_EXPERTS, D_MODEL)),
        _const_spec((N_EXPERTS, 1)),
    ]
    out_specs = (
        pl.BlockSpec((1, T, D_MODEL), lambda b, s: (b, s, 0)),
        pl.BlockSpec((1, T, D_MODEL // 2), lambda b, s: (b, s, 0)),
        tok_spec, tok_spec, tok_spec,
        pl.BlockSpec((N_EXPERTS, LANES), lambda b, s: (0, 0)),
    )
    scratch = [
        pltpu.VMEM((WINDOW, KV_WIDTH), _F32),
        pltpu.VMEM((WINDOW, KV_WIDTH), _F32),
        pltpu.VMEM((HGRN_HEADS, HGRN_DK, HGRN_DK), _F32),
        pltpu.VMEM((N_EXPERTS, LANES), _F32),
        pltpu.VMEM((T, ATTN_WIDTH), _F32),
        pltpu.VMEM((T, HGRN_WIDTH), _F32),
    ]
    return pl.pallas_call(
        _mixer_kernel,
        grid=(B, nS),
        in_specs=in_specs,
        out_specs=out_specs,
        out_shape=out_shape,
        scratch_shapes=scratch,
        compiler_params=pltpu.CompilerParams(
            dimension_semantics=("arbitrary", "arbitrary"), vmem_limit_bytes=VMEM_LIMIT),
        name="mixer",
    )(sinks, x, cos_t, sin_t, gmix, w_in, b_in, lb, hgn, lvl, w_ua, w_uh, w_out, gffn, w_rt, b_r)


def _dest_kernel(start_ref, idx_ref, rank_ref, dest_ref):
    idx = idx_ref[...]
    dest = rank_ref[...]
    for e in range(N_EXPERTS):
        dest = dest + jnp.where(idx == e, start_ref[e], 0)
    dest_ref[...] = dest


def _dest_call(pad_starts, idx, rank):
    K, N = idx.shape
    spec = pl.BlockSpec((K, DEST_TILE), lambda i: (0, i))
    return pl.pallas_call(
        _dest_kernel,
        grid=(N // DEST_TILE,),
        in_specs=[pl.BlockSpec(memory_space=pltpu.SMEM), spec, spec],
        out_specs=spec,
        out_shape=jax.ShapeDtypeStruct((K, N), jnp.int32),
        compiler_params=pltpu.CompilerParams(dimension_semantics=("arbitrary",)),
        name="dest",
    )(pad_starts, idx, rank)


def _sc_mesh():
    return plsc.VectorSubcoreMesh(core_axis_name="c", subcore_axis_name="s")


def _sc_worker_id():
    return lax.axis_index("s") * SC_CORES + lax.axis_index("c")


def _sc_scatter_rows(x, dest, n_rows):
    N, C = x.shape
    K = dest.shape[0]
    n_chunks = N // SC_CHUNK
    per_worker = n_chunks // SC_WORKERS
    dest3 = dest.reshape(K, n_chunks, SC_CHUNK)

    @functools.partial(
        pl.kernel, out_type=jax.ShapeDtypeStruct((n_rows, C), x.dtype), mesh=_sc_mesh(),
        scratch_types=[pltpu.VMEM((K, SC_CHUNK), jnp.int32), pltpu.VMEM((SC_CHUNK, C), x.dtype),
                       pltpu.SemaphoreType.DMA])
    def scatter(x_hbm, i_hbm, o_hbm, idx_v, rows_v, sem):
        base = _sc_worker_id() * per_worker

        @pl.loop(0, per_worker)
        def _(j):
            c = base + j
            off = pl.multiple_of(c * SC_CHUNK, SC_CHUNK)
            for k in range(K):
                pltpu.sync_copy(i_hbm.at[k, c], idx_v.at[k])
            pltpu.sync_copy(x_hbm.at[pl.ds(off, SC_CHUNK)], rows_v)
            for k in range(K):
                pltpu.async_copy(rows_v, o_hbm.at[idx_v.at[k]], sem).wait()

    return scatter(x, dest3)


def _sc_gather_rows(table, idx):
    M = idx.shape[0]
    C = table.shape[1]
    per_worker = M // SC_WORKERS
    n_chunks = per_worker // SC_CHUNK

    @functools.partial(
        pl.kernel, out_type=jax.ShapeDtypeStruct((M, C), table.dtype), mesh=_sc_mesh(),
        scratch_types=[pltpu.VMEM((SC_CHUNK,), jnp.int32), pltpu.VMEM((SC_CHUNK, C), table.dtype),
                       pltpu.SemaphoreType.DMA])
    def gather(t_hbm, i_hbm, o_hbm, idx_v, rows_v, sem):
        base = _sc_worker_id() * per_worker

        @pl.loop(0, n_chunks)
        def _(j):
            off = pl.multiple_of(base + j * SC_CHUNK, SC_CHUNK)
            pltpu.sync_copy(i_hbm.at[pl.ds(off, SC_CHUNK)], idx_v)
            pltpu.async_copy(t_hbm.at[idx_v], rows_v, sem).wait()
            pltpu.sync_copy(rows_v, o_hbm.at[pl.ds(off, SC_CHUNK)])

    return gather(table, idx)


def _expert_kernel(blk_e_ref, nb_ref, xs_ref, w1_ref, b1_ref, w2_ref, b2_ref, ys_ref, w1b_ref, w2b_ref):
    i = pl.program_id(0)
    active = i < nb_ref[0]
    new_expert = jnp.logical_or(i == 0, blk_e_ref[i] != blk_e_ref[jnp.maximum(i - 1, 0)])

    @pl.when(jnp.logical_and(active, new_expert))
    def _():
        for r in range(0, D_MODEL, WEIGHT_CAST_ROWS):
            w1b_ref[r:r + WEIGHT_CAST_ROWS, :] = w1_ref[0, r:r + WEIGHT_CAST_ROWS, :].astype(_BF)
        for r in range(0, D_EXPERT, WEIGHT_CAST_ROWS):
            w2b_ref[r:r + WEIGHT_CAST_ROWS, :] = w2_ref[0, r:r + WEIGHT_CAST_ROWS, :].astype(_BF)

    @pl.when(active)
    def _():
        xb = _unpack_bf16_pairs(xs_ref[...]).astype(_BF)
        h = _dot(xb, w1b_ref[...]) + b1_ref[0]
        glu = jnp.minimum(h[:, :D_EXPERT], SWIGLU_LIMIT)
        lin = jnp.clip(h[:, D_EXPERT:], -SWIGLU_LIMIT, SWIGLU_LIMIT)
        act = glu * jax.nn.sigmoid(SWIGLU_ALPHA * glu) * (lin + 1.0)
        y = _dot(act.astype(_BF), w2b_ref[...]) + b2_ref[0]
        ys_ref[...] = _pack_bf16_pairs(y)

    @pl.when(jnp.logical_not(active))
    def _():
        ys_ref[...] = jnp.zeros_like(ys_ref)


def _expert_call(blk_e, nb_used, xs, w1, b1, w2, b2):
    n_blocks = xs.shape[0] // MOE_BLOCK
    M = MOE_BLOCK
    grid_spec = pltpu.PrefetchScalarGridSpec(
        num_scalar_prefetch=2,
        grid=(n_blocks,),
        in_specs=[
            pl.BlockSpec((M, D_MODEL // 2), lambda i, be, nb: (jnp.minimum(i, nb[0] - 1), 0)),
            pl.BlockSpec((1, D_MODEL, 2 * D_EXPERT), lambda i, be, nb: (be[i], 0, 0)),
            pl.BlockSpec((1, 1, 2 * D_EXPERT), lambda i, be, nb: (be[i], 0, 0)),
            pl.BlockSpec((1, D_EXPERT, D_MODEL), lambda i, be, nb: (be[i], 0, 0)),
            pl.BlockSpec((1, 1, D_MODEL), lambda i, be, nb: (be[i], 0, 0)),
        ],
        out_specs=pl.BlockSpec((M, D_MODEL // 2), lambda i, be, nb: (i, 0)),
        scratch_shapes=[pltpu.VMEM((D_MODEL, 2 * D_EXPERT), _BF), pltpu.VMEM((D_EXPERT, D_MODEL), _BF)],
    )
    return pl.pallas_call(
        _expert_kernel,
        grid_spec=grid_spec,
        out_shape=jax.ShapeDtypeStruct(xs.shape, jnp.uint32),
        compiler_params=pltpu.CompilerParams(
            dimension_semantics=("arbitrary",), vmem_limit_bytes=VMEM_LIMIT),
        name="experts",
    )(blk_e, nb_used, xs, w1, b1, w2, b2)


def _combine_kernel(h1_ref, gate_ref, gfin_ref, y0_ref, y1_ref, y2_ref, y3_ref, *out_refs):
    out_ref = out_refs[-1]
    acc = h1_ref[...]
    g = gate_ref[...]
    gates = jnp.concatenate([g, jnp.zeros_like(g)], axis=0).T
    for k, y_ref in enumerate((y0_ref, y1_ref, y2_ref, y3_ref)):
        acc = acc + gates[:, k:k + 1] * _unpack_bf16_pairs(y_ref[...])
    out_ref[...] = _rms(acc, gfin_ref[...])


def _combine_call(h1, gates, gfin, yg, out_prev, tile0, n_total):
    N = h1.shape[0]
    T = COMBINE_TILE
    nT = N // T
    y_specs = [pl.BlockSpec((T, D_MODEL // 2), functools.partial(lambda i, k: (k * nT + i, 0), k=k))
               for k in range(TOP_K)]
    in_specs = [
        pl.BlockSpec((T, D_MODEL), lambda i: (i, 0)),
        pl.BlockSpec((TOP_K, T), lambda i: (0, i)),
        pl.BlockSpec((1, D_MODEL), lambda i: (0, 0)),
    ] + y_specs
    args = [h1, gates, gfin, yg, yg, yg, yg]
    aliases = {}
    if out_prev is not None:
        in_specs.append(pl.BlockSpec(memory_space=pl.ANY))
        args.append(out_prev)
        aliases = {len(args) - 1: 0}
    return pl.pallas_call(
        _combine_kernel,
        grid=(nT,),
        in_specs=in_specs,
        out_specs=pl.BlockSpec((T, D_MODEL), lambda i: (tile0 + i, 0)),
        out_shape=jax.ShapeDtypeStruct((n_total, D_MODEL), _F32),
        input_output_aliases=aliases,
        compiler_params=pltpu.CompilerParams(dimension_semantics=("arbitrary",)),
        name="combine",
    )(*args)


def _hgrn_level_table():
    t = np.arange(HGRN_CHUNK)[:, None]
    u = np.arange(HGRN_CHUNK)[None, :]
    top = np.floor(np.log2(np.maximum(t ^ u, 1))).astype(np.int32)
    return np.where(t > u, top, -1).astype(np.int32)


def _rope_tables(S):
    half = HEAD_DIM // 2
    inv_freq = ROPE_THETA ** (-(jnp.arange(half, dtype=_F32) * 2.0 / HEAD_DIM))
    ang = jnp.arange(S, dtype=_F32)[:, None] * inv_freq[None, :]
    cos = jnp.tile(jnp.cos(ang), (1, 4))
    sin = jnp.tile(jnp.sin(ang), (1, 4))
    sign = jnp.where(jnp.arange(LANES) % HEAD_DIM < ROT_HALF, -1.0, 1.0).astype(_F32)
    return cos, sin * sign[None, :]


def kernel(x, norm_mix_g, w_in, b_in, attn_sinks, hgrn_lb, hgrn_norm_g, w_up_attn, w_up_hgrn, w_out,
           norm_ffn_g, w_router, b_router, w_moe1, b_moe1, w_moe2, b_moe2, norm_final_g):
    B, S, D = x.shape
    N = B * S
    assert D == D_MODEL and S % SEQ_TILE == 0 and SEQ_TILE % WINDOW == 0
    assert B % BATCH_PARTS == 0 and (N // BATCH_PARTS) % (SC_WORKERS * SC_CHUNK) == 0
    assert (N // BATCH_PARTS) % COMBINE_TILE == 0 and (N // BATCH_PARTS) % DEST_TILE == 0
    assert norm_mix_g.shape[0] == 1 and hgrn_lb.shape[0] == 2
    assert GROUP == 4 and N_KV_HEADS == 2 and 2 * HEAD_DIM == LANES

    cos_t, sin_t = _rope_tables(S)
    weights = (attn_sinks[0], cos_t, sin_t, norm_mix_g,
               w_in[0].astype(_BF), b_in, hgrn_lb, hgrn_norm_g, jnp.asarray(_hgrn_level_table()),
               w_up_attn[0].astype(_BF), w_up_hgrn[0].astype(_BF), w_out[0].astype(_BF),
               norm_ffn_g, w_router[0].T, b_router[0][:, None])

    Bp = B // BATCH_PARTS
    Np = Bp * S
    n_blocks = (Np * TOP_K) // MOE_BLOCK + N_EXPERTS
    out = None
    for part in range(BATCH_PARTS):
        h1, xp, idx, rank, gates, cnt = _mixer_call(part * Bp, Bp, x, *weights)

        counts = cnt[:, 0].astype(jnp.int32)
        padded = ((counts + MOE_BLOCK - 1) // MOE_BLOCK) * MOE_BLOCK
        pad_ends = jnp.cumsum(padded)
        pad_starts = pad_ends - padded
        dest = _dest_call(pad_starts, idx, rank)
        blk_start = jnp.arange(n_blocks, dtype=jnp.int32) * MOE_BLOCK
        blk_e = jnp.minimum(jnp.sum((pad_ends[None, :] <= blk_start[:, None]).astype(jnp.int32), axis=1),
                            N_EXPERTS - 1)
        nb_used = (pad_ends[-1:] // MOE_BLOCK).astype(jnp.int32)

        xs = _sc_scatter_rows(xp.reshape(Np, D // 2), dest, n_blocks * MOE_BLOCK)
        ys = _expert_call(blk_e, nb_used, xs, w_moe1[0], b_moe1[0][:, None, :], w_moe2[0], b_moe2[0][:, None, :])
        yg = _sc_gather_rows(ys, dest.reshape(TOP_K * Np))
        out = _combine_call(h1.reshape(Np, D), gates, norm_final_g[None, :], yg, out,
                            part * (Np // COMBINE_TILE), N)
    return out.reshape(B, S, D)
```

```python
import functools

import numpy as np
import jax
import jax.numpy as jnp
from jax import lax
from jax.experimental import pallas as pl
from jax.experimental.pallas import tpu as pltpu
from jax.experimental.pallas import tpu_sc as plsc

D_MODEL = 1024
HEAD_DIM = 64
ROT_HALF = HEAD_DIM // 2
N_Q_HEADS = 8
N_KV_HEADS = 2
GROUP = N_Q_HEADS // N_KV_HEADS
ATTN_WIDTH = N_Q_HEADS * HEAD_DIM
KV_WIDTH = N_KV_HEADS * HEAD_DIM
WINDOW = 128
ROPE_THETA = 10000.0
HGRN_HEADS = 4
HGRN_DK = 128
HGRN_WIDTH = HGRN_HEADS * HGRN_DK
N_EXPERTS = 32
TOP_K = 4
D_EXPERT = 1024
SWIGLU_ALPHA = 1.702
SWIGLU_LIMIT = 7.0
MOE_BLOCK = 512
NORM_EPS = 1e-5

_OFF_Q = 0
_OFF_K = _OFF_Q + ATTN_WIDTH
_OFF_V = _OFF_K + KV_WIDTH
_OFF_HQ = _OFF_V + KV_WIDTH
_OFF_HF = _OFF_HQ + HGRN_WIDTH
_OFF_HI = _OFF_HF + HGRN_WIDTH
_OFF_HG = _OFF_HI + HGRN_WIDTH
_OFF_GA = _OFF_HG + HGRN_WIDTH
_OFF_GH = _OFF_GA + D_MODEL
IN_COLS = _OFF_GH + D_MODEL

LANES = 128
SUBLANES = 8
SEQ_TILE = 512
HGRN_CHUNK = 256
GATE_JOB_COLS = 256
COMBINE_TILE = 512
DEST_TILE = 8192
BATCH_PARTS = 2
WEIGHT_CAST_ROWS = 128
SC_CORES = 2
SC_WORKERS = 32
SC_CHUNK = 128
VMEM_LIMIT = 56 * 1024 * 1024

_BF = jnp.bfloat16
_F32 = jnp.float32


def _dot(a, b):
    return jnp.dot(a, b, preferred_element_type=_F32)


def _dot_nt(a, b):
    return lax.dot_general(a, b, (((1,), (1,)), ((), ())), preferred_element_type=_F32)


def _dot_tn(a, b):
    return lax.dot_general(a, b, (((0,), (0,)), ((), ())), preferred_element_type=_F32)


def _rowsum(x):
    return jnp.sum(x, axis=1, keepdims=True)


def _rms(x, g):
    ms = _rowsum(x * x) * (1.0 / x.shape[1])
    return x * lax.rsqrt(ms + NORM_EPS) * g


def _pack_bf16_pairs(x):
    n = x.shape[1] // 2
    lo = lax.bitcast_convert_type(x[:, :n].astype(_BF).astype(_F32), jnp.uint32)
    hi = lax.bitcast_convert_type(x[:, n:].astype(_BF).astype(_F32), jnp.uint32)
    return (lo >> 16) | (hi & jnp.uint32(0xFFFF0000))


def _unpack_bf16_pairs(u):
    lo = lax.bitcast_convert_type(u << 16, _F32)
    hi = lax.bitcast_convert_type(u & jnp.uint32(0xFFFF0000), _F32)
    return jnp.concatenate([lo, hi], axis=1)


def _mixer_kernel(sinks_ref, x_ref, cos_ref, sin_ref, gmix_ref, w_in_ref, b_in_ref, lb_ref,
                  hgn_ref, lvl_ref, w_ua_ref, w_uh_ref, w_out_ref, gffn_ref, w_rt_ref, b_r_ref,
                  h1_ref, xp_ref, idx_ref, rank_ref, gate_ref, cnt_ref,
                  kc_ref, vc_ref, st_ref, cnt_sc, ya_ref, o_ref):
    T = SEQ_TILE
    b = pl.program_id(0)
    s = pl.program_id(1)

    @pl.when(jnp.logical_and(b == 0, s == 0))
    def _():
        cnt_sc[...] = jnp.zeros_like(cnt_sc)

    @pl.when(s == 0)
    def _():
        kc_ref[...] = jnp.zeros_like(kc_ref)
        vc_ref[...] = jnp.zeros_like(vc_ref)
        st_ref[...] = jnp.zeros_like(st_ref)

    x = x_ref[0]
    xn = _rms(x, gmix_ref[...]).astype(_BF)

    def proj(off, width):
        return _dot(xn, w_in_ref[:, off:off + width]) + b_in_ref[:, off:off + width]

    cos = cos_ref[...]
    sin = sin_ref[...]

    lane = lax.broadcasted_iota(jnp.int32, (1, LANES), 1)
    first_half = (lane % HEAD_DIM) < ROT_HALF

    def rope(t):
        swapped = jnp.where(first_half, pltpu.roll(t, LANES - ROT_HALF, axis=1), pltpu.roll(t, ROT_HALF, axis=1))
        return t * cos + swapped * sin

    zq = proj(_OFF_Q, ATTN_WIDTH)
    scale = HEAD_DIM ** -0.5
    q_pairs = [(rope(zq[:, LANES * p:LANES * (p + 1)]) * scale).astype(_BF) for p in range(N_Q_HEADS // 2)]
    k_rot = rope(proj(_OFF_K, KV_WIDTH))
    v_new = proj(_OFF_V, KV_WIDTH)

    k_ext = jnp.concatenate([kc_ref[...], k_rot], axis=0)
    v_ext = jnp.concatenate([vc_ref[...], v_new], axis=0)
    kc_ref[...] = k_rot[T - WINDOW:, :]
    vc_ref[...] = v_new[T - WINDOW:, :]

    lane_lo = lane < HEAD_DIM
    k_r64 = pltpu.roll(k_ext, HEAD_DIM, axis=1)
    v_r64 = pltpu.roll(v_ext, HEAD_DIM, axis=1)
    zero = jnp.zeros_like(k_ext)
    k_var = [(jnp.where(lane_lo, k_ext, zero).astype(_BF), jnp.where(lane_lo, zero, k_r64).astype(_BF)),
             (jnp.where(lane_lo, k_r64, zero).astype(_BF), jnp.where(lane_lo, zero, k_ext).astype(_BF))]
    v_var = [(jnp.where(lane_lo, v_ext, zero).astype(_BF), jnp.where(lane_lo, zero, v_r64).astype(_BF)),
             (jnp.where(lane_lo, v_r64, zero).astype(_BF), jnp.where(lane_lo, zero, v_ext).astype(_BF))]

    qi = lax.broadcasted_iota(jnp.int32, (2 * WINDOW, 2 * WINDOW), 0) % WINDOW
    kj = lax.broadcasted_iota(jnp.int32, (2 * WINDOW, 2 * WINDOW), 1)
    band = jnp.logical_and(kj > qi, kj <= qi + WINDOW)
    row_top = lax.broadcasted_iota(jnp.int32, (2 * WINDOW, 1), 0) < WINDOW
    neg_inf = jnp.float32(-jnp.inf)

    gate_cols = [(_OFF_GA + c, GATE_JOB_COLS) for c in range(0, 2 * D_MODEL, GATE_JOB_COLS)]
    gate_parts = []
    jobs_per_unit = -(-len(gate_cols) // ((T // WINDOW) * N_KV_HEADS))

    for n in range(T // WINDOW):
        if n == 0:
            ok = jnp.logical_and(band, jnp.logical_or(kj >= WINDOW, s > 0))
        else:
            ok = band
        r0 = n * WINDOW
        for j in range(N_KV_HEADS):
            q2 = jnp.concatenate([q_pairs[2 * j][r0:r0 + WINDOW], q_pairs[2 * j + 1][r0:r0 + WINDOW]], axis=0)
            kcat = jnp.concatenate([k_var[j][0][r0:r0 + 2 * WINDOW], k_var[j][1][r0:r0 + 2 * WINDOW]], axis=0)
            vcat = jnp.concatenate([v_var[j][0][r0:r0 + 2 * WINDOW], v_var[j][1][r0:r0 + 2 * WINDOW]], axis=0)
            sc = _dot_nt(q2, kcat)
            ps, rs = [], []
            for half in range(2):
                sh = jnp.where(ok, sc[:, 2 * WINDOW * half:2 * WINDOW * (half + 1)], neg_inf)
                snk = jnp.where(row_top, sinks_ref[GROUP * j + half], sinks_ref[GROUP * j + 2 + half])
                m = jnp.maximum(jnp.max(sh, axis=1, keepdims=True), snk)
                p = jnp.exp(sh - m)
                den = _rowsum(p) + jnp.exp(snk - m)
                ps.append(p.astype(_BF))
                rs.append(1.0 / den)
            o = _dot(jnp.concatenate(ps, axis=1), vcat)
            o = o * jnp.where(lane_lo, rs[0], rs[1])
            ya_ref[r0:r0 + WINDOW, LANES * (2 * j):LANES * (2 * j + 1)] = o[:WINDOW]
            ya_ref[r0:r0 + WINDOW, LANES * (2 * j + 1):LANES * (2 * j + 2)] = o[WINDOW:]
            for _ in range(min(jobs_per_unit, len(gate_cols) - len(gate_parts))):
                gate_parts.append(proj(*gate_cols[len(gate_parts)]))

    up_a = _dot(ya_ref[...].astype(_BF), w_ua_ref[...])

    HW = HGRN_WIDTH
    a0 = lb_ref[0:1, :]
    a1 = lb_ref[1:2, :]
    am = jnp.maximum(a0, a1)
    e0 = jnp.exp(a0 - am)
    lb = e0 / (e0 + jnp.exp(a1 - am))
    f = lb + (1.0 - lb) * jax.nn.sigmoid(proj(_OFF_HF, HW))
    kk = 1.0 - f
    g = jnp.log2(f)
    hq = proj(_OFF_HQ, HW)
    hv = proj(_OFF_HI, HW)

    row = lax.broadcasted_iota(jnp.int32, (T, 1), 0)
    C = HGRN_CHUNK

    def rdown(t, d):
        return pltpu.roll(t.reshape(T // SUBLANES, SUBLANES, HW), d, axis=1).reshape(T, HW)

    def rup(t, d):
        return pltpu.roll(t.reshape(T // SUBLANES, SUBLANES, HW), SUBLANES - d, axis=1).reshape(T, HW)

    def head(t, h):
        return t[:, HGRN_DK * h:HGRN_DK * (h + 1)]

    qk0 = hq * kk
    o_parts = [_rowsum(head(qk0, h)) * head(hv, h) for h in range(HGRN_HEADS)]

    small_levels = []
    bsz = 1
    lb_cum, lb_tot = g, g
    while bsz < SUBLANES:
        odd = (row % (2 * bsz)) >= bsz
        ex = jnp.exp2(jnp.where(odd, lb_cum, lb_tot - lb_cum))
        small_levels.append(((hq * ex).astype(_BF), (kk * ex).astype(_BF)))
        prev_tot = rdown(lb_tot, bsz)
        lb_cum = lb_cum + jnp.where(odd, prev_tot, 0.0)
        lb_tot = lb_tot + jnp.where(odd, prev_tot, rup(lb_tot, bsz))
        bsz *= 2

    cum = [lb_cum[r:r + bsz] for r in range(0, T, bsz)]
    tot = [lb_tot[r:r + 1] for r in range(0, T, bsz)]
    kk_b = kk.astype(_BF)
    big_levels = []
    while bsz < C:
        q_rows, k_rows = [], []
        for j, r in enumerate(range(0, T, bsz)):
            if j % 2:
                q_rows.append(hq[r:r + bsz] * jnp.exp2(cum[j]))
                k_rows.append(kk_b[r:r + bsz])
            else:
                k_rows.append((kk[r:r + bsz] * jnp.exp2(tot[j] - cum[j])).astype(_BF))
        big_levels.append((bsz, jnp.concatenate(q_rows, axis=0).astype(_BF), jnp.concatenate(k_rows, axis=0)))
        cum = [jnp.concatenate([cum[j], cum[j + 1] + tot[j]], axis=0) for j in range(0, len(cum), 2)]
        tot = [tot[j] + tot[j + 1] for j in range(0, len(tot), 2)]
        bsz *= 2
    hv_b = hv.astype(_BF)
    lvl = lvl_ref[...]

    for c in range(T // C):
        rs_ = slice(c * C, (c + 1) * C)
        rh_ = slice(c * C // 2, (c + 1) * C // 2)
        q_in = (hq[rs_] * jnp.exp2(cum[c])).astype(_BF)
        k_st = (kk[rs_] * jnp.exp2(tot[c] - cum[c])).astype(_BF)
        dec = jnp.exp2(tot[c])
        for h in range(HGRN_HEADS):
            cs_ = slice(HGRN_DK * h, HGRN_DK * (h + 1))
            st = st_ref[h]
            vc = hv_b[rs_, cs_]
            inter = _dot_nt(q_in[:, cs_], st.astype(_BF))
            zero_tile = jnp.zeros((SUBLANES, LANES), _F32)
            tiles = [[zero_tile for _ in range(C // LANES)] for _ in range(C // SUBLANES)]

            def place(li, p, p_row, rb, col):
                r, ct = rb * SUBLANES, col // LANES
                cs = slice(ct * LANES, (ct + 1) * LANES)
                tiles[rb][ct] = jnp.where(lvl[r:r + SUBLANES, cs] == li, p[p_row:p_row + SUBLANES, cs], tiles[rb][ct])

            for li, (ql, kl) in enumerate(small_levels):
                p = _dot_nt(ql[rs_, cs_], kl[rs_, cs_])
                for rb in range(C // SUBLANES):
                    place(li, p, rb * SUBLANES, rb, rb * SUBLANES)
            for li, (bs, ql, kl) in enumerate(big_levels, start=len(small_levels)):
                p = _dot_nt(ql[rh_, cs_], kl[rs_, cs_])
                for rb in range(C // SUBLANES):
                    blk = (rb * SUBLANES) // bs
                    if blk % 2:
                        place(li, p, rb * SUBLANES - (blk + 1) // 2 * bs, rb, (blk - 1) * bs)
            amat = jnp.concatenate([jnp.concatenate(row_tiles, axis=1) for row_tiles in tiles], axis=0)
            intra = _dot(amat.astype(_BF), vc)
            st_ref[h] = st * dec[:, cs_] + _dot_tn(vc, k_st[:, cs_])
            o_ref[rs_, cs_] = inter + intra + o_parts[h][rs_]

    o = o_ref[...]
    hg = proj(_OFF_HG, HW)
    yh_parts = []
    for h in range(HGRN_HEADS):
        oh = head(o, h)
        ms = _rowsum(oh * oh) * (1.0 / HGRN_DK)
        yh_parts.append(oh * lax.rsqrt(ms + NORM_EPS))
    yh = jnp.concatenate(yh_parts, axis=1) * hgn_ref[...]
    yh = (yh * (hg * jax.nn.sigmoid(hg))).astype(_BF)

    up_h = _dot(yh, w_uh_ref[...])
    half = len(gate_parts) // 2
    z_ga = jnp.concatenate(gate_parts[:half], axis=1)
    z_gh = jnp.concatenate(gate_parts[half:], axis=1)
    merged = jax.nn.sigmoid(z_ga) * up_a + jax.nn.sigmoid(z_gh) * up_h
    h1 = x + _dot(merged.astype(_BF), w_out_ref[...])
    h1_ref[0] = h1

    xn2 = _rms(h1, gffn_ref[...])
    xp_ref[0] = _pack_bf16_pairs(xn2)
    x_hi = xn2.astype(_BF)
    x_lo = (xn2 - x_hi.astype(_F32)).astype(_BF)
    w_rt = w_rt_ref[...]
    w_hi = w_rt.astype(_BF)
    w_lo = (w_rt - w_hi.astype(_F32)).astype(_BF)
    logits = (_dot_nt(w_hi, x_hi) + (_dot_nt(w_hi, x_lo) + _dot_nt(w_lo, x_hi))) + b_r_ref[...]
    eidx = lax.broadcasted_iota(jnp.int32, (N_EXPERTS, T), 0)
    vals, sels, ohs = [], [], []
    l = logits
    for _ in range(TOP_K):
        m = jnp.max(l, axis=0, keepdims=True)
        sel = jnp.min(jnp.where(l == m, eidx, N_EXPERTS), axis=0, keepdims=True)
        oh = eidx == sel
        vals.append(m)
        sels.append(sel)
        ohs.append(oh)
        l = jnp.where(oh, neg_inf, l)
    es = [jnp.exp(v - vals[0]) for v in vals]
    den = es[0] + es[1] + es[2] + es[3]
    chosen = jnp.zeros((N_EXPERTS, T), _F32)
    for oh in ohs:
        chosen = chosen + jnp.where(oh, 1.0, 0.0)
    ui = lax.broadcasted_iota(jnp.int32, (T, T), 0)
    uj = lax.broadcasted_iota(jnp.int32, (T, T), 1)
    upper = jnp.where(ui < uj, 1.0, 0.0).astype(_BF)
    before = _dot(chosen.astype(_BF), upper) + cnt_sc[:, 0:1]
    for k in range(TOP_K):
        idx_ref[k:k + 1, :] = sels[k]
        gate_ref[k:k + 1, :] = es[k] / den
        rank_ref[k:k + 1, :] = jnp.sum(jnp.where(ohs[k], before, 0.0), axis=0, keepdims=True).astype(jnp.int32)
    cnt_sc[...] = cnt_sc[...] + _rowsum(chosen)
    cnt_ref[...] = cnt_sc[...]


def _const_spec(shape):
    return pl.BlockSpec(shape, lambda b, s: (0,) * len(shape), pipeline_mode=pl.Buffered(1))


def _mixer_call(batch0, B, x, sinks, cos_t, sin_t, gmix, w_in, b_in, lb, hgn, lvl, w_ua, w_uh, w_out, gffn, w_rt, b_r):
    S = x.shape[1]
    T = SEQ_TILE
    N = B * S
    nS = S // T
    tok_spec = pl.BlockSpec((TOP_K, T), lambda b, s: (0, b * nS + s))
    out_shape = (
        jax.ShapeDtypeStruct((B, S, D_MODEL), _F32),
        jax.ShapeDtypeStruct((B, S, D_MODEL // 2), jnp.uint32),
        jax.ShapeDtypeStruct((TOP_K, N), jnp.int32),
        jax.ShapeDtypeStruct((TOP_K, N), jnp.int32),
        jax.ShapeDtypeStruct((TOP_K, N), _F32),
        jax.ShapeDtypeStruct((N_EXPERTS, LANES), _F32),
    )
    in_specs = [
        pl.BlockSpec(memory_space=pltpu.SMEM),
        pl.BlockSpec((1, T, D_MODEL), lambda b, s: (batch0 + b, s, 0)),
        pl.BlockSpec((T, LANES), lambda b, s: (s, 0)),
        pl.BlockSpec((T, LANES), lambda b, s: (s, 0)),
        _const_spec((1, D_MODEL)),
        _const_spec((D_MODEL, IN_COLS)),
        _const_spec((1, IN_COLS)),
        _const_spec((2, HGRN_WIDTH)),
        _const_spec((1, HGRN_WIDTH)),
        _const_spec((HGRN_CHUNK, HGRN_CHUNK)),
        _const_spec((ATTN_WIDTH, D_MODEL)),
        _const_spec((HGRN_WIDTH, D_MODEL)),
        _const_spec((D_MODEL, D_MODEL)),
        _const_spec((1, D_MODEL)),
        _const_spec((N_EXPERTS, D_MODEL)),
        _const_spec((N_EXPERTS, 1)),
    ]
    out_specs = (
        pl.BlockSpec((1, T, D_MODEL), lambda b, s: (b, s, 0)),
        pl.BlockSpec((1, T, D_MODEL // 2), lambda b, s: (b, s, 0)),
        tok_spec, tok_spec, tok_spec,
        pl.BlockSpec((N_EXPERTS, LANES), lambda b, s: (0, 0)),
    )
    scratch = [
        pltpu.VMEM((WINDOW, KV_WIDTH), _F32),
        pltpu.VMEM((WINDOW, KV_WIDTH), _F32),
        pltpu.VMEM((HGRN_HEADS, HGRN_DK, HGRN_DK), _F32),
        pltpu.VMEM((N_EXPERTS, LANES), _F32),
        pltpu.VMEM((T, ATTN_WIDTH), _F32),
        pltpu.VMEM((T, HGRN_WIDTH), _F32),
    ]
    return pl.pallas_call(
        _mixer_kernel,
        grid=(B, nS),
        in_specs=in_specs,
        out_specs=out_specs,
        out_shape=out_shape,
        scratch_shapes=scratch,
        compiler_params=pltpu.CompilerParams(
            dimension_semantics=("arbitrary", "arbitrary"), vmem_limit_bytes=VMEM_LIMIT),
        name="mixer",
    )(sinks, x, cos_t, sin_t, gmix, w_in, b_in, lb, hgn, lvl, w_ua, w_uh, w_out, gffn, w_rt, b_r)


def _dest_kernel(start_ref, idx_ref, rank_ref, dest_ref):
    idx = idx_ref[...]
    dest = rank_ref[...]
    for e in range(N_EXPERTS):
        dest = dest + jnp.where(idx == e, start_ref[e], 0)
    dest_ref[...] = dest


def _dest_call(pad_starts, idx, rank):
    K, N = idx.shape
    spec = pl.BlockSpec((K, DEST_TILE), lambda i: (0, i))
    return pl.pallas_call(
        _dest_kernel,
        grid=(N // DEST_TILE,),
        in_specs=[pl.BlockSpec(memory_space=pltpu.SMEM), spec, spec],
        out_specs=spec,
        out_shape=jax.ShapeDtypeStruct((K, N), jnp.int32),
        compiler_params=pltpu.CompilerParams(dimension_semantics=("arbitrary",)),
        name="dest",
    )(pad_starts, idx, rank)


def _sc_mesh():
    return plsc.VectorSubcoreMesh(core_axis_name="c", subcore_axis_name="s")


def _sc_worker_id():
    return lax.axis_index("s") * SC_CORES + lax.axis_index("c")


def _sc_scatter_rows(x, dest, n_rows):
    N, C = x.shape
    K = dest.shape[0]
    n_chunks = N // SC_CHUNK
    per_worker = n_chunks // SC_WORKERS
    dest3 = dest.reshape(K, n_chunks, SC_CHUNK)

    @functools.partial(
        pl.kernel, out_type=jax.ShapeDtypeStruct((n_rows, C), x.dtype), mesh=_sc_mesh(),
        scratch_types=[pltpu.VMEM((K, SC_CHUNK), jnp.int32), pltpu.VMEM((SC_CHUNK, C), x.dtype),
                       pltpu.SemaphoreType.DMA])
    def scatter(x_hbm, i_hbm, o_hbm, idx_v, rows_v, sem):
        base = _sc_worker_id() * per_worker

        @pl.loop(0, per_worker)
        def _(j):
            c = base + j
            off = pl.multiple_of(c * SC_CHUNK, SC_CHUNK)
            for k in range(K):
                pltpu.sync_copy(i_hbm.at[k, c], idx_v.at[k])
            pltpu.sync_copy(x_hbm.at[pl.ds(off, SC_CHUNK)], rows_v)
            for k in range(K):
                pltpu.async_copy(rows_v, o_hbm.at[idx_v.at[k]], sem).wait()

    return scatter(x, dest3)


def _sc_gather_rows(table, idx):
    M = idx.shape[0]
    C = table.shape[1]
    per_worker = M // SC_WORKERS
    n_chunks = per_worker // SC_CHUNK

    @functools.partial(
        pl.kernel, out_type=jax.ShapeDtypeStruct((M, C), table.dtype), mesh=_sc_mesh(),
        scratch_types=[pltpu.VMEM((SC_CHUNK,), jnp.int32), pltpu.VMEM((SC_CHUNK, C), table.dtype),
                       pltpu.SemaphoreType.DMA])
    def gather(t_hbm, i_hbm, o_hbm, idx_v, rows_v, sem):
        base = _sc_worker_id() * per_worker

        @pl.loop(0, n_chunks)
        def _(j):
            off = pl.multiple_of(base + j * SC_CHUNK, SC_CHUNK)
            pltpu.sync_copy(i_hbm.at[pl.ds(off, SC_CHUNK)], idx_v)
            pltpu.async_copy(t_hbm.at[idx_v], rows_v, sem).wait()
            pltpu.sync_copy(rows_v, o_hbm.at[pl.ds(off, SC_CHUNK)])

    return gather(table, idx)


def _expert_kernel(blk_e_ref, nb_ref, next_ref, xs_ref, w1_hbm, b1_ref, w2_hbm, b2_ref, ys_ref,
                   w1f_ref, w2f_ref, w1b_ref, w2b_ref, sems):
    i = pl.program_id(0)
    active = i < nb_ref[0]
    e = blk_e_ref[i]
    new_expert = jnp.logical_or(i == 0, e != blk_e_ref[jnp.maximum(i - 1, 0)])

    def weight_copies(expert):
        return (pltpu.make_async_copy(w1_hbm.at[expert], w1f_ref, sems.at[0]),
                pltpu.make_async_copy(w2_hbm.at[expert], w2f_ref, sems.at[1]))

    @pl.when(i == 0)
    def _():
        for c in weight_copies(e):
            c.start()

    @pl.when(jnp.logical_and(active, new_expert))
    def _():
        for c in weight_copies(e):
            c.wait()
        for r in range(0, D_MODEL, WEIGHT_CAST_ROWS):
            w1b_ref[r:r + WEIGHT_CAST_ROWS, :] = w1f_ref[r:r + WEIGHT_CAST_ROWS, :].astype(_BF)
        for r in range(0, D_EXPERT, WEIGHT_CAST_ROWS):
            w2b_ref[r:r + WEIGHT_CAST_ROWS, :] = w2f_ref[r:r + WEIGHT_CAST_ROWS, :].astype(_BF)
        nxt = next_ref[e]

        @pl.when(nxt >= 0)
        def _():
            for c in weight_copies(nxt):
                c.start()

    @pl.when(active)
    def _():
        xb = _unpack_bf16_pairs(xs_ref[...]).astype(_BF)
        h = _dot(xb, w1b_ref[...]) + b1_ref[0]
        glu = jnp.minimum(h[:, :D_EXPERT], SWIGLU_LIMIT)
        lin = jnp.clip(h[:, D_EXPERT:], -SWIGLU_LIMIT, SWIGLU_LIMIT)
        act = glu * jax.nn.sigmoid(SWIGLU_ALPHA * glu) * (lin + 1.0)
        y = _dot(act.astype(_BF), w2b_ref[...]) + b2_ref[0]
        ys_ref[...] = _pack_bf16_pairs(y)

    @pl.when(jnp.logical_not(active))
    def _():
        ys_ref[...] = jnp.zeros_like(ys_ref)


def _expert_call(blk_e, nb_used, next_expert, xs, w1, b1, w2, b2):
    n_blocks = xs.shape[0] // MOE_BLOCK
    M = MOE_BLOCK
    grid_spec = pltpu.PrefetchScalarGridSpec(
        num_scalar_prefetch=3,
        grid=(n_blocks,),
        in_specs=[
            pl.BlockSpec((M, D_MODEL // 2), lambda i, be, nb, nx: (jnp.minimum(i, nb[0] - 1), 0)),
            pl.BlockSpec(memory_space=pl.ANY),
            pl.BlockSpec((1, 1, 2 * D_EXPERT), lambda i, be, nb, nx: (be[i], 0, 0)),
            pl.BlockSpec(memory_space=pl.ANY),
            pl.BlockSpec((1, 1, D_MODEL), lambda i, be, nb, nx: (be[i], 0, 0)),
        ],
        out_specs=pl.BlockSpec((M, D_MODEL // 2), lambda i, be, nb, nx: (i, 0)),
        scratch_shapes=[pltpu.VMEM((D_MODEL, 2 * D_EXPERT), _F32), pltpu.VMEM((D_EXPERT, D_MODEL), _F32),
                        pltpu.VMEM((D_MODEL, 2 * D_EXPERT), _BF), pltpu.VMEM((D_EXPERT, D_MODEL), _BF),
                        pltpu.SemaphoreType.DMA((2,))],
    )
    return pl.pallas_call(
        _expert_kernel,
        grid_spec=grid_spec,
        out_shape=jax.ShapeDtypeStruct(xs.shape, jnp.uint32),
        compiler_params=pltpu.CompilerParams(
            dimension_semantics=("arbitrary",), vmem_limit_bytes=VMEM_LIMIT),
        name="experts",
    )(blk_e, nb_used, next_expert, xs, w1, b1, w2, b2)


def _combine_kernel(h1_ref, gate_ref, gfin_ref, y0_ref, y1_ref, y2_ref, y3_ref, *out_refs):
    out_ref = out_refs[-1]
    acc = h1_ref[...]
    g = gate_ref[...]
    gates = jnp.concatenate([g, jnp.zeros_like(g)], axis=0).T
    for k, y_ref in enumerate((y0_ref, y1_ref, y2_ref, y3_ref)):
        acc = acc + gates[:, k:k + 1] * _unpack_bf16_pairs(y_ref[...])
    out_ref[...] = _rms(acc, gfin_ref[...])


def _combine_call(h1, gates, gfin, yg, out_prev, tile0, n_total):
    N = h1.shape[0]
    T = COMBINE_TILE
    nT = N // T
    y_specs = [pl.BlockSpec((T, D_MODEL // 2), functools.partial(lambda i, k: (k * nT + i, 0), k=k))
               for k in range(TOP_K)]
    in_specs = [
        pl.BlockSpec((T, D_MODEL), lambda i: (i, 0)),
        pl.BlockSpec((TOP_K, T), lambda i: (0, i)),
        pl.BlockSpec((1, D_MODEL), lambda i: (0, 0)),
    ] + y_specs
    args = [h1, gates, gfin, yg, yg, yg, yg]
    aliases = {}
    if out_prev is not None:
        in_specs.append(pl.BlockSpec(memory_space=pl.ANY))
        args.append(out_prev)
        aliases = {len(args) - 1: 0}
    return pl.pallas_call(
        _combine_kernel,
        grid=(nT,),
        in_specs=in_specs,
        out_specs=pl.BlockSpec((T, D_MODEL), lambda i: (tile0 + i, 0)),
        out_shape=jax.ShapeDtypeStruct((n_total, D_MODEL), _F32),
        input_output_aliases=aliases,
        compiler_params=pltpu.CompilerParams(dimension_semantics=("arbitrary",)),
        name="combine",
    )(*args)


def _hgrn_level_table():
    t = np.arange(HGRN_CHUNK)[:, None]
    u = np.arange(HGRN_CHUNK)[None, :]
    top = np.floor(np.log2(np.maximum(t ^ u, 1))).astype(np.int32)
    return np.where(t > u, top, -1).astype(np.int32)


def _rope_tables(S):
    half = HEAD_DIM // 2
    inv_freq = ROPE_THETA ** (-(jnp.arange(half, dtype=_F32) * 2.0 / HEAD_DIM))
    ang = jnp.arange(S, dtype=_F32)[:, None] * inv_freq[None, :]
    cos = jnp.tile(jnp.cos(ang), (1, 4))
    sin = jnp.tile(jnp.sin(ang), (1, 4))
    sign = jnp.where(jnp.arange(LANES) % HEAD_DIM < ROT_HALF, -1.0, 1.0).astype(_F32)
    return cos, sin * sign[None, :]


def kernel(x, norm_mix_g, w_in, b_in, attn_sinks, hgrn_lb, hgrn_norm_g, w_up_attn, w_up_hgrn, w_out,
           norm_ffn_g, w_router, b_router, w_moe1, b_moe1, w_moe2, b_moe2, norm_final_g):
    B, S, D = x.shape
    N = B * S
    assert D == D_MODEL and S % SEQ_TILE == 0 and SEQ_TILE % WINDOW == 0
    assert B % BATCH_PARTS == 0 and (N // BATCH_PARTS) % (SC_WORKERS * SC_CHUNK) == 0
    assert (N // BATCH_PARTS) % COMBINE_TILE == 0 and (N // BATCH_PARTS) % DEST_TILE == 0
    assert norm_mix_g.shape[0] == 1 and hgrn_lb.shape[0] == 2
    assert GROUP == 4 and N_KV_HEADS == 2 and 2 * HEAD_DIM == LANES

    cos_t, sin_t = _rope_tables(S)
    weights = (attn_sinks[0], cos_t, sin_t, norm_mix_g,
               w_in[0].astype(_BF), b_in, hgrn_lb, hgrn_norm_g, jnp.asarray(_hgrn_level_table()),
               w_up_attn[0].astype(_BF), w_up_hgrn[0].astype(_BF), w_out[0].astype(_BF),
               norm_ffn_g, w_router[0].T, b_router[0][:, None])

    Bp = B // BATCH_PARTS
    Np = Bp * S
    n_blocks = (Np * TOP_K) // MOE_BLOCK + N_EXPERTS
    out = None
    for part in range(BATCH_PARTS):
        h1, xp, idx, rank, gates, cnt = _mixer_call(part * Bp, Bp, x, *weights)

        counts = cnt[:, 0].astype(jnp.int32)
        padded = ((counts + MOE_BLOCK - 1) // MOE_BLOCK) * MOE_BLOCK
        pad_ends = jnp.cumsum(padded)
        pad_starts = pad_ends - padded
        dest = _dest_call(pad_starts, idx, rank)
        blk_start = jnp.arange(n_blocks, dtype=jnp.int32) * MOE_BLOCK
        blk_e = jnp.minimum(jnp.sum((pad_ends[None, :] <= blk_start[:, None]).astype(jnp.int32), axis=1),
                            N_EXPERTS - 1)
        nb_used = (pad_ends[-1:] // MOE_BLOCK).astype(jnp.int32)
        e_ids = jnp.arange(N_EXPERTS, dtype=jnp.int32)
        later = jnp.logical_and(e_ids[None, :] > e_ids[:, None], counts[None, :] > 0)
        next_expert = jnp.min(jnp.where(later, e_ids[None, :], N_EXPERTS), axis=1)
        next_expert = jnp.where(next_expert < N_EXPERTS, next_expert, -1).astype(jnp.int32)

        xs = _sc_scatter_rows(xp.reshape(Np, D // 2), dest, n_blocks * MOE_BLOCK)
        ys = _expert_call(blk_e, nb_used, next_expert, xs, w_moe1[0], b_moe1[0][:, None, :], w_moe2[0], b_moe2[0][:, None, :])
        yg = _sc_gather_rows(ys, dest.reshape(TOP_K * Np))
        out = _combine_call(h1.reshape(Np, D), gates, norm_final_g[None, :], yg, out,
                            part * (Np // COMBINE_TILE), N)
    return out.reshape(B, S, D)
```

```python
import functools

import numpy as np
import jax
import jax.numpy as jnp
from jax import lax
from jax.experimental import pallas as pl
from jax.experimental.pallas import tpu as pltpu
from jax.experimental.pallas import tpu_sc as plsc

D_MODEL = 1024
HEAD_DIM = 64
ROT_HALF = HEAD_DIM // 2
N_Q_HEADS = 8
N_KV_HEADS = 2
GROUP = N_Q_HEADS // N_KV_HEADS
ATTN_WIDTH = N_Q_HEADS * HEAD_DIM
KV_WIDTH = N_KV_HEADS * HEAD_DIM
WINDOW = 128
ROPE_THETA = 10000.0
HGRN_HEADS = 4
HGRN_DK = 128
HGRN_WIDTH = HGRN_HEADS * HGRN_DK
N_EXPERTS = 32
TOP_K = 4
D_EXPERT = 1024
SWIGLU_ALPHA = 1.702
SWIGLU_LIMIT = 7.0
MOE_BLOCK = 512
NORM_EPS = 1e-5

_OFF_Q = 0
_OFF_K = _OFF_Q + ATTN_WIDTH
_OFF_V = _OFF_K + KV_WIDTH
_OFF_HQ = _OFF_V + KV_WIDTH
_OFF_HF = _OFF_HQ + HGRN_WIDTH
_OFF_HI = _OFF_HF + HGRN_WIDTH
_OFF_HG = _OFF_HI + HGRN_WIDTH
_OFF_GA = _OFF_HG + HGRN_WIDTH
_OFF_GH = _OFF_GA + D_MODEL
IN_COLS = _OFF_GH + D_MODEL

LANES = 128
SUBLANES = 8
SEQ_TILE = 512
HGRN_CHUNK = 256
GATE_JOB_COLS = 256
COMBINE_TILE = 512
DEST_TILE = 8192
BATCH_PARTS = 2
WEIGHT_CAST_ROWS = 128
SC_CORES = 2
SC_WORKERS = 32
SC_CHUNK = 128
VMEM_LIMIT = 56 * 1024 * 1024

_BF = jnp.bfloat16
_F32 = jnp.float32


def _dot(a, b):
    return jnp.dot(a, b, preferred_element_type=_F32)


def _dot_nt(a, b):
    return lax.dot_general(a, b, (((1,), (1,)), ((), ())), preferred_element_type=_F32)


def _dot_tn(a, b):
    return lax.dot_general(a, b, (((0,), (0,)), ((), ())), preferred_element_type=_F32)


def _rowsum(x):
    return jnp.sum(x, axis=1, keepdims=True)


def _rms(x, g):
    ms = _rowsum(x * x) * (1.0 / x.shape[1])
    return x * lax.rsqrt(ms + NORM_EPS) * g


def _pack_bf16_pairs(x):
    n = x.shape[1] // 2
    lo = lax.bitcast_convert_type(x[:, :n].astype(_BF).astype(_F32), jnp.uint32)
    hi = lax.bitcast_convert_type(x[:, n:].astype(_BF).astype(_F32), jnp.uint32)
    return (lo >> 16) | (hi & jnp.uint32(0xFFFF0000))


def _unpack_bf16_pairs(u):
    lo = lax.bitcast_convert_type(u << 16, _F32)
    hi = lax.bitcast_convert_type(u & jnp.uint32(0xFFFF0000), _F32)
    return jnp.concatenate([lo, hi], axis=1)


def _mixer_kernel(sinks_ref, x_ref, cos_ref, sin_ref, gmix_ref, w_in_ref, b_in_ref, lb_ref,
                  hgn_ref, lvl_ref, w_ua_ref, w_uh_ref, w_out_ref, gffn_ref, w_rt_ref, b_r_ref,
                  h1_ref, xp_ref, idx_ref, rank_ref, gate_ref, cnt_ref,
                  kc_ref, vc_ref, st_ref, cnt_sc, ya_ref, o_ref):
    T = SEQ_TILE
    b = pl.program_id(0)
    s = pl.program_id(1)

    @pl.when(jnp.logical_and(b == 0, s == 0))
    def _():
        cnt_sc[...] = jnp.zeros_like(cnt_sc)

    @pl.when(s == 0)
    def _():
        kc_ref[...] = jnp.zeros_like(kc_ref)
        vc_ref[...] = jnp.zeros_like(vc_ref)
        st_ref[...] = jnp.zeros_like(st_ref)

    x = x_ref[0]
    xn = _rms(x, gmix_ref[...]).astype(_BF)

    def proj(off, width):
        return _dot(xn, w_in_ref[:, off:off + width]) + b_in_ref[:, off:off + width]

    cos = cos_ref[...]
    sin = sin_ref[...]

    lane = lax.broadcasted_iota(jnp.int32, (1, LANES), 1)
    first_half = (lane % HEAD_DIM) < ROT_HALF

    def rope(t):
        swapped = jnp.where(first_half, pltpu.roll(t, LANES - ROT_HALF, axis=1), pltpu.roll(t, ROT_HALF, axis=1))
        return t * cos + swapped * sin

    zq = proj(_OFF_Q, ATTN_WIDTH)
    scale = HEAD_DIM ** -0.5
    q_pairs = [(rope(zq[:, LANES * p:LANES * (p + 1)]) * scale).astype(_BF) for p in range(N_Q_HEADS // 2)]
    k_rot = rope(proj(_OFF_K, KV_WIDTH))
    v_new = proj(_OFF_V, KV_WIDTH)

    k_ext = jnp.concatenate([kc_ref[...], k_rot], axis=0)
    v_ext = jnp.concatenate([vc_ref[...], v_new], axis=0)
    kc_ref[...] = k_rot[T - WINDOW:, :]
    vc_ref[...] = v_new[T - WINDOW:, :]

    lane_lo = lane < HEAD_DIM
    k_r64 = pltpu.roll(k_ext, HEAD_DIM, axis=1)
    v_r64 = pltpu.roll(v_ext, HEAD_DIM, axis=1)
    zero = jnp.zeros_like(k_ext)
    k_var = [(jnp.where(lane_lo, k_ext, zero).astype(_BF), jnp.where(lane_lo, zero, k_r64).astype(_BF)),
             (jnp.where(lane_lo, k_r64, zero).astype(_BF), jnp.where(lane_lo, zero, k_ext).astype(_BF))]
    v_var = [(jnp.where(lane_lo, v_ext, zero).astype(_BF), jnp.where(lane_lo, zero, v_r64).astype(_BF)),
             (jnp.where(lane_lo, v_r64, zero).astype(_BF), jnp.where(lane_lo, zero, v_ext).astype(_BF))]

    qi = lax.broadcasted_iota(jnp.int32, (2 * WINDOW, 2 * WINDOW), 0) % WINDOW
    kj = lax.broadcasted_iota(jnp.int32, (2 * WINDOW, 2 * WINDOW), 1)
    band = jnp.logical_and(kj > qi, kj <= qi + WINDOW)
    row_top = lax.broadcasted_iota(jnp.int32, (2 * WINDOW, 1), 0) < WINDOW
    neg_inf = jnp.float32(-jnp.inf)

    gate_cols = [(_OFF_GA + c, GATE_JOB_COLS) for c in range(0, 2 * D_MODEL, GATE_JOB_COLS)]
    gate_parts = []
    jobs_per_unit = -(-len(gate_cols) // ((T // WINDOW) * N_KV_HEADS))

    for n in range(T // WINDOW):
        if n == 0:
            ok = jnp.logical_and(band, jnp.logical_or(kj >= WINDOW, s > 0))
        else:
            ok = band
        r0 = n * WINDOW
        for j in range(N_KV_HEADS):
            q2 = jnp.concatenate([q_pairs[2 * j][r0:r0 + WINDOW], q_pairs[2 * j + 1][r0:r0 + WINDOW]], axis=0)
            kcat = jnp.concatenate([k_var[j][0][r0:r0 + 2 * WINDOW], k_var[j][1][r0:r0 + 2 * WINDOW]], axis=0)
            vcat = jnp.concatenate([v_var[j][0][r0:r0 + 2 * WINDOW], v_var[j][1][r0:r0 + 2 * WINDOW]], axis=0)
            sc = _dot_nt(q2, kcat)
            ps, rs = [], []
            for half in range(2):
                sh = jnp.where(ok, sc[:, 2 * WINDOW * half:2 * WINDOW * (half + 1)], neg_inf)
                snk = jnp.where(row_top, sinks_ref[GROUP * j + half], sinks_ref[GROUP * j + 2 + half])
                m = jnp.maximum(jnp.max(sh, axis=1, keepdims=True), snk)
                p = jnp.exp(sh - m)
                den = _rowsum(p) + jnp.exp(snk - m)
                ps.append(p.astype(_BF))
                rs.append(1.0 / den)
            o = _dot(jnp.concatenate(ps, axis=1), vcat)
            o = o * jnp.where(lane_lo, rs[0], rs[1])
            ya_ref[r0:r0 + WINDOW, LANES * (2 * j):LANES * (2 * j + 1)] = o[:WINDOW]
            ya_ref[r0:r0 + WINDOW, LANES * (2 * j + 1):LANES * (2 * j + 2)] = o[WINDOW:]
            for _ in range(min(jobs_per_unit, len(gate_cols) - len(gate_parts))):
                gate_parts.append(proj(*gate_cols[len(gate_parts)]))

    up_a = _dot(ya_ref[...].astype(_BF), w_ua_ref[...])

    HW = HGRN_WIDTH
    a0 = lb_ref[0:1, :]
    a1 = lb_ref[1:2, :]
    am = jnp.maximum(a0, a1)
    e0 = jnp.exp(a0 - am)
    lb = e0 / (e0 + jnp.exp(a1 - am))
    f = lb + (1.0 - lb) * jax.nn.sigmoid(proj(_OFF_HF, HW))
    kk = 1.0 - f
    g = jnp.log2(f)
    hq = proj(_OFF_HQ, HW)
    hv = proj(_OFF_HI, HW)

    row = lax.broadcasted_iota(jnp.int32, (T, 1), 0)
    C = HGRN_CHUNK

    def rdown(t, d):
        return pltpu.roll(t.reshape(T // SUBLANES, SUBLANES, HW), d, axis=1).reshape(T, HW)

    def rup(t, d):
        return pltpu.roll(t.reshape(T // SUBLANES, SUBLANES, HW), SUBLANES - d, axis=1).reshape(T, HW)

    def head(t, h):
        return t[:, HGRN_DK * h:HGRN_DK * (h + 1)]

    qk0 = hq * kk
    o_parts = [_rowsum(head(qk0, h)) * head(hv, h) for h in range(HGRN_HEADS)]

    small_levels = []
    bsz = 1
    lb_cum, lb_tot = g, g
    while bsz < SUBLANES:
        odd = (row % (2 * bsz)) >= bsz
        ex = jnp.exp2(jnp.where(odd, lb_cum, lb_tot - lb_cum))
        small_levels.append(((hq * ex).astype(_BF), (kk * ex).astype(_BF)))
        prev_tot = rdown(lb_tot, bsz)
        lb_cum = lb_cum + jnp.where(odd, prev_tot, 0.0)
        lb_tot = lb_tot + jnp.where(odd, prev_tot, rup(lb_tot, bsz))
        bsz *= 2

    cum = [lb_cum[r:r + bsz] for r in range(0, T, bsz)]
    tot = [lb_tot[r:r + 1] for r in range(0, T, bsz)]
    kk_b = kk.astype(_BF)
    big_levels = []
    while bsz < C:
        q_rows, k_rows = [], []
        for j, r in enumerate(range(0, T, bsz)):
            if j % 2:
                q_rows.append(hq[r:r + bsz] * jnp.exp2(cum[j]))
                k_rows.append(kk_b[r:r + bsz])
            else:
                k_rows.append((kk[r:r + bsz] * jnp.exp2(tot[j] - cum[j])).astype(_BF))
        big_levels.append((bsz, jnp.concatenate(q_rows, axis=0).astype(_BF), jnp.concatenate(k_rows, axis=0)))
        cum = [jnp.concatenate([cum[j], cum[j + 1] + tot[j]], axis=0) for j in range(0, len(cum), 2)]
        tot = [tot[j] + tot[j + 1] for j in range(0, len(tot), 2)]
        bsz *= 2
    hv_b = hv.astype(_BF)
    lvl = lvl_ref[...]

    for c in range(T // C):
        rs_ = slice(c * C, (c + 1) * C)
        rh_ = slice(c * C // 2, (c + 1) * C // 2)
        q_in = (hq[rs_] * jnp.exp2(cum[c])).astype(_BF)
        k_st = (kk[rs_] * jnp.exp2(tot[c] - cum[c])).astype(_BF)
        dec = jnp.exp2(tot[c])
        for h in range(HGRN_HEADS):
            cs_ = slice(HGRN_DK * h, HGRN_DK * (h + 1))
            st = st_ref[h]
            vc = hv_b[rs_, cs_]
            inter = _dot_nt(q_in[:, cs_], st.astype(_BF))
            zero_tile = jnp.zeros((SUBLANES, LANES), _F32)
            tiles = [[zero_tile for _ in range(C // LANES)] for _ in range(C // SUBLANES)]

            def place(li, p, p_row, rb, col):
                r, ct = rb * SUBLANES, col // LANES
                cs = slice(ct * LANES, (ct + 1) * LANES)
                tiles[rb][ct] = jnp.where(lvl[r:r + SUBLANES, cs] == li, p[p_row:p_row + SUBLANES, cs], tiles[rb][ct])

            for li, (ql, kl) in enumerate(small_levels):
                p = _dot_nt(ql[rs_, cs_], kl[rs_, cs_])
                for rb in range(C // SUBLANES):
                    place(li, p, rb * SUBLANES, rb, rb * SUBLANES)
            for li, (bs, ql, kl) in enumerate(big_levels, start=len(small_levels)):
                p = _dot_nt(ql[rh_, cs_], kl[rs_, cs_])
                for rb in range(C // SUBLANES):
                    blk = (rb * SUBLANES) // bs
                    if blk % 2:
                        place(li, p, rb * SUBLANES - (blk + 1) // 2 * bs, rb, (blk - 1) * bs)
            amat = jnp.concatenate([jnp.concatenate(row_tiles, axis=1) for row_tiles in tiles], axis=0)
            intra = _dot(amat.astype(_BF), vc)
            st_ref[h] = st * dec[:, cs_] + _dot_tn(vc, k_st[:, cs_])
            o_ref[rs_, cs_] = inter + intra + o_parts[h][rs_]

    o = o_ref[...]
    hg = proj(_OFF_HG, HW)
    yh_parts = []
    for h in range(HGRN_HEADS):
        oh = head(o, h)
        ms = _rowsum(oh * oh) * (1.0 / HGRN_DK)
        yh_parts.append(oh * lax.rsqrt(ms + NORM_EPS))
    yh = jnp.concatenate(yh_parts, axis=1) * hgn_ref[...]
    yh = (yh * (hg * jax.nn.sigmoid(hg))).astype(_BF)

    up_h = _dot(yh, w_uh_ref[...])
    half = len(gate_parts) // 2
    z_ga = jnp.concatenate(gate_parts[:half], axis=1)
    z_gh = jnp.concatenate(gate_parts[half:], axis=1)
    merged = jax.nn.sigmoid(z_ga) * up_a + jax.nn.sigmoid(z_gh) * up_h
    h1 = x + _dot(merged.astype(_BF), w_out_ref[...])
    h1_ref[0] = h1

    xn2 = _rms(h1, gffn_ref[...])
    xp_ref[0] = _pack_bf16_pairs(xn2)
    x_hi = xn2.astype(_BF)
    x_lo = (xn2 - x_hi.astype(_F32)).astype(_BF)
    w_rt = w_rt_ref[...]
    w_hi = w_rt.astype(_BF)
    w_lo = (w_rt - w_hi.astype(_F32)).astype(_BF)
    logits = (_dot_nt(w_hi, x_hi) + (_dot_nt(w_hi, x_lo) + _dot_nt(w_lo, x_hi))) + b_r_ref[...]
    eidx = lax.broadcasted_iota(jnp.int32, (N_EXPERTS, T), 0)
    vals, sels, ohs = [], [], []
    l = logits
    for _ in range(TOP_K):
        m = jnp.max(l, axis=0, keepdims=True)
        sel = jnp.min(jnp.where(l == m, eidx, N_EXPERTS), axis=0, keepdims=True)
        oh = eidx == sel
        vals.append(m)
        sels.append(sel)
        ohs.append(oh)
        l = jnp.where(oh, neg_inf, l)
    es = [jnp.exp(v - vals[0]) for v in vals]
    den = es[0] + es[1] + es[2] + es[3]
    chosen = jnp.zeros((N_EXPERTS, T), _F32)
    for oh in ohs:
        chosen = chosen + jnp.where(oh, 1.0, 0.0)
    ui = lax.broadcasted_iota(jnp.int32, (T, T), 0)
    uj = lax.broadcasted_iota(jnp.int32, (T, T), 1)
    upper = jnp.where(ui < uj, 1.0, 0.0).astype(_BF)
    before = _dot(chosen.astype(_BF), upper) + cnt_sc[:, 0:1]
    for k in range(TOP_K):
        idx_ref[k:k + 1, :] = sels[k]
        gate_ref[k:k + 1, :] = es[k] / den
        rank_ref[k:k + 1, :] = jnp.sum(jnp.where(ohs[k], before, 0.0), axis=0, keepdims=True).astype(jnp.int32)
    cnt_sc[...] = cnt_sc[...] + _rowsum(chosen)
    cnt_ref[...] = cnt_sc[...]


def _const_spec(shape):
    return pl.BlockSpec(shape, lambda b, s: (0,) * len(shape), pipeline_mode=pl.Buffered(1))


def _mixer_call(batch0, B, x, sinks, cos_t, sin_t, gmix, w_in, b_in, lb, hgn, lvl, w_ua, w_uh, w_out, gffn, w_rt, b_r):
    S = x.shape[1]
    T = SEQ_TILE
    N = B * S
    nS = S // T
    tok_spec = pl.BlockSpec((TOP_K, T), lambda b, s: (0, b * nS + s))
    out_shape = (
        jax.ShapeDtypeStruct((B, S, D_MODEL), _F32),
        jax.ShapeDtypeStruct((B, S, D_MODEL // 2), jnp.uint32),
        jax.ShapeDtypeStruct((TOP_K, N), jnp.int32),
        jax.ShapeDtypeStruct((TOP_K, N), jnp.int32),
        jax.ShapeDtypeStruct((TOP_K, N), _F32),
        jax.ShapeDtypeStruct((N_EXPERTS, LANES), _F32),
    )
    in_specs = [
        pl.BlockSpec(memory_space=pltpu.SMEM),
        pl.BlockSpec((1, T, D_MODEL), lambda b, s: (batch0 + b, s, 0)),
        pl.BlockSpec((T, LANES), lambda b, s: (s, 0)),
        pl.BlockSpec((T, LANES), lambda b, s: (s, 0)),
        _const_spec((1, D_MODEL)),
        _const_spec((D_MODEL, IN_COLS)),
        _const_spec((1, IN_COLS)),
        _const_spec((2, HGRN_WIDTH)),
        _const_spec((1, HGRN_WIDTH)),
        _const_spec((HGRN_CHUNK, HGRN_CHUNK)),
        _const_spec((ATTN_WIDTH, D_MODEL)),
        _const_spec((HGRN_WIDTH, D_MODEL)),
        _const_spec((D_MODEL, D_MODEL)),
        _const_spec((1, D_MODEL)),
        _const_spec((N_EXPERTS, D_MODEL)),
        _const_spec((N_EXPERTS, 1)),
    ]
    out_specs = (
        pl.BlockSpec((1, T, D_MODEL), lambda b, s: (b, s, 0)),
        pl.BlockSpec((1, T, D_MODEL // 2), lambda b, s: (b, s, 0)),
        tok_spec, tok_spec, tok_spec,
        pl.BlockSpec((N_EXPERTS, LANES), lambda b, s: (0, 0)),
    )
    scratch = [
        pltpu.VMEM((WINDOW, KV_WIDTH), _F32),
        pltpu.VMEM((WINDOW, KV_WIDTH), _F32),
        pltpu.VMEM((HGRN_HEADS, HGRN_DK, HGRN_DK), _F32),
        pltpu.VMEM((N_EXPERTS, LANES), _F32),
        pltpu.VMEM((T, ATTN_WIDTH), _F32),
        pltpu.VMEM((T, HGRN_WIDTH), _F32),
    ]
    return pl.pallas_call(
        _mixer_kernel,
        grid=(B, nS),
        in_specs=in_specs,
        out_specs=out_specs,
        out_shape=out_shape,
        scratch_shapes=scratch,
        compiler_params=pltpu.CompilerParams(
            dimension_semantics=("arbitrary", "arbitrary"), vmem_limit_bytes=VMEM_LIMIT),
        name="mixer",
    )(sinks, x, cos_t, sin_t, gmix, w_in, b_in, lb, hgn, lvl, w_ua, w_uh, w_out, gffn, w_rt, b_r)


def _dest_kernel(start_ref, idx_ref, rank_ref, dest_ref):
    idx = idx_ref[...]
    dest = rank_ref[...]
    for e in range(N_EXPERTS):
        dest = dest + jnp.where(idx == e, start_ref[e], 0)
    dest_ref[...] = dest


def _dest_call(pad_starts, idx, rank):
    K, N = idx.shape
    spec = pl.BlockSpec((K, DEST_TILE), lambda i: (0, i))
    return pl.pallas_call(
        _dest_kernel,
        grid=(N // DEST_TILE,),
        in_specs=[pl.BlockSpec(memory_space=pltpu.SMEM), spec, spec],
        out_specs=spec,
        out_shape=jax.ShapeDtypeStruct((K, N), jnp.int32),
        compiler_params=pltpu.CompilerParams(dimension_semantics=("arbitrary",)),
        name="dest",
    )(pad_starts, idx, rank)


def _sc_mesh():
    return plsc.VectorSubcoreMesh(core_axis_name="c", subcore_axis_name="s")


def _sc_worker_id():
    return lax.axis_index("s") * SC_CORES + lax.axis_index("c")


def _sc_scatter_rows(x, dest, n_rows):
    N, C = x.shape
    K = dest.shape[0]
    n_chunks = N // SC_CHUNK
    per_worker = n_chunks // SC_WORKERS
    dest3 = dest.reshape(K, n_chunks, SC_CHUNK)

    @functools.partial(
        pl.kernel, out_type=jax.ShapeDtypeStruct((n_rows, C), x.dtype), mesh=_sc_mesh(),
        scratch_types=[pltpu.VMEM((K, SC_CHUNK), jnp.int32), pltpu.VMEM((SC_CHUNK, C), x.dtype),
                       pltpu.SemaphoreType.DMA])
    def scatter(x_hbm, i_hbm, o_hbm, idx_v, rows_v, sem):
        base = _sc_worker_id() * per_worker

        @pl.loop(0, per_worker)
        def _(j):
            c = base + j
            off = pl.multiple_of(c * SC_CHUNK, SC_CHUNK)
            for k in range(K):
                pltpu.sync_copy(i_hbm.at[k, c], idx_v.at[k])
            pltpu.sync_copy(x_hbm.at[pl.ds(off, SC_CHUNK)], rows_v)
            for k in range(K):
                pltpu.async_copy(rows_v, o_hbm.at[idx_v.at[k]], sem).wait()

    return scatter(x, dest3)


def _sc_gather_rows(table, idx):
    M = idx.shape[0]
    C = table.shape[1]
    per_worker = M // SC_WORKERS
    n_chunks = per_worker // SC_CHUNK

    @functools.partial(
        pl.kernel, out_type=jax.ShapeDtypeStruct((M, C), table.dtype), mesh=_sc_mesh(),
        scratch_types=[pltpu.VMEM((SC_CHUNK,), jnp.int32), pltpu.VMEM((SC_CHUNK, C), table.dtype),
                       pltpu.SemaphoreType.DMA])
    def gather(t_hbm, i_hbm, o_hbm, idx_v, rows_v, sem):
        base = _sc_worker_id() * per_worker

        @pl.loop(0, n_chunks)
        def _(j):
            off = pl.multiple_of(base + j * SC_CHUNK, SC_CHUNK)
            pltpu.sync_copy(i_hbm.at[pl.ds(off, SC_CHUNK)], idx_v)
            pltpu.async_copy(t_hbm.at[idx_v], rows_v, sem).wait()
            pltpu.sync_copy(rows_v, o_hbm.at[pl.ds(off, SC_CHUNK)])

    return gather(table, idx)


def _expert_kernel(blk_e_ref, nb_ref, next_ref, xs_ref, w1_hbm, b1_ref, w2_hbm, b2_ref, ys_ref,
                   w1f_ref, w2f_ref, w1b_ref, w2b_ref, act0_ref, act1_ref, sems):
    i = pl.program_id(0)
    nb = nb_ref[0]
    last = nb - 1

    def expert_of(j):
        return blk_e_ref[jnp.clip(j, 0, last)]

    def first_block(j):
        return jnp.logical_or(j == 0, expert_of(j) != expert_of(j - 1))

    def w1_copy(expert):
        return pltpu.make_async_copy(w1_hbm.at[expert], w1f_ref, sems.at[0])

    def w2_copy(expert):
        return pltpu.make_async_copy(w2_hbm.at[expert], w2f_ref, sems.at[1])

    @pl.when(i == 0)
    def _():
        w1_copy(expert_of(0)).start()
        w2_copy(expert_of(0)).start()
        act1_ref[...] = jnp.zeros_like(act1_ref)
        w2b_ref[...] = jnp.zeros_like(w2b_ref)

    @pl.when(jnp.logical_and(i < nb, first_block(i)))
    def _():
        e = expert_of(i)
        w1_copy(e).wait()
        for r in range(0, D_MODEL, WEIGHT_CAST_ROWS):
            w1b_ref[r:r + WEIGHT_CAST_ROWS, :] = w1f_ref[r:r + WEIGHT_CAST_ROWS, :].astype(_BF)
        nxt = next_ref[e]

        @pl.when(nxt >= 0)
        def _():
            w1_copy(nxt).start()

    @pl.when(jnp.logical_and(jnp.logical_and(i >= 1, i <= nb), first_block(i - 1)))
    def _():
        e = expert_of(i - 1)
        w2_copy(e).wait()
        for r in range(0, D_EXPERT, WEIGHT_CAST_ROWS):
            w2b_ref[r:r + WEIGHT_CAST_ROWS, :] = w2f_ref[r:r + WEIGHT_CAST_ROWS, :].astype(_BF)
        nxt = next_ref[e]

        @pl.when(nxt >= 0)
        def _():
            w2_copy(nxt).start()

    def both_halves(act_w_ref, act_r_ref):
        y = _dot(act_r_ref[...], w2b_ref[...]) + b2_ref[0]
        ys_ref[...] = _pack_bf16_pairs(y)
        xb = _unpack_bf16_pairs(xs_ref[...]).astype(_BF)
        h = _dot(xb, w1b_ref[...]) + b1_ref[0]
        glu = jnp.minimum(h[:, :D_EXPERT], SWIGLU_LIMIT)
        lin = jnp.clip(h[:, D_EXPERT:], -SWIGLU_LIMIT, SWIGLU_LIMIT)
        act_w_ref[...] = (glu * jax.nn.sigmoid(SWIGLU_ALPHA * glu) * (lin + 1.0)).astype(_BF)

    busy = i <= nb

    @pl.when(jnp.logical_and(busy, i % 2 == 0))
    def _():
        both_halves(act0_ref, act1_ref)

    @pl.when(jnp.logical_and(busy, i % 2 == 1))
    def _():
        both_halves(act1_ref, act0_ref)

    @pl.when(jnp.logical_not(busy))
    def _():
        ys_ref[...] = jnp.zeros_like(ys_ref)


def _expert_call(blk_e, nb_used, next_expert, xs, w1, b1, w2, b2):
    n_blocks = xs.shape[0] // MOE_BLOCK
    M = MOE_BLOCK
    cur = lambda i, nb: jnp.minimum(i, nb[0] - 1)
    prev = lambda i, nb: jnp.clip(i - 1, 0, nb[0] - 1)
    grid_spec = pltpu.PrefetchScalarGridSpec(
        num_scalar_prefetch=3,
        grid=(n_blocks + 1,),
        in_specs=[
            pl.BlockSpec((M, D_MODEL // 2), lambda i, be, nb, nx: (cur(i, nb), 0)),
            pl.BlockSpec(memory_space=pl.ANY),
            pl.BlockSpec((1, 1, 2 * D_EXPERT), lambda i, be, nb, nx: (be[cur(i, nb)], 0, 0)),
            pl.BlockSpec(memory_space=pl.ANY),
            pl.BlockSpec((1, 1, D_MODEL), lambda i, be, nb, nx: (be[prev(i, nb)], 0, 0)),
        ],
        out_specs=pl.BlockSpec((M, D_MODEL // 2), lambda i, be, nb, nx: (jnp.maximum(i - 1, 0), 0)),
        scratch_shapes=[pltpu.VMEM((D_MODEL, 2 * D_EXPERT), _F32), pltpu.VMEM((D_EXPERT, D_MODEL), _F32),
                        pltpu.VMEM((D_MODEL, 2 * D_EXPERT), _BF), pltpu.VMEM((D_EXPERT, D_MODEL), _BF),
                        pltpu.VMEM((M, D_EXPERT), _BF), pltpu.VMEM((M, D_EXPERT), _BF),
                        pltpu.SemaphoreType.DMA((2,))],
    )
    return pl.pallas_call(
        _expert_kernel,
        grid_spec=grid_spec,
        out_shape=jax.ShapeDtypeStruct(xs.shape, jnp.uint32),
        compiler_params=pltpu.CompilerParams(
            dimension_semantics=("arbitrary",), vmem_limit_bytes=VMEM_LIMIT),
        name="experts",
    )(blk_e, nb_used, next_expert, xs, w1, b1, w2, b2)


def _combine_kernel(h1_ref, gate_ref, gfin_ref, y0_ref, y1_ref, y2_ref, y3_ref, *out_refs):
    out_ref = out_refs[-1]
    acc = h1_ref[...]
    g = gate_ref[...]
    gates = jnp.concatenate([g, jnp.zeros_like(g)], axis=0).T
    for k, y_ref in enumerate((y0_ref, y1_ref, y2_ref, y3_ref)):
        acc = acc + gates[:, k:k + 1] * _unpack_bf16_pairs(y_ref[...])
    out_ref[...] = _rms(acc, gfin_ref[...])


def _combine_call(h1, gates, gfin, yg, out_prev, tile0, n_total):
    N = h1.shape[0]
    T = COMBINE_TILE
    nT = N // T
    y_specs = [pl.BlockSpec((T, D_MODEL // 2), functools.partial(lambda i, k: (k * nT + i, 0), k=k))
               for k in range(TOP_K)]
    in_specs = [
        pl.BlockSpec((T, D_MODEL), lambda i: (i, 0)),
        pl.BlockSpec((TOP_K, T), lambda i: (0, i)),
        pl.BlockSpec((1, D_MODEL), lambda i: (0, 0)),
    ] + y_specs
    args = [h1, gates, gfin, yg, yg, yg, yg]
    aliases = {}
    if out_prev is not None:
        in_specs.append(pl.BlockSpec(memory_space=pl.ANY))
        args.append(out_prev)
        aliases = {len(args) - 1: 0}
    return pl.pallas_call(
        _combine_kernel,
        grid=(nT,),
        in_specs=in_specs,
        out_specs=pl.BlockSpec((T, D_MODEL), lambda i: (tile0 + i, 0)),
        out_shape=jax.ShapeDtypeStruct((n_total, D_MODEL), _F32),
        input_output_aliases=aliases,
        compiler_params=pltpu.CompilerParams(dimension_semantics=("arbitrary",)),
        name="combine",
    )(*args)


def _hgrn_level_table():
    t = np.arange(HGRN_CHUNK)[:, None]
    u = np.arange(HGRN_CHUNK)[None, :]
    top = np.floor(np.log2(np.maximum(t ^ u, 1))).astype(np.int32)
    return np.where(t > u, top, -1).astype(np.int32)


def _rope_tables(S):
    half = HEAD_DIM // 2
    inv_freq = ROPE_THETA ** (-(jnp.arange(half, dtype=_F32) * 2.0 / HEAD_DIM))
    ang = jnp.arange(S, dtype=_F32)[:, None] * inv_freq[None, :]
    cos = jnp.tile(jnp.cos(ang), (1, 4))
    sin = jnp.tile(jnp.sin(ang), (1, 4))
    sign = jnp.where(jnp.arange(LANES) % HEAD_DIM < ROT_HALF, -1.0, 1.0).astype(_F32)
    return cos, sin * sign[None, :]


def kernel(x, norm_mix_g, w_in, b_in, attn_sinks, hgrn_lb, hgrn_norm_g, w_up_attn, w_up_hgrn, w_out,
           norm_ffn_g, w_router, b_router, w_moe1, b_moe1, w_moe2, b_moe2, norm_final_g):
    B, S, D = x.shape
    N = B * S
    assert D == D_MODEL and S % SEQ_TILE == 0 and SEQ_TILE % WINDOW == 0
    assert B % BATCH_PARTS == 0 and (N // BATCH_PARTS) % (SC_WORKERS * SC_CHUNK) == 0
    assert (N // BATCH_PARTS) % COMBINE_TILE == 0 and (N // BATCH_PARTS) % DEST_TILE == 0
    assert norm_mix_g.shape[0] == 1 and hgrn_lb.shape[0] == 2
    assert GROUP == 4 and N_KV_HEADS == 2 and 2 * HEAD_DIM == LANES

    cos_t, sin_t = _rope_tables(S)
    weights = (attn_sinks[0], cos_t, sin_t, norm_mix_g,
               w_in[0].astype(_BF), b_in, hgrn_lb, hgrn_norm_g, jnp.asarray(_hgrn_level_table()),
               w_up_attn[0].astype(_BF), w_up_hgrn[0].astype(_BF), w_out[0].astype(_BF),
               norm_ffn_g, w_router[0].T, b_router[0][:, None])

    Bp = B // BATCH_PARTS
    Np = Bp * S
    n_blocks = (Np * TOP_K) // MOE_BLOCK + N_EXPERTS
    out = None
    for part in range(BATCH_PARTS):
        h1, xp, idx, rank, gates, cnt = _mixer_call(part * Bp, Bp, x, *weights)

        counts = cnt[:, 0].astype(jnp.int32)
        padded = ((counts + MOE_BLOCK - 1) // MOE_BLOCK) * MOE_BLOCK
        pad_ends = jnp.cumsum(padded)
        pad_starts = pad_ends - padded
        dest = _dest_call(pad_starts, idx, rank)
        blk_start = jnp.arange(n_blocks, dtype=jnp.int32) * MOE_BLOCK
        blk_e = jnp.minimum(jnp.sum((pad_ends[None, :] <= blk_start[:, None]).astype(jnp.int32), axis=1),
                            N_EXPERTS - 1)
        nb_used = (pad_ends[-1:] // MOE_BLOCK).astype(jnp.int32)
        e_ids = jnp.arange(N_EXPERTS, dtype=jnp.int32)
        later = jnp.logical_and(e_ids[None, :] > e_ids[:, None], counts[None, :] > 0)
        next_expert = jnp.min(jnp.where(later, e_ids[None, :], N_EXPERTS), axis=1)
        next_expert = jnp.where(next_expert < N_EXPERTS, next_expert, -1).astype(jnp.int32)

        xs = _sc_scatter_rows(xp.reshape(Np, D // 2), dest, n_blocks * MOE_BLOCK)
        ys = _expert_call(blk_e, nb_used, next_expert, xs, w_moe1[0], b_moe1[0][:, None, :], w_moe2[0], b_moe2[0][:, None, :])
        yg = _sc_gather_rows(ys, dest.reshape(TOP_K * Np))
        out = _combine_call(h1.reshape(Np, D), gates, norm_final_g[None, :], yg, out,
                            part * (Np // COMBINE_TILE), N)
    return out.reshape(B, S, D)
```

```python
import functools

import numpy as np
import jax
import jax.numpy as jnp
from jax import lax
from jax.experimental import pallas as pl
from jax.experimental.pallas import tpu as pltpu
from jax.experimental.pallas import tpu_sc as plsc

D_MODEL = 1024
HEAD_DIM = 64
ROT_HALF = HEAD_DIM // 2
N_Q_HEADS = 8
N_KV_HEADS = 2
GROUP = N_Q_HEADS // N_KV_HEADS
ATTN_WIDTH = N_Q_HEADS * HEAD_DIM
KV_WIDTH = N_KV_HEADS * HEAD_DIM
WINDOW = 128
ROPE_THETA = 10000.0
HGRN_HEADS = 4
HGRN_DK = 128
HGRN_WIDTH = HGRN_HEADS * HGRN_DK
N_EXPERTS = 32
TOP_K = 4
D_EXPERT = 1024
SWIGLU_ALPHA = 1.702
SWIGLU_LIMIT = 7.0
MOE_BLOCK = 512
NORM_EPS = 1e-5

_OFF_Q = 0
_OFF_K = _OFF_Q + ATTN_WIDTH
_OFF_V = _OFF_K + KV_WIDTH
_OFF_HQ = _OFF_V + KV_WIDTH
_OFF_HF = _OFF_HQ + HGRN_WIDTH
_OFF_HI = _OFF_HF + HGRN_WIDTH
_OFF_HG = _OFF_HI + HGRN_WIDTH
_OFF_GA = _OFF_HG + HGRN_WIDTH
_OFF_GH = _OFF_GA + D_MODEL
IN_COLS = _OFF_GH + D_MODEL

LANES = 128
SUBLANES = 8
SEQ_TILE = 512
HGRN_CHUNK = 256
GATE_JOB_COLS = 256
COMBINE_TILE = 512
DEST_TILE = 8192
BATCH_PARTS = 2
WEIGHT_CAST_ROWS = 128
SC_CORES = 2
SC_WORKERS = 32
SC_CHUNK = 128
VMEM_LIMIT = 56 * 1024 * 1024

_BF = jnp.bfloat16
_F8 = jnp.float8_e4m3fn
FP8_TOP = 224.0
FP8_TINY = 1e-30
_F32 = jnp.float32


def _dot(a, b):
    return jnp.dot(a, b, preferred_element_type=_F32)


def _dot_nt(a, b):
    return lax.dot_general(a, b, (((1,), (1,)), ((), ())), preferred_element_type=_F32)


def _dot_tn(a, b):
    return lax.dot_general(a, b, (((0,), (0,)), ((), ())), preferred_element_type=_F32)


def _rowsum(x):
    return jnp.sum(x, axis=1, keepdims=True)


def _rms(x, g):
    ms = _rowsum(x * x) * (1.0 / x.shape[1])
    return x * lax.rsqrt(ms + NORM_EPS) * g


def _fp8_scaled(v):
    amax = jnp.max(jnp.max(jnp.abs(v.astype(_F32)), axis=1, keepdims=True), axis=0, keepdims=True)
    s = jnp.exp2(jnp.floor(jnp.log2(FP8_TOP / jnp.maximum(amax, FP8_TINY))))
    return (v.astype(_F32) * s).astype(_F8), 1.0 / s


def _fp8_rows(v):
    amax = jnp.max(jnp.abs(v), axis=1, keepdims=True)
    s = jnp.exp2(jnp.floor(jnp.log2(FP8_TOP / jnp.maximum(amax, FP8_TINY))))
    return (v * s).astype(_F8), 1.0 / s


def _pack_bf16_pairs(x):
    n = x.shape[1] // 2
    lo = lax.bitcast_convert_type(x[:, :n].astype(_BF).astype(_F32), jnp.uint32)
    hi = lax.bitcast_convert_type(x[:, n:].astype(_BF).astype(_F32), jnp.uint32)
    return (lo >> 16) | (hi & jnp.uint32(0xFFFF0000))


def _unpack_bf16_pairs(u):
    lo = lax.bitcast_convert_type(u << 16, _F32)
    hi = lax.bitcast_convert_type(u & jnp.uint32(0xFFFF0000), _F32)
    return jnp.concatenate([lo, hi], axis=1)


def _mixer_kernel(sinks_ref, x_ref, cos_ref, sin_ref, gmix_ref, w_in_ref, b_in_ref, lb_ref,
                  hgn_ref, lvl_ref, w_ua_ref, w_uh_ref, w_out_ref, gffn_ref, w_rt_ref, b_r_ref,
                  h1_ref, xp_ref, idx_ref, rank_ref, gate_ref, cnt_ref,
                  kc_ref, vc_ref, st_ref, cnt_sc, ya_ref, o_ref):
    T = SEQ_TILE
    b = pl.program_id(0)
    s = pl.program_id(1)

    @pl.when(jnp.logical_and(b == 0, s == 0))
    def _():
        cnt_sc[...] = jnp.zeros_like(cnt_sc)

    @pl.when(s == 0)
    def _():
        kc_ref[...] = jnp.zeros_like(kc_ref)
        vc_ref[...] = jnp.zeros_like(vc_ref)
        st_ref[...] = jnp.zeros_like(st_ref)

    x = x_ref[0]
    xn = _rms(x, gmix_ref[...]).astype(_BF)

    def proj(off, width):
        return _dot(xn, w_in_ref[:, off:off + width]) + b_in_ref[:, off:off + width]

    cos = cos_ref[...]
    sin = sin_ref[...]

    lane = lax.broadcasted_iota(jnp.int32, (1, LANES), 1)
    first_half = (lane % HEAD_DIM) < ROT_HALF

    def rope(t):
        swapped = jnp.where(first_half, pltpu.roll(t, LANES - ROT_HALF, axis=1), pltpu.roll(t, ROT_HALF, axis=1))
        return t * cos + swapped * sin

    zq = proj(_OFF_Q, ATTN_WIDTH)
    scale = HEAD_DIM ** -0.5
    q_pairs = [(rope(zq[:, LANES * p:LANES * (p + 1)]) * scale).astype(_BF) for p in range(N_Q_HEADS // 2)]
    k_rot = rope(proj(_OFF_K, KV_WIDTH))
    v_new = proj(_OFF_V, KV_WIDTH)

    k_ext = jnp.concatenate([kc_ref[...], k_rot], axis=0)
    v_ext = jnp.concatenate([vc_ref[...], v_new], axis=0)
    kc_ref[...] = k_rot[T - WINDOW:, :]
    vc_ref[...] = v_new[T - WINDOW:, :]

    lane_lo = lane < HEAD_DIM
    k_r64 = pltpu.roll(k_ext, HEAD_DIM, axis=1)
    v_r64 = pltpu.roll(v_ext, HEAD_DIM, axis=1)
    zero = jnp.zeros_like(k_ext)
    k_var = [(jnp.where(lane_lo, k_ext, zero).astype(_BF), jnp.where(lane_lo, zero, k_r64).astype(_BF)),
             (jnp.where(lane_lo, k_r64, zero).astype(_BF), jnp.where(lane_lo, zero, k_ext).astype(_BF))]
    v_var = [(jnp.where(lane_lo, v_ext, zero).astype(_BF), jnp.where(lane_lo, zero, v_r64).astype(_BF)),
             (jnp.where(lane_lo, v_r64, zero).astype(_BF), jnp.where(lane_lo, zero, v_ext).astype(_BF))]

    qi = lax.broadcasted_iota(jnp.int32, (2 * WINDOW, 2 * WINDOW), 0) % WINDOW
    kj = lax.broadcasted_iota(jnp.int32, (2 * WINDOW, 2 * WINDOW), 1)
    band = jnp.logical_and(kj > qi, kj <= qi + WINDOW)
    row_top = lax.broadcasted_iota(jnp.int32, (2 * WINDOW, 1), 0) < WINDOW
    neg_inf = jnp.float32(-jnp.inf)

    gate_cols = [(_OFF_GA + c, GATE_JOB_COLS) for c in range(0, 2 * D_MODEL, GATE_JOB_COLS)]
    gate_parts = []
    jobs_per_unit = -(-len(gate_cols) // ((T // WINDOW) * N_KV_HEADS))

    for n in range(T // WINDOW):
        if n == 0:
            ok = jnp.logical_and(band, jnp.logical_or(kj >= WINDOW, s > 0))
        else:
            ok = band
        r0 = n * WINDOW
        for j in range(N_KV_HEADS):
            q2 = jnp.concatenate([q_pairs[2 * j][r0:r0 + WINDOW], q_pairs[2 * j + 1][r0:r0 + WINDOW]], axis=0)
            kcat = jnp.concatenate([k_var[j][0][r0:r0 + 2 * WINDOW], k_var[j][1][r0:r0 + 2 * WINDOW]], axis=0)
            vcat = jnp.concatenate([v_var[j][0][r0:r0 + 2 * WINDOW], v_var[j][1][r0:r0 + 2 * WINDOW]], axis=0)
            sc = _dot_nt(q2, kcat)
            ps, rs = [], []
            for half in range(2):
                sh = jnp.where(ok, sc[:, 2 * WINDOW * half:2 * WINDOW * (half + 1)], neg_inf)
                snk = jnp.where(row_top, sinks_ref[GROUP * j + half], sinks_ref[GROUP * j + 2 + half])
                m = jnp.maximum(jnp.max(sh, axis=1, keepdims=True), snk)
                p = jnp.exp(sh - m)
                den = _rowsum(p) + jnp.exp(snk - m)
                ps.append(p.astype(_BF))
                rs.append(1.0 / den)
            o = _dot(jnp.concatenate(ps, axis=1), vcat)
            o = o * jnp.where(lane_lo, rs[0], rs[1])
            ya_ref[r0:r0 + WINDOW, LANES * (2 * j):LANES * (2 * j + 1)] = o[:WINDOW]
            ya_ref[r0:r0 + WINDOW, LANES * (2 * j + 1):LANES * (2 * j + 2)] = o[WINDOW:]
            for _ in range(min(jobs_per_unit, len(gate_cols) - len(gate_parts))):
                gate_parts.append(proj(*gate_cols[len(gate_parts)]))

    up_a = _dot(ya_ref[...].astype(_BF), w_ua_ref[...])

    HW = HGRN_WIDTH
    a0 = lb_ref[0:1, :]
    a1 = lb_ref[1:2, :]
    am = jnp.maximum(a0, a1)
    e0 = jnp.exp(a0 - am)
    lb = e0 / (e0 + jnp.exp(a1 - am))
    f = lb + (1.0 - lb) * jax.nn.sigmoid(proj(_OFF_HF, HW))
    kk = 1.0 - f
    g = jnp.log2(f)
    hq = proj(_OFF_HQ, HW)
    hv = proj(_OFF_HI, HW)

    row = lax.broadcasted_iota(jnp.int32, (T, 1), 0)
    C = HGRN_CHUNK

    def rdown(t, d):
        return pltpu.roll(t.reshape(T // SUBLANES, SUBLANES, HW), d, axis=1).reshape(T, HW)

    def rup(t, d):
        return pltpu.roll(t.reshape(T // SUBLANES, SUBLANES, HW), SUBLANES - d, axis=1).reshape(T, HW)

    def head(t, h):
        return t[:, HGRN_DK * h:HGRN_DK * (h + 1)]

    qk0 = hq * kk
    o_parts = [_rowsum(head(qk0, h)) * head(hv, h) for h in range(HGRN_HEADS)]

    small_levels = []
    bsz = 1
    lb_cum, lb_tot = g, g
    while bsz < SUBLANES:
        odd = (row % (2 * bsz)) >= bsz
        ex = jnp.exp2(jnp.where(odd, lb_cum, lb_tot - lb_cum))
        small_levels.append(((hq * ex).astype(_BF), (kk * ex).astype(_BF)))
        prev_tot = rdown(lb_tot, bsz)
        lb_cum = lb_cum + jnp.where(odd, prev_tot, 0.0)
        lb_tot = lb_tot + jnp.where(odd, prev_tot, rup(lb_tot, bsz))
        bsz *= 2

    cum = [lb_cum[r:r + bsz] for r in range(0, T, bsz)]
    tot = [lb_tot[r:r + 1] for r in range(0, T, bsz)]
    kk_b = kk.astype(_BF)
    big_levels = []
    while bsz < C:
        q_rows, k_rows = [], []
        for j, r in enumerate(range(0, T, bsz)):
            if j % 2:
                q_rows.append(hq[r:r + bsz] * jnp.exp2(cum[j]))
                k_rows.append(kk_b[r:r + bsz])
            else:
                k_rows.append((kk[r:r + bsz] * jnp.exp2(tot[j] - cum[j])).astype(_BF))
        big_levels.append((bsz, jnp.concatenate(q_rows, axis=0).astype(_BF), jnp.concatenate(k_rows, axis=0)))
        cum = [jnp.concatenate([cum[j], cum[j + 1] + tot[j]], axis=0) for j in range(0, len(cum), 2)]
        tot = [tot[j] + tot[j + 1] for j in range(0, len(tot), 2)]
        bsz *= 2
    hv_b = hv.astype(_BF)
    lvl = lvl_ref[...]

    for c in range(T // C):
        rs_ = slice(c * C, (c + 1) * C)
        rh_ = slice(c * C // 2, (c + 1) * C // 2)
        q_in = (hq[rs_] * jnp.exp2(cum[c])).astype(_BF)
        k_st = (kk[rs_] * jnp.exp2(tot[c] - cum[c])).astype(_BF)
        dec = jnp.exp2(tot[c])
        for h in range(HGRN_HEADS):
            cs_ = slice(HGRN_DK * h, HGRN_DK * (h + 1))
            st = st_ref[h]
            vc = hv_b[rs_, cs_]
            inter = _dot_nt(q_in[:, cs_], st.astype(_BF))
            zero_tile = jnp.zeros((SUBLANES, LANES), _F32)
            tiles = [[zero_tile for _ in range(C // LANES)] for _ in range(C // SUBLANES)]

            def place(li, p, p_row, rb, col):
                r, ct = rb * SUBLANES, col // LANES
                cs = slice(ct * LANES, (ct + 1) * LANES)
                tiles[rb][ct] = jnp.where(lvl[r:r + SUBLANES, cs] == li, p[p_row:p_row + SUBLANES, cs], tiles[rb][ct])

            for li, (ql, kl) in enumerate(small_levels):
                p = _dot_nt(ql[rs_, cs_], kl[rs_, cs_])
                for rb in range(C // SUBLANES):
                    place(li, p, rb * SUBLANES, rb, rb * SUBLANES)
            for li, (bs, ql, kl) in enumerate(big_levels, start=len(small_levels)):
                p = _dot_nt(ql[rh_, cs_], kl[rs_, cs_])
                for rb in range(C // SUBLANES):
                    blk = (rb * SUBLANES) // bs
                    if blk % 2:
                        place(li, p, rb * SUBLANES - (blk + 1) // 2 * bs, rb, (blk - 1) * bs)
            amat = jnp.concatenate([jnp.concatenate(row_tiles, axis=1) for row_tiles in tiles], axis=0)
            intra = _dot(amat.astype(_BF), vc)
            st_ref[h] = st * dec[:, cs_] + _dot_tn(vc, k_st[:, cs_])
            o_ref[rs_, cs_] = inter + intra + o_parts[h][rs_]

    o = o_ref[...]
    hg = proj(_OFF_HG, HW)
    yh_parts = []
    for h in range(HGRN_HEADS):
        oh = head(o, h)
        ms = _rowsum(oh * oh) * (1.0 / HGRN_DK)
        yh_parts.append(oh * lax.rsqrt(ms + NORM_EPS))
    yh = jnp.concatenate(yh_parts, axis=1) * hgn_ref[...]
    yh = (yh * (hg * jax.nn.sigmoid(hg))).astype(_BF)

    up_h = _dot(yh, w_uh_ref[...])
    half = len(gate_parts) // 2
    z_ga = jnp.concatenate(gate_parts[:half], axis=1)
    z_gh = jnp.concatenate(gate_parts[half:], axis=1)
    merged = jax.nn.sigmoid(z_ga) * up_a + jax.nn.sigmoid(z_gh) * up_h
    h1 = x + _dot(merged.astype(_BF), w_out_ref[...])
    h1_ref[0] = h1

    xn2 = _rms(h1, gffn_ref[...])
    xp_ref[0] = _pack_bf16_pairs(xn2)
    x_hi = xn2.astype(_BF)
    x_lo = (xn2 - x_hi.astype(_F32)).astype(_BF)
    w_rt = w_rt_ref[...]
    w_hi = w_rt.astype(_BF)
    w_lo = (w_rt - w_hi.astype(_F32)).astype(_BF)
    logits = (_dot_nt(w_hi, x_hi) + (_dot_nt(w_hi, x_lo) + _dot_nt(w_lo, x_hi))) + b_r_ref[...]
    eidx = lax.broadcasted_iota(jnp.int32, (N_EXPERTS, T), 0)
    vals, sels, ohs = [], [], []
    l = logits
    for _ in range(TOP_K):
        m = jnp.max(l, axis=0, keepdims=True)
        sel = jnp.min(jnp.where(l == m, eidx, N_EXPERTS), axis=0, keepdims=True)
        oh = eidx == sel
        vals.append(m)
        sels.append(sel)
        ohs.append(oh)
        l = jnp.where(oh, neg_inf, l)
    es = [jnp.exp(v - vals[0]) for v in vals]
    den = es[0] + es[1] + es[2] + es[3]
    chosen = jnp.zeros((N_EXPERTS, T), _F32)
    for oh in ohs:
        chosen = chosen + jnp.where(oh, 1.0, 0.0)
    ui = lax.broadcasted_iota(jnp.int32, (T, T), 0)
    uj = lax.broadcasted_iota(jnp.int32, (T, T), 1)
    upper = jnp.where(ui < uj, 1.0, 0.0).astype(_BF)
    before = _dot(chosen.astype(_BF), upper) + cnt_sc[:, 0:1]
    for k in range(TOP_K):
        idx_ref[k:k + 1, :] = sels[k]
        gate_ref[k:k + 1, :] = es[k] / den
        rank_ref[k:k + 1, :] = jnp.sum(jnp.where(ohs[k], before, 0.0), axis=0, keepdims=True).astype(jnp.int32)
    cnt_sc[...] = cnt_sc[...] + _rowsum(chosen)
    cnt_ref[...] = cnt_sc[...]


def _const_spec(shape):
    return pl.BlockSpec(shape, lambda b, s: (0,) * len(shape), pipeline_mode=pl.Buffered(1))


def _mixer_call(batch0, B, x, sinks, cos_t, sin_t, gmix, w_in, b_in, lb, hgn, lvl, w_ua, w_uh, w_out, gffn, w_rt, b_r):
    S = x.shape[1]
    T = SEQ_TILE
    N = B * S
    nS = S // T
    tok_spec = pl.BlockSpec((TOP_K, T), lambda b, s: (0, b * nS + s))
    out_shape = (
        jax.ShapeDtypeStruct((B, S, D_MODEL), _F32),
        jax.ShapeDtypeStruct((B, S, D_MODEL // 2), jnp.uint32),
        jax.ShapeDtypeStruct((TOP_K, N), jnp.int32),
        jax.ShapeDtypeStruct((TOP_K, N), jnp.int32),
        jax.ShapeDtypeStruct((TOP_K, N), _F32),
        jax.ShapeDtypeStruct((N_EXPERTS, LANES), _F32),
    )
    in_specs = [
        pl.BlockSpec(memory_space=pltpu.SMEM),
        pl.BlockSpec((1, T, D_MODEL), lambda b, s: (batch0 + b, s, 0)),
        pl.BlockSpec((T, LANES), lambda b, s: (s, 0)),
        pl.BlockSpec((T, LANES), lambda b, s: (s, 0)),
        _const_spec((1, D_MODEL)),
        _const_spec((D_MODEL, IN_COLS)),
        _const_spec((1, IN_COLS)),
        _const_spec((2, HGRN_WIDTH)),
        _const_spec((1, HGRN_WIDTH)),
        _const_spec((HGRN_CHUNK, HGRN_CHUNK)),
        _const_spec((ATTN_WIDTH, D_MODEL)),
        _const_spec((HGRN_WIDTH, D_MODEL)),
        _const_spec((D_MODEL, D_MODEL)),
        _const_spec((1, D_MODEL)),
        _const_spec((N_EXPERTS, D_MODEL)),
        _const_spec((N_EXPERTS, 1)),
    ]
    out_specs = (
        pl.BlockSpec((1, T, D_MODEL), lambda b, s: (b, s, 0)),
        pl.BlockSpec((1, T, D_MODEL // 2), lambda b, s: (b, s, 0)),
        tok_spec, tok_spec, tok_spec,
        pl.BlockSpec((N_EXPERTS, LANES), lambda b, s: (0, 0)),
    )
    scratch = [
        pltpu.VMEM((WINDOW, KV_WIDTH), _F32),
        pltpu.VMEM((WINDOW, KV_WIDTH), _F32),
        pltpu.VMEM((HGRN_HEADS, HGRN_DK, HGRN_DK), _F32),
        pltpu.VMEM((N_EXPERTS, LANES), _F32),
        pltpu.VMEM((T, ATTN_WIDTH), _F32),
        pltpu.VMEM((T, HGRN_WIDTH), _F32),
    ]
    return pl.pallas_call(
        _mixer_kernel,
        grid=(B, nS),
        in_specs=in_specs,
        out_specs=out_specs,
        out_shape=out_shape,
        scratch_shapes=scratch,
        compiler_params=pltpu.CompilerParams(
            dimension_semantics=("arbitrary", "arbitrary"), vmem_limit_bytes=VMEM_LIMIT),
        name="mixer",
    )(sinks, x, cos_t, sin_t, gmix, w_in, b_in, lb, hgn, lvl, w_ua, w_uh, w_out, gffn, w_rt, b_r)


def _dest_kernel(start_ref, idx_ref, rank_ref, dest_ref):
    idx = idx_ref[...]
    dest = rank_ref[...]
    for e in range(N_EXPERTS):
        dest = dest + jnp.where(idx == e, start_ref[e], 0)
    dest_ref[...] = dest


def _dest_call(pad_starts, idx, rank):
    K, N = idx.shape
    spec = pl.BlockSpec((K, DEST_TILE), lambda i: (0, i))
    return pl.pallas_call(
        _dest_kernel,
        grid=(N // DEST_TILE,),
        in_specs=[pl.BlockSpec(memory_space=pltpu.SMEM), spec, spec],
        out_specs=spec,
        out_shape=jax.ShapeDtypeStruct((K, N), jnp.int32),
        compiler_params=pltpu.CompilerParams(dimension_semantics=("arbitrary",)),
        name="dest",
    )(pad_starts, idx, rank)


def _sc_mesh():
    return plsc.VectorSubcoreMesh(core_axis_name="c", subcore_axis_name="s")


def _sc_worker_id():
    return lax.axis_index("s") * SC_CORES + lax.axis_index("c")


def _sc_scatter_rows(x, dest, n_rows):
    N, C = x.shape
    K = dest.shape[0]
    n_chunks = N // SC_CHUNK
    per_worker = n_chunks // SC_WORKERS
    dest3 = dest.reshape(K, n_chunks, SC_CHUNK)

    @functools.partial(
        pl.kernel, out_type=jax.ShapeDtypeStruct((n_rows, C), x.dtype), mesh=_sc_mesh(),
        scratch_types=[pltpu.VMEM((K, SC_CHUNK), jnp.int32), pltpu.VMEM((SC_CHUNK, C), x.dtype),
                       pltpu.SemaphoreType.DMA])
    def scatter(x_hbm, i_hbm, o_hbm, idx_v, rows_v, sem):
        base = _sc_worker_id() * per_worker

        @pl.loop(0, per_worker)
        def _(j):
            c = base + j
            off = pl.multiple_of(c * SC_CHUNK, SC_CHUNK)
            for k in range(K):
                pltpu.sync_copy(i_hbm.at[k, c], idx_v.at[k])
            pltpu.sync_copy(x_hbm.at[pl.ds(off, SC_CHUNK)], rows_v)
            for k in range(K):
                pltpu.async_copy(rows_v, o_hbm.at[idx_v.at[k]], sem).wait()

    return scatter(x, dest3)


def _sc_gather_rows(table, idx):
    M = idx.shape[0]
    C = table.shape[1]
    per_worker = M // SC_WORKERS
    n_chunks = per_worker // SC_CHUNK

    @functools.partial(
        pl.kernel, out_type=jax.ShapeDtypeStruct((M, C), table.dtype), mesh=_sc_mesh(),
        scratch_types=[pltpu.VMEM((SC_CHUNK,), jnp.int32), pltpu.VMEM((SC_CHUNK, C), table.dtype),
                       pltpu.SemaphoreType.DMA])
    def gather(t_hbm, i_hbm, o_hbm, idx_v, rows_v, sem):
        base = _sc_worker_id() * per_worker

        @pl.loop(0, n_chunks)
        def _(j):
            off = pl.multiple_of(base + j * SC_CHUNK, SC_CHUNK)
            pltpu.sync_copy(i_hbm.at[pl.ds(off, SC_CHUNK)], idx_v)
            pltpu.async_copy(t_hbm.at[idx_v], rows_v, sem).wait()
            pltpu.sync_copy(rows_v, o_hbm.at[pl.ds(off, SC_CHUNK)])

    return gather(table, idx)


def _expert_kernel(blk_e_ref, nb_ref, next_ref, xs_ref, w1_hbm, b1_ref, w2_hbm, b2_ref, ys_ref,
                   w1f_ref, w2f_ref, w1b_ref, w2b_ref, act0_ref, act1_ref, asc0_ref, asc1_ref, sc_ref, sems):
    i = pl.program_id(0)
    nb = nb_ref[0]
    last = nb - 1

    def expert_of(j):
        return blk_e_ref[jnp.clip(j, 0, last)]

    def first_block(j):
        return jnp.logical_or(j == 0, expert_of(j) != expert_of(j - 1))

    def w1_copy(expert):
        return pltpu.make_async_copy(w1_hbm.at[expert], w1f_ref, sems.at[0])

    def w2_copy(expert):
        return pltpu.make_async_copy(w2_hbm.at[expert], w2f_ref, sems.at[1])

    @pl.when(i == 0)
    def _():
        w1_copy(expert_of(0)).start()
        w2_copy(expert_of(0)).start()
        act1_ref[...] = jnp.zeros_like(act1_ref)
        w2b_ref[...] = jnp.zeros_like(w2b_ref)
        sc_ref[...] = jnp.zeros_like(sc_ref)
        asc1_ref[...] = jnp.zeros_like(asc1_ref)

    @pl.when(jnp.logical_and(i < nb, first_block(i)))
    def _():
        e = expert_of(i)
        w1_copy(e).wait()
        w8, inv = _fp8_scaled(w1f_ref[...])
        w1b_ref[...] = w8
        sc_ref[0:1, :] = jnp.broadcast_to(inv, (1, LANES))
        nxt = next_ref[e]

        @pl.when(nxt >= 0)
        def _():
            w1_copy(nxt).start()

    @pl.when(jnp.logical_and(jnp.logical_and(i >= 1, i <= nb), first_block(i - 1)))
    def _():
        e = expert_of(i - 1)
        w2_copy(e).wait()
        w8, inv = _fp8_scaled(w2f_ref[...])
        w2b_ref[...] = w8
        sc_ref[1:2, :] = jnp.broadcast_to(inv, (1, LANES))
        nxt = next_ref[e]

        @pl.when(nxt >= 0)
        def _():
            w2_copy(nxt).start()

    def both_halves(act_w_ref, asc_w_ref, act_r_ref, asc_r_ref):
        y = _dot(act_r_ref[...], w2b_ref[...]) * (asc_r_ref[:, 0:1] * sc_ref[1:2, 0:1]) + b2_ref[0]
        ys_ref[...] = _pack_bf16_pairs(y)
        x8, inv_x = _fp8_rows(_unpack_bf16_pairs(xs_ref[...]))
        h = _dot(x8, w1b_ref[...]) * (inv_x * sc_ref[0:1, 0:1]) + b1_ref[0]
        glu = jnp.minimum(h[:, :D_EXPERT], SWIGLU_LIMIT)
        lin = jnp.clip(h[:, D_EXPERT:], -SWIGLU_LIMIT, SWIGLU_LIMIT)
        a8, inv_a = _fp8_rows(glu * jax.nn.sigmoid(SWIGLU_ALPHA * glu) * (lin + 1.0))
        act_w_ref[...] = a8
        asc_w_ref[...] = jnp.broadcast_to(inv_a, asc_w_ref.shape)

    busy = i <= nb

    @pl.when(jnp.logical_and(busy, i % 2 == 0))
    def _():
        both_halves(act0_ref, asc0_ref, act1_ref, asc1_ref)

    @pl.when(jnp.logical_and(busy, i % 2 == 1))
    def _():
        both_halves(act1_ref, asc1_ref, act0_ref, asc0_ref)

    @pl.when(jnp.logical_not(busy))
    def _():
        ys_ref[...] = jnp.zeros_like(ys_ref)


def _expert_call(blk_e, nb_used, next_expert, xs, w1, b1, w2, b2):
    n_blocks = xs.shape[0] // MOE_BLOCK
    M = MOE_BLOCK
    cur = lambda i, nb: jnp.minimum(i, nb[0] - 1)
    prev = lambda i, nb: jnp.clip(i - 1, 0, nb[0] - 1)
    grid_spec = pltpu.PrefetchScalarGridSpec(
        num_scalar_prefetch=3,
        grid=(n_blocks + 1,),
        in_specs=[
            pl.BlockSpec((M, D_MODEL // 2), lambda i, be, nb, nx: (cur(i, nb), 0)),
            pl.BlockSpec(memory_space=pl.ANY),
            pl.BlockSpec((1, 1, 2 * D_EXPERT), lambda i, be, nb, nx: (be[cur(i, nb)], 0, 0)),
            pl.BlockSpec(memory_space=pl.ANY),
            pl.BlockSpec((1, 1, D_MODEL), lambda i, be, nb, nx: (be[prev(i, nb)], 0, 0)),
        ],
        out_specs=pl.BlockSpec((M, D_MODEL // 2), lambda i, be, nb, nx: (jnp.maximum(i - 1, 0), 0)),
        scratch_shapes=[pltpu.VMEM((D_MODEL, 2 * D_EXPERT), _F32), pltpu.VMEM((D_EXPERT, D_MODEL), _F32),
                        pltpu.VMEM((D_MODEL, 2 * D_EXPERT), _F8), pltpu.VMEM((D_EXPERT, D_MODEL), _F8),
                        pltpu.VMEM((M, D_EXPERT), _F8), pltpu.VMEM((M, D_EXPERT), _F8),
                        pltpu.VMEM((M, LANES), _F32), pltpu.VMEM((M, LANES), _F32),
                        pltpu.VMEM((SUBLANES, LANES), _F32),
                        pltpu.SemaphoreType.DMA((2,))],
    )
    return pl.pallas_call(
        _expert_kernel,
        grid_spec=grid_spec,
        out_shape=jax.ShapeDtypeStruct(xs.shape, jnp.uint32),
        compiler_params=pltpu.CompilerParams(
            dimension_semantics=("arbitrary",), vmem_limit_bytes=VMEM_LIMIT),
        name="experts",
    )(blk_e, nb_used, next_expert, xs, w1, b1, w2, b2)


def _combine_kernel(h1_ref, gate_ref, gfin_ref, y0_ref, y1_ref, y2_ref, y3_ref, *out_refs):
    out_ref = out_refs[-1]
    acc = h1_ref[...]
    g = gate_ref[...]
    gates = jnp.concatenate([g, jnp.zeros_like(g)], axis=0).T
    for k, y_ref in enumerate((y0_ref, y1_ref, y2_ref, y3_ref)):
        acc = acc + gates[:, k:k + 1] * _unpack_bf16_pairs(y_ref[...])
    out_ref[...] = _rms(acc, gfin_ref[...])


def _combine_call(h1, gates, gfin, yg, out_prev, tile0, n_total):
    N = h1.shape[0]
    T = COMBINE_TILE
    nT = N // T
    y_specs = [pl.BlockSpec((T, D_MODEL // 2), functools.partial(lambda i, k: (k * nT + i, 0), k=k))
               for k in range(TOP_K)]
    in_specs = [
        pl.BlockSpec((T, D_MODEL), lambda i: (i, 0)),
        pl.BlockSpec((TOP_K, T), lambda i: (0, i)),
        pl.BlockSpec((1, D_MODEL), lambda i: (0, 0)),
    ] + y_specs
    args = [h1, gates, gfin, yg, yg, yg, yg]
    aliases = {}
    if out_prev is not None:
        in_specs.append(pl.BlockSpec(memory_space=pl.ANY))
        args.append(out_prev)
        aliases = {len(args) - 1: 0}
    return pl.pallas_call(
        _combine_kernel,
        grid=(nT,),
        in_specs=in_specs,
        out_specs=pl.BlockSpec((T, D_MODEL), lambda i: (tile0 + i, 0)),
        out_shape=jax.ShapeDtypeStruct((n_total, D_MODEL), _F32),
        input_output_aliases=aliases,
        compiler_params=pltpu.CompilerParams(dimension_semantics=("arbitrary",)),
        name="combine",
    )(*args)


def _hgrn_level_table():
    t = np.arange(HGRN_CHUNK)[:, None]
    u = np.arange(HGRN_CHUNK)[None, :]
    top = np.floor(np.log2(np.maximum(t ^ u, 1))).astype(np.int32)
    return np.where(t > u, top, -1).astype(np.int32)


def _rope_tables(S):
    half = HEAD_DIM // 2
    inv_freq = ROPE_THETA ** (-(jnp.arange(half, dtype=_F32) * 2.0 / HEAD_DIM))
    ang = jnp.arange(S, dtype=_F32)[:, None] * inv_freq[None, :]
    cos = jnp.tile(jnp.cos(ang), (1, 4))
    sin = jnp.tile(jnp.sin(ang), (1, 4))
    sign = jnp.where(jnp.arange(LANES) % HEAD_DIM < ROT_HALF, -1.0, 1.0).astype(_F32)
    return cos, sin * sign[None, :]


def kernel(x, norm_mix_g, w_in, b_in, attn_sinks, hgrn_lb, hgrn_norm_g, w_up_attn, w_up_hgrn, w_out,
           norm_ffn_g, w_router, b_router, w_moe1, b_moe1, w_moe2, b_moe2, norm_final_g):
    B, S, D = x.shape
    N = B * S
    assert D == D_MODEL and S % SEQ_TILE == 0 and SEQ_TILE % WINDOW == 0
    assert B % BATCH_PARTS == 0 and (N // BATCH_PARTS) % (SC_WORKERS * SC_CHUNK) == 0
    assert (N // BATCH_PARTS) % COMBINE_TILE == 0 and (N // BATCH_PARTS) % DEST_TILE == 0
    assert norm_mix_g.shape[0] == 1 and hgrn_lb.shape[0] == 2
    assert GROUP == 4 and N_KV_HEADS == 2 and 2 * HEAD_DIM == LANES

    cos_t, sin_t = _rope_tables(S)
    weights = (attn_sinks[0], cos_t, sin_t, norm_mix_g,
               w_in[0].astype(_BF), b_in, hgrn_lb, hgrn_norm_g, jnp.asarray(_hgrn_level_table()),
               w_up_attn[0].astype(_BF), w_up_hgrn[0].astype(_BF), w_out[0].astype(_BF),
               norm_ffn_g, w_router[0].T, b_router[0][:, None])

    Bp = B // BATCH_PARTS
    Np = Bp * S
    n_blocks = (Np * TOP_K) // MOE_BLOCK + N_EXPERTS
    out = None
    for part in range(BATCH_PARTS):
        h1, xp, idx, rank, gates, cnt = _mixer_call(part * Bp, Bp, x, *weights)

        counts = cnt[:, 0].astype(jnp.int32)
        padded = ((counts + MOE_BLOCK - 1) // MOE_BLOCK) * MOE_BLOCK
        pad_ends = jnp.cumsum(padded)
        pad_starts = pad_ends - padded
        dest = _dest_call(pad_starts, idx, rank)
        blk_start = jnp.arange(n_blocks, dtype=jnp.int32) * MOE_BLOCK
        blk_e = jnp.minimum(jnp.sum((pad_ends[None, :] <= blk_start[:, None]).astype(jnp.int32), axis=1),
                            N_EXPERTS - 1)
        nb_used = (pad_ends[-1:] // MOE_BLOCK).astype(jnp.int32)
        e_ids = jnp.arange(N_EXPERTS, dtype=jnp.int32)
        later = jnp.logical_and(e_ids[None, :] > e_ids[:, None], counts[None, :] > 0)
        next_expert = jnp.min(jnp.where(later, e_ids[None, :], N_EXPERTS), axis=1)
        next_expert = jnp.where(next_expert < N_EXPERTS, next_expert, -1).astype(jnp.int32)

        xs = _sc_scatter_rows(xp.reshape(Np, D // 2), dest, n_blocks * MOE_BLOCK)
        ys = _expert_call(blk_e, nb_used, next_expert, xs, w_moe1[0], b_moe1[0][:, None, :], w_moe2[0], b_moe2[0][:, None, :])
        yg = _sc_gather_rows(ys, dest.reshape(TOP_K * Np))
        out = _combine_call(h1.reshape(Np, D), gates, norm_final_g[None, :], yg, out,
                            part * (Np // COMBINE_TILE), N)
    return out.reshape(B, S, D)
```

```python
import functools

import numpy as np
import jax
import jax.numpy as jnp
from jax import lax
from jax.experimental import pallas as pl
from jax.experimental.pallas import tpu as pltpu
from jax.experimental.pallas import tpu_sc as plsc

D_MODEL = 1024
HEAD_DIM = 64
ROT_HALF = HEAD_DIM // 2
N_Q_HEADS = 8
N_KV_HEADS = 2
GROUP = N_Q_HEADS // N_KV_HEADS
ATTN_WIDTH = N_Q_HEADS * HEAD_DIM
KV_WIDTH = N_KV_HEADS * HEAD_DIM
WINDOW = 128
ROPE_THETA = 10000.0
HGRN_HEADS = 4
HGRN_DK = 128
HGRN_WIDTH = HGRN_HEADS * HGRN_DK
N_EXPERTS = 32
TOP_K = 4
D_EXPERT = 1024
SWIGLU_ALPHA = 1.702
SWIGLU_LIMIT = 7.0
MOE_BLOCK = 512
NORM_EPS = 1e-5

_OFF_Q = 0
_OFF_K = _OFF_Q + ATTN_WIDTH
_OFF_V = _OFF_K + KV_WIDTH
_OFF_HQ = _OFF_V + KV_WIDTH
_OFF_HF = _OFF_HQ + HGRN_WIDTH
_OFF_HI = _OFF_HF + HGRN_WIDTH
_OFF_HG = _OFF_HI + HGRN_WIDTH
_OFF_GA = _OFF_HG + HGRN_WIDTH
_OFF_GH = _OFF_GA + D_MODEL
IN_COLS = _OFF_GH + D_MODEL

LANES = 128
SUBLANES = 8
SEQ_TILE = 512
HGRN_CHUNK = 256
GATE_JOB_COLS = 256
COMBINE_TILE = 512
DEST_TILE = 8192
BATCH_PARTS = 2
WEIGHT_DMA_PRIORITY = 1
SC_CORES = 2
SC_WORKERS = 32
SC_CHUNK = 128
VMEM_LIMIT = 56 * 1024 * 1024

_BF = jnp.bfloat16
_F8 = jnp.float8_e4m3fn
FP8_TOP = 224.0
FP8_TINY = 1e-30
_F32 = jnp.float32


def _dot(a, b):
    return jnp.dot(a, b, preferred_element_type=_F32)


def _dot_nt(a, b):
    return lax.dot_general(a, b, (((1,), (1,)), ((), ())), preferred_element_type=_F32)


def _dot_tn(a, b):
    return lax.dot_general(a, b, (((0,), (0,)), ((), ())), preferred_element_type=_F32)


def _rowsum(x):
    return jnp.sum(x, axis=1, keepdims=True)


def _rms(x, g):
    ms = _rowsum(x * x) * (1.0 / x.shape[1])
    return x * lax.rsqrt(ms + NORM_EPS) * g


def _fp8_scaled(v):
    amax = jnp.max(jnp.max(jnp.abs(v.astype(_F32)), axis=1, keepdims=True), axis=0, keepdims=True)
    s = jnp.exp2(jnp.floor(jnp.log2(FP8_TOP / jnp.maximum(amax, FP8_TINY))))
    return (v.astype(_F32) * s).astype(_F8), 1.0 / s


def _fp8_rows(v):
    amax = jnp.max(jnp.abs(v), axis=1, keepdims=True)
    s = jnp.exp2(jnp.floor(jnp.log2(FP8_TOP / jnp.maximum(amax, FP8_TINY))))
    return (v * s).astype(_F8), 1.0 / s


def _pack_bf16_pairs(x):
    n = x.shape[1] // 2
    lo = lax.bitcast_convert_type(x[:, :n].astype(_BF).astype(_F32), jnp.uint32)
    hi = lax.bitcast_convert_type(x[:, n:].astype(_BF).astype(_F32), jnp.uint32)
    return (lo >> 16) | (hi & jnp.uint32(0xFFFF0000))


def _unpack_bf16_pairs(u):
    lo = lax.bitcast_convert_type(u << 16, _F32)
    hi = lax.bitcast_convert_type(u & jnp.uint32(0xFFFF0000), _F32)
    return jnp.concatenate([lo, hi], axis=1)


def _mixer_kernel(sinks_ref, x_ref, cos_ref, sin_ref, gmix_ref, w_in_ref, b_in_ref, lb_ref,
                  hgn_ref, lvl_ref, w_ua_ref, w_uh_ref, w_out_ref, gffn_ref, w_rt_ref, b_r_ref,
                  h1_ref, xp_ref, idx_ref, rank_ref, gate_ref, cnt_ref,
                  kc_ref, vc_ref, st_ref, cnt_sc, ya_ref, o_ref):
    T = SEQ_TILE
    b = pl.program_id(0)
    s = pl.program_id(1)

    @pl.when(jnp.logical_and(b == 0, s == 0))
    def _():
        cnt_sc[...] = jnp.zeros_like(cnt_sc)

    @pl.when(s == 0)
    def _():
        kc_ref[...] = jnp.zeros_like(kc_ref)
        vc_ref[...] = jnp.zeros_like(vc_ref)
        st_ref[...] = jnp.zeros_like(st_ref)

    x = x_ref[0]
    xn = _rms(x, gmix_ref[...]).astype(_BF)

    def proj(off, width):
        return _dot(xn, w_in_ref[:, off:off + width]) + b_in_ref[:, off:off + width]

    cos = cos_ref[...]
    sin = sin_ref[...]

    lane = lax.broadcasted_iota(jnp.int32, (1, LANES), 1)
    first_half = (lane % HEAD_DIM) < ROT_HALF

    def rope(t):
        swapped = jnp.where(first_half, pltpu.roll(t, LANES - ROT_HALF, axis=1), pltpu.roll(t, ROT_HALF, axis=1))
        return t * cos + swapped * sin

    zq = proj(_OFF_Q, ATTN_WIDTH)
    scale = HEAD_DIM ** -0.5
    q_pairs = [(rope(zq[:, LANES * p:LANES * (p + 1)]) * scale).astype(_BF) for p in range(N_Q_HEADS // 2)]
    k_rot = rope(proj(_OFF_K, KV_WIDTH))
    v_new = proj(_OFF_V, KV_WIDTH)

    k_ext = jnp.concatenate([kc_ref[...], k_rot], axis=0)
    v_ext = jnp.concatenate([vc_ref[...], v_new], axis=0)
    kc_ref[...] = k_rot[T - WINDOW:, :]
    vc_ref[...] = v_new[T - WINDOW:, :]

    lane_lo = lane < HEAD_DIM
    k_r64 = pltpu.roll(k_ext, HEAD_DIM, axis=1)
    v_r64 = pltpu.roll(v_ext, HEAD_DIM, axis=1)
    zero = jnp.zeros_like(k_ext)
    k_var = [(jnp.where(lane_lo, k_ext, zero).astype(_BF), jnp.where(lane_lo, zero, k_r64).astype(_BF)),
             (jnp.where(lane_lo, k_r64, zero).astype(_BF), jnp.where(lane_lo, zero, k_ext).astype(_BF))]
    v_var = [(jnp.where(lane_lo, v_ext, zero).astype(_BF), jnp.where(lane_lo, zero, v_r64).astype(_BF)),
             (jnp.where(lane_lo, v_r64, zero).astype(_BF), jnp.where(lane_lo, zero, v_ext).astype(_BF))]

    qi = lax.broadcasted_iota(jnp.int32, (2 * WINDOW, 2 * WINDOW), 0) % WINDOW
    kj = lax.broadcasted_iota(jnp.int32, (2 * WINDOW, 2 * WINDOW), 1)
    band = jnp.logical_and(kj > qi, kj <= qi + WINDOW)
    row_top = lax.broadcasted_iota(jnp.int32, (2 * WINDOW, 1), 0) < WINDOW
    neg_inf = jnp.float32(-jnp.inf)

    gate_cols = [(_OFF_GA + c, GATE_JOB_COLS) for c in range(0, 2 * D_MODEL, GATE_JOB_COLS)]
    gate_parts = []
    jobs_per_unit = -(-len(gate_cols) // ((T // WINDOW) * N_KV_HEADS))

    for n in range(T // WINDOW):
        if n == 0:
            ok = jnp.logical_and(band, jnp.logical_or(kj >= WINDOW, s > 0))
        else:
            ok = band
        r0 = n * WINDOW
        for j in range(N_KV_HEADS):
            q2 = jnp.concatenate([q_pairs[2 * j][r0:r0 + WINDOW], q_pairs[2 * j + 1][r0:r0 + WINDOW]], axis=0)
            kcat = jnp.concatenate([k_var[j][0][r0:r0 + 2 * WINDOW], k_var[j][1][r0:r0 + 2 * WINDOW]], axis=0)
            vcat = jnp.concatenate([v_var[j][0][r0:r0 + 2 * WINDOW], v_var[j][1][r0:r0 + 2 * WINDOW]], axis=0)
            sc = _dot_nt(q2, kcat)
            ps, rs = [], []
            for half in range(2):
                sh = jnp.where(ok, sc[:, 2 * WINDOW * half:2 * WINDOW * (half + 1)], neg_inf)
                snk = jnp.where(row_top, sinks_ref[GROUP * j + half], sinks_ref[GROUP * j + 2 + half])
                m = jnp.maximum(jnp.max(sh, axis=1, keepdims=True), snk)
                p = jnp.exp(sh - m)
                den = _rowsum(p) + jnp.exp(snk - m)
                ps.append(p.astype(_BF))
                rs.append(1.0 / den)
            o = _dot(jnp.concatenate(ps, axis=1), vcat)
            o = o * jnp.where(lane_lo, rs[0], rs[1])
            ya_ref[r0:r0 + WINDOW, LANES * (2 * j):LANES * (2 * j + 1)] = o[:WINDOW]
            ya_ref[r0:r0 + WINDOW, LANES * (2 * j + 1):LANES * (2 * j + 2)] = o[WINDOW:]
            for _ in range(min(jobs_per_unit, len(gate_cols) - len(gate_parts))):
                gate_parts.append(proj(*gate_cols[len(gate_parts)]))

    up_a = _dot(ya_ref[...].astype(_BF), w_ua_ref[...])

    HW = HGRN_WIDTH
    a0 = lb_ref[0:1, :]
    a1 = lb_ref[1:2, :]
    am = jnp.maximum(a0, a1)
    e0 = jnp.exp(a0 - am)
    lb = e0 / (e0 + jnp.exp(a1 - am))
    f = lb + (1.0 - lb) * jax.nn.sigmoid(proj(_OFF_HF, HW))
    kk = 1.0 - f
    g = jnp.log2(f)
    hq = proj(_OFF_HQ, HW)
    hv = proj(_OFF_HI, HW)

    row = lax.broadcasted_iota(jnp.int32, (T, 1), 0)
    C = HGRN_CHUNK

    def rdown(t, d):
        return pltpu.roll(t.reshape(T // SUBLANES, SUBLANES, HW), d, axis=1).reshape(T, HW)

    def rup(t, d):
        return pltpu.roll(t.reshape(T // SUBLANES, SUBLANES, HW), SUBLANES - d, axis=1).reshape(T, HW)

    def head(t, h):
        return t[:, HGRN_DK * h:HGRN_DK * (h + 1)]

    qk0 = hq * kk
    o_parts = [_rowsum(head(qk0, h)) * head(hv, h) for h in range(HGRN_HEADS)]

    small_levels = []
    bsz = 1
    lb_cum, lb_tot = g, g
    while bsz < SUBLANES:
        odd = (row % (2 * bsz)) >= bsz
        ex = jnp.exp2(jnp.where(odd, lb_cum, lb_tot - lb_cum))
        small_levels.append(((hq * ex).astype(_BF), (kk * ex).astype(_BF)))
        prev_tot = rdown(lb_tot, bsz)
        lb_cum = lb_cum + jnp.where(odd, prev_tot, 0.0)
        lb_tot = lb_tot + jnp.where(odd, prev_tot, rup(lb_tot, bsz))
        bsz *= 2

    cum = [lb_cum[r:r + bsz] for r in range(0, T, bsz)]
    tot = [lb_tot[r:r + 1] for r in range(0, T, bsz)]
    kk_b = kk.astype(_BF)
    big_levels = []
    while bsz < C:
        q_rows, k_rows = [], []
        for j, r in enumerate(range(0, T, bsz)):
            if j % 2:
                q_rows.append(hq[r:r + bsz] * jnp.exp2(cum[j]))
                k_rows.append(kk_b[r:r + bsz])
            else:
                k_rows.append((kk[r:r + bsz] * jnp.exp2(tot[j] - cum[j])).astype(_BF))
        big_levels.append((bsz, jnp.concatenate(q_rows, axis=0).astype(_BF), jnp.concatenate(k_rows, axis=0)))
        cum = [jnp.concatenate([cum[j], cum[j + 1] + tot[j]], axis=0) for j in range(0, len(cum), 2)]
        tot = [tot[j] + tot[j + 1] for j in range(0, len(tot), 2)]
        bsz *= 2
    hv_b = hv.astype(_BF)
    lvl = lvl_ref[...]

    for c in range(T // C):
        rs_ = slice(c * C, (c + 1) * C)
        rh_ = slice(c * C // 2, (c + 1) * C // 2)
        q_in = (hq[rs_] * jnp.exp2(cum[c])).astype(_BF)
        k_st = (kk[rs_] * jnp.exp2(tot[c] - cum[c])).astype(_BF)
        dec = jnp.exp2(tot[c])
        for h in range(HGRN_HEADS):
            cs_ = slice(HGRN_DK * h, HGRN_DK * (h + 1))
            st = st_ref[h]
            vc = hv_b[rs_, cs_]
            inter = _dot_nt(q_in[:, cs_], st.astype(_BF))
            zero_tile = jnp.zeros((SUBLANES, LANES), _F32)
            tiles = [[zero_tile for _ in range(C // LANES)] for _ in range(C // SUBLANES)]

            def place(li, p, p_row, rb, col):
                r, ct = rb * SUBLANES, col // LANES
                cs = slice(ct * LANES, (ct + 1) * LANES)
                tiles[rb][ct] = jnp.where(lvl[r:r + SUBLANES, cs] == li, p[p_row:p_row + SUBLANES, cs], tiles[rb][ct])

            for li, (ql, kl) in enumerate(small_levels):
                p = _dot_nt(ql[rs_, cs_], kl[rs_, cs_])
                for rb in range(C // SUBLANES):
                    place(li, p, rb * SUBLANES, rb, rb * SUBLANES)
            for li, (bs, ql, kl) in enumerate(big_levels, start=len(small_levels)):
                p = _dot_nt(ql[rh_, cs_], kl[rs_, cs_])
                for rb in range(C // SUBLANES):
                    blk = (rb * SUBLANES) // bs
                    if blk % 2:
                        place(li, p, rb * SUBLANES - (blk + 1) // 2 * bs, rb, (blk - 1) * bs)
            amat = jnp.concatenate([jnp.concatenate(row_tiles, axis=1) for row_tiles in tiles], axis=0)
            intra = _dot(amat.astype(_BF), vc)
            st_ref[h] = st * dec[:, cs_] + _dot_tn(vc, k_st[:, cs_])
            o_ref[rs_, cs_] = inter + intra + o_parts[h][rs_]

    o = o_ref[...]
    hg = proj(_OFF_HG, HW)
    yh_parts = []
    for h in range(HGRN_HEADS):
        oh = head(o, h)
        ms = _rowsum(oh * oh) * (1.0 / HGRN_DK)
        yh_parts.append(oh * lax.rsqrt(ms + NORM_EPS))
    yh = jnp.concatenate(yh_parts, axis=1) * hgn_ref[...]
    yh = (yh * (hg * jax.nn.sigmoid(hg))).astype(_BF)

    up_h = _dot(yh, w_uh_ref[...])
    half = len(gate_parts) // 2
    z_ga = jnp.concatenate(gate_parts[:half], axis=1)
    z_gh = jnp.concatenate(gate_parts[half:], axis=1)
    merged = jax.nn.sigmoid(z_ga) * up_a + jax.nn.sigmoid(z_gh) * up_h
    h1 = x + _dot(merged.astype(_BF), w_out_ref[...])
    h1_ref[0] = h1

    xn2 = _rms(h1, gffn_ref[...])
    xp_ref[0] = _pack_bf16_pairs(xn2)
    x_hi = xn2.astype(_BF)
    x_lo = (xn2 - x_hi.astype(_F32)).astype(_BF)
    w_rt = w_rt_ref[...]
    w_hi = w_rt.astype(_BF)
    w_lo = (w_rt - w_hi.astype(_F32)).astype(_BF)
    logits = (_dot_nt(w_hi, x_hi) + (_dot_nt(w_hi, x_lo) + _dot_nt(w_lo, x_hi))) + b_r_ref[...]
    eidx = lax.broadcasted_iota(jnp.int32, (N_EXPERTS, T), 0)
    vals, sels, ohs = [], [], []
    l = logits
    for _ in range(TOP_K):
        m = jnp.max(l, axis=0, keepdims=True)
        sel = jnp.min(jnp.where(l == m, eidx, N_EXPERTS), axis=0, keepdims=True)
        oh = eidx == sel
        vals.append(m)
        sels.append(sel)
        ohs.append(oh)
        l = jnp.where(oh, neg_inf, l)
    es = [jnp.exp(v - vals[0]) for v in vals]
    den = es[0] + es[1] + es[2] + es[3]
    chosen = jnp.zeros((N_EXPERTS, T), _F32)
    for oh in ohs:
        chosen = chosen + jnp.where(oh, 1.0, 0.0)
    ui = lax.broadcasted_iota(jnp.int32, (T, T), 0)
    uj = lax.broadcasted_iota(jnp.int32, (T, T), 1)
    upper = jnp.where(ui < uj, 1.0, 0.0).astype(_BF)
    before = _dot(chosen.astype(_BF), upper) + cnt_sc[:, 0:1]
    for k in range(TOP_K):
        idx_ref[k:k + 1, :] = sels[k]
        gate_ref[k:k + 1, :] = es[k] / den
        rank_ref[k:k + 1, :] = jnp.sum(jnp.where(ohs[k], before, 0.0), axis=0, keepdims=True).astype(jnp.int32)
    cnt_sc[...] = cnt_sc[...] + _rowsum(chosen)
    cnt_ref[...] = cnt_sc[...]


def _const_spec(shape):
    return pl.BlockSpec(shape, lambda b, s: (0,) * len(shape), pipeline_mode=pl.Buffered(1))


def _mixer_call(batch0, B, x, sinks, cos_t, sin_t, gmix, w_in, b_in, lb, hgn, lvl, w_ua, w_uh, w_out, gffn, w_rt, b_r):
    S = x.shape[1]
    T = SEQ_TILE
    N = B * S
    nS = S // T
    tok_spec = pl.BlockSpec((TOP_K, T), lambda b, s: (0, b * nS + s))
    out_shape = (
        jax.ShapeDtypeStruct((B, S, D_MODEL), _F32),
        jax.ShapeDtypeStruct((B, S, D_MODEL // 2), jnp.uint32),
        jax.ShapeDtypeStruct((TOP_K, N), jnp.int32),
        jax.ShapeDtypeStruct((TOP_K, N), jnp.int32),
        jax.ShapeDtypeStruct((TOP_K, N), _F32),
        jax.ShapeDtypeStruct((N_EXPERTS, LANES), _F32),
    )
    in_specs = [
        pl.BlockSpec(memory_space=pltpu.SMEM),
        pl.BlockSpec((1, T, D_MODEL), lambda b, s: (batch0 + b, s, 0)),
        pl.BlockSpec((T, LANES), lambda b, s: (s, 0)),
        pl.BlockSpec((T, LANES), lambda b, s: (s, 0)),
        _const_spec((1, D_MODEL)),
        _const_spec((D_MODEL, IN_COLS)),
        _const_spec((1, IN_COLS)),
        _const_spec((2, HGRN_WIDTH)),
        _const_spec((1, HGRN_WIDTH)),
        _const_spec((HGRN_CHUNK, HGRN_CHUNK)),
        _const_spec((ATTN_WIDTH, D_MODEL)),
        _const_spec((HGRN_WIDTH, D_MODEL)),
        _const_spec((D_MODEL, D_MODEL)),
        _const_spec((1, D_MODEL)),
        _const_spec((N_EXPERTS, D_MODEL)),
        _const_spec((N_EXPERTS, 1)),
    ]
    out_specs = (
        pl.BlockSpec((1, T, D_MODEL), lambda b, s: (b, s, 0)),
        pl.BlockSpec((1, T, D_MODEL // 2), lambda b, s: (b, s, 0)),
        tok_spec, tok_spec, tok_spec,
        pl.BlockSpec((N_EXPERTS, LANES), lambda b, s: (0, 0)),
    )
    scratch = [
        pltpu.VMEM((WINDOW, KV_WIDTH), _F32),
        pltpu.VMEM((WINDOW, KV_WIDTH), _F32),
        pltpu.VMEM((HGRN_HEADS, HGRN_DK, HGRN_DK), _F32),
        pltpu.VMEM((N_EXPERTS, LANES), _F32),
        pltpu.VMEM((T, ATTN_WIDTH), _F32),
        pltpu.VMEM((T, HGRN_WIDTH), _F32),
    ]
    return pl.pallas_call(
        _mixer_kernel,
        grid=(B, nS),
        in_specs=in_specs,
        out_specs=out_specs,
        out_shape=out_shape,
        scratch_shapes=scratch,
        compiler_params=pltpu.CompilerParams(
            dimension_semantics=("arbitrary", "arbitrary"), vmem_limit_bytes=VMEM_LIMIT),
        name="mixer",
    )(sinks, x, cos_t, sin_t, gmix, w_in, b_in, lb, hgn, lvl, w_ua, w_uh, w_out, gffn, w_rt, b_r)


def _dest_kernel(start_ref, idx_ref, rank_ref, dest_ref):
    idx = idx_ref[...]
    dest = rank_ref[...]
    for e in range(N_EXPERTS):
        dest = dest + jnp.where(idx == e, start_ref[e], 0)
    dest_ref[...] = dest


def _dest_call(pad_starts, idx, rank):
    K, N = idx.shape
    spec = pl.BlockSpec((K, DEST_TILE), lambda i: (0, i))
    return pl.pallas_call(
        _dest_kernel,
        grid=(N // DEST_TILE,),
        in_specs=[pl.BlockSpec(memory_space=pltpu.SMEM), spec, spec],
        out_specs=spec,
        out_shape=jax.ShapeDtypeStruct((K, N), jnp.int32),
        compiler_params=pltpu.CompilerParams(dimension_semantics=("arbitrary",)),
        name="dest",
    )(pad_starts, idx, rank)


def _sc_mesh():
    return plsc.VectorSubcoreMesh(core_axis_name="c", subcore_axis_name="s")


def _sc_worker_id():
    return lax.axis_index("s") * SC_CORES + lax.axis_index("c")


def _sc_scatter_rows(x, dest, n_rows):
    N, C = x.shape
    K = dest.shape[0]
    n_chunks = N // SC_CHUNK
    per_worker = n_chunks // SC_WORKERS
    dest3 = dest.reshape(K, n_chunks, SC_CHUNK)

    @functools.partial(
        pl.kernel, out_type=jax.ShapeDtypeStruct((n_rows, C), x.dtype), mesh=_sc_mesh(),
        scratch_types=[pltpu.VMEM((K, SC_CHUNK), jnp.int32), pltpu.VMEM((SC_CHUNK, C), x.dtype),
                       pltpu.SemaphoreType.DMA])
    def scatter(x_hbm, i_hbm, o_hbm, idx_v, rows_v, sem):
        base = _sc_worker_id() * per_worker

        @pl.loop(0, per_worker)
        def _(j):
            c = base + j
            off = pl.multiple_of(c * SC_CHUNK, SC_CHUNK)
            for k in range(K):
                pltpu.sync_copy(i_hbm.at[k, c], idx_v.at[k])
            pltpu.sync_copy(x_hbm.at[pl.ds(off, SC_CHUNK)], rows_v)
            for k in range(K):
                pltpu.async_copy(rows_v, o_hbm.at[idx_v.at[k]], sem).wait()

    return scatter(x, dest3)


def _sc_gather_rows(table, idx):
    M = idx.shape[0]
    C = table.shape[1]
    per_worker = M // SC_WORKERS
    n_chunks = per_worker // SC_CHUNK

    @functools.partial(
        pl.kernel, out_type=jax.ShapeDtypeStruct((M, C), table.dtype), mesh=_sc_mesh(),
        scratch_types=[pltpu.VMEM((SC_CHUNK,), jnp.int32), pltpu.VMEM((SC_CHUNK, C), table.dtype),
                       pltpu.SemaphoreType.DMA])
    def gather(t_hbm, i_hbm, o_hbm, idx_v, rows_v, sem):
        base = _sc_worker_id() * per_worker

        @pl.loop(0, n_chunks)
        def _(j):
            off = pl.multiple_of(base + j * SC_CHUNK, SC_CHUNK)
            pltpu.sync_copy(i_hbm.at[pl.ds(off, SC_CHUNK)], idx_v)
            pltpu.async_copy(t_hbm.at[idx_v], rows_v, sem).wait()
            pltpu.sync_copy(rows_v, o_hbm.at[pl.ds(off, SC_CHUNK)])

    return gather(table, idx)


def _expert_kernel(blk_e_ref, nb_ref, next_ref, xs_ref, w1_hbm, b1_ref, w2_hbm, b2_ref, ys_ref,
                   w1f_ref, w2f_ref, w1b_ref, w2b_ref, act0_ref, act1_ref, asc0_ref, asc1_ref, sc_ref, sems):
    i = pl.program_id(0)
    nb = nb_ref[0]
    last = nb - 1

    def expert_of(j):
        return blk_e_ref[jnp.clip(j, 0, last)]

    def first_block(j):
        return jnp.logical_or(j == 0, expert_of(j) != expert_of(j - 1))

    def w1_copy(expert):
        return pltpu.make_async_copy(w1_hbm.at[expert], w1f_ref, sems.at[0])

    def w2_copy(expert):
        return pltpu.make_async_copy(w2_hbm.at[expert], w2f_ref, sems.at[1])

    @pl.when(i == 0)
    def _():
        w1_copy(expert_of(0)).start()
        w2_copy(expert_of(0)).start()
        act1_ref[...] = jnp.zeros_like(act1_ref)
        w2b_ref[...] = jnp.zeros_like(w2b_ref)
        sc_ref[...] = jnp.zeros_like(sc_ref)
        asc1_ref[...] = jnp.zeros_like(asc1_ref)

    @pl.when(jnp.logical_and(i < nb, first_block(i)))
    def _():
        e = expert_of(i)
        w1_copy(e).wait()
        w8, inv = _fp8_scaled(w1f_ref[...])
        w1b_ref[...] = w8
        sc_ref[0:1, :] = jnp.broadcast_to(inv, (1, LANES))
        nxt = next_ref[e]

        @pl.when(nxt >= 0)
        def _():
            w1_copy(nxt).start(priority=WEIGHT_DMA_PRIORITY)

    @pl.when(jnp.logical_and(jnp.logical_and(i >= 1, i <= nb), first_block(i - 1)))
    def _():
        e = expert_of(i - 1)
        w2_copy(e).wait()
        w8, inv = _fp8_scaled(w2f_ref[...])
        w2b_ref[...] = w8
        sc_ref[1:2, :] = jnp.broadcast_to(inv, (1, LANES))
        nxt = next_ref[e]

        @pl.when(nxt >= 0)
        def _():
            w2_copy(nxt).start(priority=WEIGHT_DMA_PRIORITY)

    def both_halves(act_w_ref, asc_w_ref, act_r_ref, asc_r_ref):
        y = _dot(act_r_ref[...], w2b_ref[...]) * (asc_r_ref[:, 0:1] * sc_ref[1:2, 0:1]) + b2_ref[0]
        ys_ref[...] = _pack_bf16_pairs(y)
        x8, inv_x = _fp8_rows(_unpack_bf16_pairs(xs_ref[...]))
        h = _dot(x8, w1b_ref[...]) * (inv_x * sc_ref[0:1, 0:1]) + b1_ref[0]
        glu = jnp.minimum(h[:, :D_EXPERT], SWIGLU_LIMIT)
        lin = jnp.clip(h[:, D_EXPERT:], -SWIGLU_LIMIT, SWIGLU_LIMIT)
        a8, inv_a = _fp8_rows(glu * jax.nn.sigmoid(SWIGLU_ALPHA * glu) * (lin + 1.0))
        act_w_ref[...] = a8
        asc_w_ref[...] = jnp.broadcast_to(inv_a, asc_w_ref.shape)

    busy = i <= nb

    @pl.when(jnp.logical_and(busy, i % 2 == 0))
    def _():
        both_halves(act0_ref, asc0_ref, act1_ref, asc1_ref)

    @pl.when(jnp.logical_and(busy, i % 2 == 1))
    def _():
        both_halves(act1_ref, asc1_ref, act0_ref, asc0_ref)

    @pl.when(jnp.logical_not(busy))
    def _():
        ys_ref[...] = jnp.zeros_like(ys_ref)


def _expert_call(blk_e, nb_used, next_expert, xs, w1, b1, w2, b2):
    n_blocks = xs.shape[0] // MOE_BLOCK
    M = MOE_BLOCK
    cur = lambda i, nb: jnp.minimum(i, nb[0] - 1)
    prev = lambda i, nb: jnp.clip(i - 1, 0, nb[0] - 1)
    grid_spec = pltpu.PrefetchScalarGridSpec(
        num_scalar_prefetch=3,
        grid=(n_blocks + 1,),
        in_specs=[
            pl.BlockSpec((M, D_MODEL // 2), lambda i, be, nb, nx: (cur(i, nb), 0)),
            pl.BlockSpec(memory_space=pl.ANY),
            pl.BlockSpec((1, 1, 2 * D_EXPERT), lambda i, be, nb, nx: (be[cur(i, nb)], 0, 0)),
            pl.BlockSpec(memory_space=pl.ANY),
            pl.BlockSpec((1, 1, D_MODEL), lambda i, be, nb, nx: (be[prev(i, nb)], 0, 0)),
        ],
        out_specs=pl.BlockSpec((M, D_MODEL // 2), lambda i, be, nb, nx: (jnp.maximum(i - 1, 0), 0)),
        scratch_shapes=[pltpu.VMEM((D_MODEL, 2 * D_EXPERT), _F32), pltpu.VMEM((D_EXPERT, D_MODEL), _F32),
                        pltpu.VMEM((D_MODEL, 2 * D_EXPERT), _F8), pltpu.VMEM((D_EXPERT, D_MODEL), _F8),
                        pltpu.VMEM((M, D_EXPERT), _F8), pltpu.VMEM((M, D_EXPERT), _F8),
                        pltpu.VMEM((M, LANES), _F32), pltpu.VMEM((M, LANES), _F32),
                        pltpu.VMEM((SUBLANES, LANES), _F32),
                        pltpu.SemaphoreType.DMA((2,))],
    )
    return pl.pallas_call(
        _expert_kernel,
        grid_spec=grid_spec,
        out_shape=jax.ShapeDtypeStruct(xs.shape, jnp.uint32),
        compiler_params=pltpu.CompilerParams(
            dimension_semantics=("arbitrary",), vmem_limit_bytes=VMEM_LIMIT),
        name="experts",
    )(blk_e, nb_used, next_expert, xs, w1, b1, w2, b2)


def _combine_kernel(h1_ref, gate_ref, gfin_ref, y0_ref, y1_ref, y2_ref, y3_ref, *out_refs):
    out_ref = out_refs[-1]
    acc = h1_ref[...]
    g = gate_ref[...]
    gates = jnp.concatenate([g, jnp.zeros_like(g)], axis=0).T
    for k, y_ref in enumerate((y0_ref, y1_ref, y2_ref, y3_ref)):
        acc = acc + gates[:, k:k + 1] * _unpack_bf16_pairs(y_ref[...])
    out_ref[...] = _rms(acc, gfin_ref[...])


def _combine_call(h1, gates, gfin, yg, out_prev, tile0, n_total):
    N = h1.shape[0]
    T = COMBINE_TILE
    nT = N // T
    y_specs = [pl.BlockSpec((T, D_MODEL // 2), functools.partial(lambda i, k: (k * nT + i, 0), k=k))
               for k in range(TOP_K)]
    in_specs = [
        pl.BlockSpec((T, D_MODEL), lambda i: (i, 0)),
        pl.BlockSpec((TOP_K, T), lambda i: (0, i)),
        pl.BlockSpec((1, D_MODEL), lambda i: (0, 0)),
    ] + y_specs
    args = [h1, gates, gfin, yg, yg, yg, yg]
    aliases = {}
    if out_prev is not None:
        in_specs.append(pl.BlockSpec(memory_space=pl.ANY))
        args.append(out_prev)
        aliases = {len(args) - 1: 0}
    return pl.pallas_call(
        _combine_kernel,
        grid=(nT,),
        in_specs=in_specs,
        out_specs=pl.BlockSpec((T, D_MODEL), lambda i: (tile0 + i, 0)),
        out_shape=jax.ShapeDtypeStruct((n_total, D_MODEL), _F32),
        input_output_aliases=aliases,
        compiler_params=pltpu.CompilerParams(dimension_semantics=("arbitrary",)),
        name="combine",
    )(*args)


def _hgrn_level_table():
    t = np.arange(HGRN_CHUNK)[:, None]
    u = np.arange(HGRN_CHUNK)[None, :]
    top = np.floor(np.log2(np.maximum(t ^ u, 1))).astype(np.int32)
    return np.where(t > u, top, -1).astype(np.int32)


def _rope_tables(S):
    half = HEAD_DIM // 2
    inv_freq = ROPE_THETA ** (-(jnp.arange(half, dtype=_F32) * 2.0 / HEAD_DIM))
    ang = jnp.arange(S, dtype=_F32)[:, None] * inv_freq[None, :]
    cos = jnp.tile(jnp.cos(ang), (1, 4))
    sin = jnp.tile(jnp.sin(ang), (1, 4))
    sign = jnp.where(jnp.arange(LANES) % HEAD_DIM < ROT_HALF, -1.0, 1.0).astype(_F32)
    return cos, sin * sign[None, :]


def kernel(x, norm_mix_g, w_in, b_in, attn_sinks, hgrn_lb, hgrn_norm_g, w_up_attn, w_up_hgrn, w_out,
           norm_ffn_g, w_router, b_router, w_moe1, b_moe1, w_moe2, b_moe2, norm_final_g):
    B, S, D = x.shape
    N = B * S
    assert D == D_MODEL and S % SEQ_TILE == 0 and SEQ_TILE % WINDOW == 0
    assert B % BATCH_PARTS == 0 and (N // BATCH_PARTS) % (SC_WORKERS * SC_CHUNK) == 0
    assert (N // BATCH_PARTS) % COMBINE_TILE == 0 and (N // BATCH_PARTS) % DEST_TILE == 0
    assert norm_mix_g.shape[0] == 1 and hgrn_lb.shape[0] == 2
    assert GROUP == 4 and N_KV_HEADS == 2 and 2 * HEAD_DIM == LANES

    cos_t, sin_t = _rope_tables(S)
    weights = (attn_sinks[0], cos_t, sin_t, norm_mix_g,
               w_in[0].astype(_BF), b_in, hgrn_lb, hgrn_norm_g, jnp.asarray(_hgrn_level_table()),
               w_up_attn[0].astype(_BF), w_up_hgrn[0].astype(_BF), w_out[0].astype(_BF),
               norm_ffn_g, w_router[0].T, b_router[0][:, None])

    Bp = B // BATCH_PARTS
    Np = Bp * S
    n_blocks = (Np * TOP_K) // MOE_BLOCK + N_EXPERTS
    out = None
    for part in range(BATCH_PARTS):
        h1, xp, idx, rank, gates, cnt = _mixer_call(part * Bp, Bp, x, *weights)

        counts = cnt[:, 0].astype(jnp.int32)
        padded = ((counts + MOE_BLOCK - 1) // MOE_BLOCK) * MOE_BLOCK
        pad_ends = jnp.cumsum(padded)
        pad_starts = pad_ends - padded
        dest = _dest_call(pad_starts, idx, rank)
        blk_start = jnp.arange(n_blocks, dtype=jnp.int32) * MOE_BLOCK
        blk_e = jnp.minimum(jnp.sum((pad_ends[None, :] <= blk_start[:, None]).astype(jnp.int32), axis=1),
                            N_EXPERTS - 1)
        nb_used = (pad_ends[-1:] // MOE_BLOCK).astype(jnp.int32)
        e_ids = jnp.arange(N_EXPERTS, dtype=jnp.int32)
        later = jnp.logical_and(e_ids[None, :] > e_ids[:, None], counts[None, :] > 0)
        next_expert = jnp.min(jnp.where(later, e_ids[None, :], N_EXPERTS), axis=1)
        next_expert = jnp.where(next_expert < N_EXPERTS, next_expert, -1).astype(jnp.int32)

        xs = _sc_scatter_rows(xp.reshape(Np, D // 2), dest, n_blocks * MOE_BLOCK)
        ys = _expert_call(blk_e, nb_used, next_expert, xs, w_moe1[0], b_moe1[0][:, None, :], w_moe2[0], b_moe2[0][:, None, :])
        yg = _sc_gather_rows(ys, dest.reshape(TOP_K * Np))
        out = _combine_call(h1.reshape(Np, D), gates, norm_final_g[None, :], yg, out,
                            part * (Np // COMBINE_TILE), N)
    return out.reshape(B, S, D)
```

```python
import functools

import numpy as np
import jax
import jax.numpy as jnp
from jax import lax
from jax.experimental import pallas as pl
from jax.experimental.pallas import tpu as pltpu
from jax.experimental.pallas import tpu_sc as plsc

D_MODEL = 1024
HEAD_DIM = 64
ROT_HALF = HEAD_DIM // 2
N_Q_HEADS = 8
N_KV_HEADS = 2
GROUP = N_Q_HEADS // N_KV_HEADS
ATTN_WIDTH = N_Q_HEADS * HEAD_DIM
KV_WIDTH = N_KV_HEADS * HEAD_DIM
WINDOW = 128
ROPE_THETA = 10000.0
HGRN_HEADS = 4
HGRN_DK = 128
HGRN_WIDTH = HGRN_HEADS * HGRN_DK
N_EXPERTS = 32
TOP_K = 4
D_EXPERT = 1024
SWIGLU_ALPHA = 1.702
SWIGLU_LIMIT = 7.0
MOE_BLOCK = 512
NORM_EPS = 1e-5

_OFF_Q = 0
_OFF_K = _OFF_Q + ATTN_WIDTH
_OFF_V = _OFF_K + KV_WIDTH
_OFF_HQ = _OFF_V + KV_WIDTH
_OFF_HF = _OFF_HQ + HGRN_WIDTH
_OFF_HI = _OFF_HF + HGRN_WIDTH
_OFF_HG = _OFF_HI + HGRN_WIDTH
_OFF_GA = _OFF_HG + HGRN_WIDTH
_OFF_GH = _OFF_GA + D_MODEL
IN_COLS = _OFF_GH + D_MODEL

LANES = 128
SUBLANES = 8
SEQ_TILE = 512
HGRN_CHUNK = 256
GATE_JOB_COLS = 256
COMBINE_TILE = 512
DEST_TILE = 8192
BATCH_PARTS = 2
WEIGHT_DMA_PRIORITY = 1
SC_CORES = 2
SC_WORKERS = 32
SC_CHUNK = 128
VMEM_LIMIT = 56 * 1024 * 1024

_BF = jnp.bfloat16
_F8 = jnp.float8_e4m3fn
FP8_TOP = 224.0
FP8_TINY = 1e-30
_F32 = jnp.float32


def _dot(a, b):
    return jnp.dot(a, b, preferred_element_type=_F32)


def _dot_nt(a, b):
    return lax.dot_general(a, b, (((1,), (1,)), ((), ())), preferred_element_type=_F32)


def _dot_tn(a, b):
    return lax.dot_general(a, b, (((0,), (0,)), ((), ())), preferred_element_type=_F32)


def _rowsum(x):
    return jnp.sum(x, axis=1, keepdims=True)


def _rms(x, g):
    ms = _rowsum(x * x) * (1.0 / x.shape[1])
    return x * lax.rsqrt(ms + NORM_EPS) * g


def _fp8_scaled(v):
    amax = jnp.max(jnp.max(jnp.abs(v.astype(_F32)), axis=1, keepdims=True), axis=0, keepdims=True)
    s = jnp.exp2(jnp.floor(jnp.log2(FP8_TOP / jnp.maximum(amax, FP8_TINY))))
    return (v.astype(_F32) * s).astype(_F8), 1.0 / s


def _fp8_rows(v):
    amax = jnp.max(jnp.abs(v), axis=1, keepdims=True)
    s = jnp.exp2(jnp.floor(jnp.log2(FP8_TOP / jnp.maximum(amax, FP8_TINY))))
    return (v * s).astype(_F8), 1.0 / s


def _pack_bf16_pairs(x):
    n = x.shape[1] // 2
    lo = lax.bitcast_convert_type(x[:, :n].astype(_BF).astype(_F32), jnp.uint32)
    hi = lax.bitcast_convert_type(x[:, n:].astype(_BF).astype(_F32), jnp.uint32)
    return (lo >> 16) | (hi & jnp.uint32(0xFFFF0000))


def _unpack_bf16_pairs(u):
    lo = lax.bitcast_convert_type(u << 16, _F32)
    hi = lax.bitcast_convert_type(u & jnp.uint32(0xFFFF0000), _F32)
    return jnp.concatenate([lo, hi], axis=1)


def _mixer_kernel(sinks_ref, x_ref, cos_ref, sin_ref, gmix_ref, w_in_ref, b_in_ref, lb_ref,
                  hgn_ref, lvl_ref, upper_ref, w_ua_ref, w_uh_ref, w_out_ref, gffn_ref, w_rt_ref, b_r_ref,
                  h1_ref, xp_ref, idx_ref, rank_ref, gate_ref, cnt_ref,
                  kc_ref, vc_ref, st_ref, cnt_sc, ya_ref, o_ref):
    T = SEQ_TILE
    b = pl.program_id(0)
    s = pl.program_id(1)

    @pl.when(jnp.logical_and(b == 0, s == 0))
    def _():
        cnt_sc[...] = jnp.zeros_like(cnt_sc)

    @pl.when(s == 0)
    def _():
        kc_ref[...] = jnp.zeros_like(kc_ref)
        vc_ref[...] = jnp.zeros_like(vc_ref)
        st_ref[...] = jnp.zeros_like(st_ref)

    x = x_ref[0]
    xn = _rms(x, gmix_ref[...]).astype(_BF)

    def proj(off, width):
        return _dot(xn, w_in_ref[:, off:off + width]) + b_in_ref[:, off:off + width]

    cos = cos_ref[...]
    sin = sin_ref[...]

    lane = lax.broadcasted_iota(jnp.int32, (1, LANES), 1)
    first_half = (lane % HEAD_DIM) < ROT_HALF

    def rope(t):
        swapped = jnp.where(first_half, pltpu.roll(t, LANES - ROT_HALF, axis=1), pltpu.roll(t, ROT_HALF, axis=1))
        return t * cos + swapped * sin

    zq = proj(_OFF_Q, ATTN_WIDTH)
    scale = HEAD_DIM ** -0.5
    q_pairs = [(rope(zq[:, LANES * p:LANES * (p + 1)]) * scale).astype(_BF) for p in range(N_Q_HEADS // 2)]
    k_rot = rope(proj(_OFF_K, KV_WIDTH))
    v_new = proj(_OFF_V, KV_WIDTH)

    k_ext = jnp.concatenate([kc_ref[...], k_rot], axis=0)
    v_ext = jnp.concatenate([vc_ref[...], v_new], axis=0)
    kc_ref[...] = k_rot[T - WINDOW:, :]
    vc_ref[...] = v_new[T - WINDOW:, :]

    lane_lo = lane < HEAD_DIM
    k_r64 = pltpu.roll(k_ext, HEAD_DIM, axis=1)
    v_r64 = pltpu.roll(v_ext, HEAD_DIM, axis=1)
    zero = jnp.zeros_like(k_ext)
    k_var = [(jnp.where(lane_lo, k_ext, zero).astype(_BF), jnp.where(lane_lo, zero, k_r64).astype(_BF)),
             (jnp.where(lane_lo, k_r64, zero).astype(_BF), jnp.where(lane_lo, zero, k_ext).astype(_BF))]
    v_var = [(jnp.where(lane_lo, v_ext, zero).astype(_BF), jnp.where(lane_lo, zero, v_r64).astype(_BF)),
             (jnp.where(lane_lo, v_r64, zero).astype(_BF), jnp.where(lane_lo, zero, v_ext).astype(_BF))]

    qi = lax.broadcasted_iota(jnp.int32, (2 * WINDOW, 2 * WINDOW), 0) % WINDOW
    kj = lax.broadcasted_iota(jnp.int32, (2 * WINDOW, 2 * WINDOW), 1)
    band = jnp.logical_and(kj > qi, kj <= qi + WINDOW)
    row_top = lax.broadcasted_iota(jnp.int32, (2 * WINDOW, 1), 0) < WINDOW
    neg_inf = jnp.float32(-jnp.inf)

    gate_cols = [(_OFF_GA + c, GATE_JOB_COLS) for c in range(0, 2 * D_MODEL, GATE_JOB_COLS)]
    gate_parts = []
    jobs_per_unit = -(-len(gate_cols) // ((T // WINDOW) * N_KV_HEADS))

    for n in range(T // WINDOW):
        if n == 0:
            ok = jnp.logical_and(band, jnp.logical_or(kj >= WINDOW, s > 0))
        else:
            ok = band
        r0 = n * WINDOW
        for j in range(N_KV_HEADS):
            q2 = jnp.concatenate([q_pairs[2 * j][r0:r0 + WINDOW], q_pairs[2 * j + 1][r0:r0 + WINDOW]], axis=0)
            kcat = jnp.concatenate([k_var[j][0][r0:r0 + 2 * WINDOW], k_var[j][1][r0:r0 + 2 * WINDOW]], axis=0)
            vcat = jnp.concatenate([v_var[j][0][r0:r0 + 2 * WINDOW], v_var[j][1][r0:r0 + 2 * WINDOW]], axis=0)
            sc = _dot_nt(q2, kcat)
            ps, rs = [], []
            for half in range(2):
                sh = jnp.where(ok, sc[:, 2 * WINDOW * half:2 * WINDOW * (half + 1)], neg_inf)
                snk = jnp.where(row_top, sinks_ref[GROUP * j + half], sinks_ref[GROUP * j + 2 + half])
                m = jnp.maximum(jnp.max(sh, axis=1, keepdims=True), snk)
                p = jnp.exp(sh - m)
                den = _rowsum(p) + jnp.exp(snk - m)
                ps.append(p.astype(_BF))
                rs.append(1.0 / den)
            o = _dot(jnp.concatenate(ps, axis=1), vcat)
            o = o * jnp.where(lane_lo, rs[0], rs[1])
            ya_ref[r0:r0 + WINDOW, LANES * (2 * j):LANES * (2 * j + 1)] = o[:WINDOW]
            ya_ref[r0:r0 + WINDOW, LANES * (2 * j + 1):LANES * (2 * j + 2)] = o[WINDOW:]
            for _ in range(min(jobs_per_unit, len(gate_cols) - len(gate_parts))):
                gate_parts.append(proj(*gate_cols[len(gate_parts)]))

    up_a = _dot(ya_ref[...].astype(_BF), w_ua_ref[...])

    HW = HGRN_WIDTH
    a0 = lb_ref[0:1, :]
    a1 = lb_ref[1:2, :]
    am = jnp.maximum(a0, a1)
    e0 = jnp.exp(a0 - am)
    lb = e0 / (e0 + jnp.exp(a1 - am))
    f = lb + (1.0 - lb) * jax.nn.sigmoid(proj(_OFF_HF, HW))
    kk = 1.0 - f
    g = jnp.log2(f)
    hq = proj(_OFF_HQ, HW)
    hv = proj(_OFF_HI, HW)

    row = lax.broadcasted_iota(jnp.int32, (T, 1), 0)
    C = HGRN_CHUNK

    def rdown(t, d):
        return pltpu.roll(t.reshape(T // SUBLANES, SUBLANES, HW), d, axis=1).reshape(T, HW)

    def rup(t, d):
        return pltpu.roll(t.reshape(T // SUBLANES, SUBLANES, HW), SUBLANES - d, axis=1).reshape(T, HW)

    def head(t, h):
        return t[:, HGRN_DK * h:HGRN_DK * (h + 1)]

    qk0 = hq * kk
    o_parts = [_rowsum(head(qk0, h)) * head(hv, h) for h in range(HGRN_HEADS)]

    small_levels = []
    bsz = 1
    lb_cum, lb_tot = g, g
    while bsz < SUBLANES:
        odd = (row % (2 * bsz)) >= bsz
        ex = jnp.exp2(jnp.where(odd, lb_cum, lb_tot - lb_cum))
        small_levels.append(((hq * ex).astype(_BF), (kk * ex).astype(_BF)))
        prev_tot = rdown(lb_tot, bsz)
        lb_cum = lb_cum + jnp.where(odd, prev_tot, 0.0)
        lb_tot = lb_tot + jnp.where(odd, prev_tot, rup(lb_tot, bsz))
        bsz *= 2

    cum = [lb_cum[r:r + bsz] for r in range(0, T, bsz)]
    tot = [lb_tot[r:r + 1] for r in range(0, T, bsz)]
    kk_b = kk.astype(_BF)
    big_levels = []
    while bsz < C:
        q_rows, k_rows = [], []
        for j, r in enumerate(range(0, T, bsz)):
            if j % 2:
                q_rows.append(hq[r:r + bsz] * jnp.exp2(cum[j]))
                k_rows.append(kk_b[r:r + bsz])
            else:
                k_rows.append((kk[r:r + bsz] * jnp.exp2(tot[j] - cum[j])).astype(_BF))
        big_levels.append((bsz, jnp.concatenate(q_rows, axis=0).astype(_BF), jnp.concatenate(k_rows, axis=0)))
        cum = [jnp.concatenate([cum[j], cum[j + 1] + tot[j]], axis=0) for j in range(0, len(cum), 2)]
        tot = [tot[j] + tot[j + 1] for j in range(0, len(tot), 2)]
        bsz *= 2
    hv_b = hv.astype(_BF)
    lvl = lvl_ref[...]

    for c in range(T // C):
        rs_ = slice(c * C, (c + 1) * C)
        rh_ = slice(c * C // 2, (c + 1) * C // 2)
        q_in = (hq[rs_] * jnp.exp2(cum[c])).astype(_BF)
        k_st = (kk[rs_] * jnp.exp2(tot[c] - cum[c])).astype(_BF)
        dec = jnp.exp2(tot[c])
        for h in range(HGRN_HEADS):
            cs_ = slice(HGRN_DK * h, HGRN_DK * (h + 1))
            st = st_ref[h]
            vc = hv_b[rs_, cs_]
            inter = _dot_nt(q_in[:, cs_], st.astype(_BF))
            zero_tile = jnp.zeros((SUBLANES, LANES), _F32)
            tiles = [[zero_tile for _ in range(C // LANES)] for _ in range(C // SUBLANES)]

            def place(li, p, p_row, rb, col):
                r, ct = rb * SUBLANES, col // LANES
                cs = slice(ct * LANES, (ct + 1) * LANES)
                tiles[rb][ct] = jnp.where(lvl[r:r + SUBLANES, cs] == li, p[p_row:p_row + SUBLANES, cs], tiles[rb][ct])

            for li, (ql, kl) in enumerate(small_levels):
                p = _dot_nt(ql[rs_, cs_], kl[rs_, cs_])
                for rb in range(C // SUBLANES):
                    place(li, p, rb * SUBLANES, rb, rb * SUBLANES)
            for li, (bs, ql, kl) in enumerate(big_levels, start=len(small_levels)):
                p = _dot_nt(ql[rh_, cs_], kl[rs_, cs_])
                for rb in range(C // SUBLANES):
                    blk = (rb * SUBLANES) // bs
                    if blk % 2:
                        place(li, p, rb * SUBLANES - (blk + 1) // 2 * bs, rb, (blk - 1) * bs)
            amat = jnp.concatenate([jnp.concatenate(row_tiles, axis=1) for row_tiles in tiles], axis=0)
            intra = _dot(amat.astype(_BF), vc)
            st_ref[h] = st * dec[:, cs_] + _dot_tn(vc, k_st[:, cs_])
            o_ref[rs_, cs_] = inter + intra + o_parts[h][rs_]

    o = o_ref[...]
    hg = proj(_OFF_HG, HW)
    yh_parts = []
    for h in range(HGRN_HEADS):
        oh = head(o, h)
        ms = _rowsum(oh * oh) * (1.0 / HGRN_DK)
        yh_parts.append(oh * lax.rsqrt(ms + NORM_EPS))
    yh = jnp.concatenate(yh_parts, axis=1) * hgn_ref[...]
    yh = (yh * (hg * jax.nn.sigmoid(hg))).astype(_BF)

    up_h = _dot(yh, w_uh_ref[...])
    half = len(gate_parts) // 2
    z_ga = jnp.concatenate(gate_parts[:half], axis=1)
    z_gh = jnp.concatenate(gate_parts[half:], axis=1)
    merged = jax.nn.sigmoid(z_ga) * up_a + jax.nn.sigmoid(z_gh) * up_h
    h1 = x + _dot(merged.astype(_BF), w_out_ref[...])
    h1_ref[0] = h1

    xn2 = _rms(h1, gffn_ref[...])
    xp_ref[0] = _pack_bf16_pairs(xn2)
    x_hi = xn2.astype(_BF)
    x_lo = (xn2 - x_hi.astype(_F32)).astype(_BF)
    w_rt = w_rt_ref[...]
    w_hi = w_rt.astype(_BF)
    w_lo = (w_rt - w_hi.astype(_F32)).astype(_BF)
    logits = (_dot_nt(w_hi, x_hi) + (_dot_nt(w_hi, x_lo) + _dot_nt(w_lo, x_hi))) + b_r_ref[...]
    eidx = lax.broadcasted_iota(jnp.int32, (N_EXPERTS, T), 0)
    vals, sels, ohs = [], [], []
    l = logits
    for _ in range(TOP_K):
        m = jnp.max(l, axis=0, keepdims=True)
        sel = jnp.min(jnp.where(l == m, eidx, N_EXPERTS), axis=0, keepdims=True)
        oh = eidx == sel
        vals.append(m)
        sels.append(sel)
        ohs.append(oh)
        l = jnp.where(oh, neg_inf, l)
    es = [jnp.exp(v - vals[0]) for v in vals]
    den = es[0] + es[1] + es[2] + es[3]
    chosen = jnp.zeros((N_EXPERTS, T), _F32)
    for oh in ohs:
        chosen = chosen + jnp.where(oh, 1.0, 0.0)
    before = _dot(chosen.astype(_BF), upper_ref[...]) + cnt_sc[:, 0:1]
    for k in range(TOP_K):
        idx_ref[k:k + 1, :] = sels[k]
        gate_ref[k:k + 1, :] = es[k] / den
        rank_ref[k:k + 1, :] = jnp.sum(jnp.where(ohs[k], before, 0.0), axis=0, keepdims=True).astype(jnp.int32)
    cnt_sc[...] = cnt_sc[...] + _rowsum(chosen)
    cnt_ref[...] = cnt_sc[...]


def _const_spec(shape):
    return pl.BlockSpec(shape, lambda b, s: (0,) * len(shape), pipeline_mode=pl.Buffered(1))


def _mixer_call(batch0, B, x, sinks, cos_t, sin_t, gmix, w_in, b_in, lb, hgn, lvl, upper, w_ua, w_uh, w_out, gffn, w_rt, b_r):
    S = x.shape[1]
    T = SEQ_TILE
    N = B * S
    nS = S // T
    tok_spec = pl.BlockSpec((TOP_K, T), lambda b, s: (0, b * nS + s))
    out_shape = (
        jax.ShapeDtypeStruct((B, S, D_MODEL), _F32),
        jax.ShapeDtypeStruct((B, S, D_MODEL // 2), jnp.uint32),
        jax.ShapeDtypeStruct((TOP_K, N), jnp.int32),
        jax.ShapeDtypeStruct((TOP_K, N), jnp.int32),
        jax.ShapeDtypeStruct((TOP_K, N), _F32),
        jax.ShapeDtypeStruct((N_EXPERTS, LANES), _F32),
    )
    in_specs = [
        pl.BlockSpec(memory_space=pltpu.SMEM),
        pl.BlockSpec((1, T, D_MODEL), lambda b, s: (batch0 + b, s, 0)),
        pl.BlockSpec((T, LANES), lambda b, s: (s, 0)),
        pl.BlockSpec((T, LANES), lambda b, s: (s, 0)),
        _const_spec((1, D_MODEL)),
        _const_spec((D_MODEL, IN_COLS)),
        _const_spec((1, IN_COLS)),
        _const_spec((2, HGRN_WIDTH)),
        _const_spec((1, HGRN_WIDTH)),
        _const_spec((HGRN_CHUNK, HGRN_CHUNK)),
        _const_spec((SEQ_TILE, SEQ_TILE)),
        _const_spec((ATTN_WIDTH, D_MODEL)),
        _const_spec((HGRN_WIDTH, D_MODEL)),
        _const_spec((D_MODEL, D_MODEL)),
        _const_spec((1, D_MODEL)),
        _const_spec((N_EXPERTS, D_MODEL)),
        _const_spec((N_EXPERTS, 1)),
    ]
    out_specs = (
        pl.BlockSpec((1, T, D_MODEL), lambda b, s: (b, s, 0)),
        pl.BlockSpec((1, T, D_MODEL // 2), lambda b, s: (b, s, 0)),
        tok_spec, tok_spec, tok_spec,
        pl.BlockSpec((N_EXPERTS, LANES), lambda b, s: (0, 0)),
    )
    scratch = [
        pltpu.VMEM((WINDOW, KV_WIDTH), _F32),
        pltpu.VMEM((WINDOW, KV_WIDTH), _F32),
        pltpu.VMEM((HGRN_HEADS, HGRN_DK, HGRN_DK), _F32),
        pltpu.VMEM((N_EXPERTS, LANES), _F32),
        pltpu.VMEM((T, ATTN_WIDTH), _F32),
        pltpu.VMEM((T, HGRN_WIDTH), _F32),
    ]
    return pl.pallas_call(
        _mixer_kernel,
        grid=(B, nS),
        in_specs=in_specs,
        out_specs=out_specs,
        out_shape=out_shape,
        scratch_shapes=scratch,
        compiler_params=pltpu.CompilerParams(
            dimension_semantics=("arbitrary", "arbitrary"), vmem_limit_bytes=VMEM_LIMIT),
        name="mixer",
    )(sinks, x, cos_t, sin_t, gmix, w_in, b_in, lb, hgn, lvl, upper, w_ua, w_uh, w_out, gffn, w_rt, b_r)


def _dest_kernel(start_ref, idx_ref, rank_ref, dest_ref):
    idx = idx_ref[...]
    dest = rank_ref[...]
    for e in range(N_EXPERTS):
        dest = dest + jnp.where(idx == e, start_ref[e], 0)
    dest_ref[...] = dest


def _dest_call(pad_starts, idx, rank):
    K, N = idx.shape
    spec = pl.BlockSpec((K, DEST_TILE), lambda i: (0, i))
    return pl.pallas_call(
        _dest_kernel,
        grid=(N // DEST_TILE,),
        in_specs=[pl.BlockSpec(memory_space=pltpu.SMEM), spec, spec],
        out_specs=spec,
        out_shape=jax.ShapeDtypeStruct((K, N), jnp.int32),
        compiler_params=pltpu.CompilerParams(dimension_semantics=("arbitrary",)),
        name="dest",
    )(pad_starts, idx, rank)


def _sc_mesh():
    return plsc.VectorSubcoreMesh(core_axis_name="c", subcore_axis_name="s")


def _sc_worker_id():
    return lax.axis_index("s") * SC_CORES + lax.axis_index("c")


def _sc_scatter_rows(x, dest, n_rows):
    N, C = x.shape
    K = dest.shape[0]
    n_chunks = N // SC_CHUNK
    per_worker = n_chunks // SC_WORKERS
    dest3 = dest.reshape(K, n_chunks, SC_CHUNK)

    @functools.partial(
        pl.kernel, out_type=jax.ShapeDtypeStruct((n_rows, C), x.dtype), mesh=_sc_mesh(),
        scratch_types=[pltpu.VMEM((K, SC_CHUNK), jnp.int32), pltpu.VMEM((SC_CHUNK, C), x.dtype),
                       pltpu.SemaphoreType.DMA])
    def scatter(x_hbm, i_hbm, o_hbm, idx_v, rows_v, sem):
        base = _sc_worker_id() * per_worker

        @pl.loop(0, per_worker)
        def _(j):
            c = base + j
            off = pl.multiple_of(c * SC_CHUNK, SC_CHUNK)
            for k in range(K):
                pltpu.sync_copy(i_hbm.at[k, c], idx_v.at[k])
            pltpu.sync_copy(x_hbm.at[pl.ds(off, SC_CHUNK)], rows_v)
            for k in range(K):
                pltpu.async_copy(rows_v, o_hbm.at[idx_v.at[k]], sem).wait()

    return scatter(x, dest3)


def _sc_gather_rows(table, idx):
    M = idx.shape[0]
    C = table.shape[1]
    per_worker = M // SC_WORKERS
    n_chunks = per_worker // SC_CHUNK

    @functools.partial(
        pl.kernel, out_type=jax.ShapeDtypeStruct((M, C), table.dtype), mesh=_sc_mesh(),
        scratch_types=[pltpu.VMEM((SC_CHUNK,), jnp.int32), pltpu.VMEM((SC_CHUNK, C), table.dtype),
                       pltpu.SemaphoreType.DMA])
    def gather(t_hbm, i_hbm, o_hbm, idx_v, rows_v, sem):
        base = _sc_worker_id() * per_worker

        @pl.loop(0, n_chunks)
        def _(j):
            off = pl.multiple_of(base + j * SC_CHUNK, SC_CHUNK)
            pltpu.sync_copy(i_hbm.at[pl.ds(off, SC_CHUNK)], idx_v)
            pltpu.async_copy(t_hbm.at[idx_v], rows_v, sem).wait()
            pltpu.sync_copy(rows_v, o_hbm.at[pl.ds(off, SC_CHUNK)])

    return gather(table, idx)


def _expert_kernel(blk_e_ref, nb_ref, next_ref, xs_ref, w1_hbm, b1_ref, w2_hbm, b2_ref, ys_ref,
                   w1f_ref, w2f_ref, w1b_ref, w2b_ref, act0_ref, act1_ref, asc0_ref, asc1_ref, sc_ref, sems):
    i = pl.program_id(0)
    nb = nb_ref[0]
    last = nb - 1

    def expert_of(j):
        return blk_e_ref[jnp.clip(j, 0, last)]

    def first_block(j):
        return jnp.logical_or(j == 0, expert_of(j) != expert_of(j - 1))

    def w1_copy(expert):
        return pltpu.make_async_copy(w1_hbm.at[expert], w1f_ref, sems.at[0])

    def w2_copy(expert):
        return pltpu.make_async_copy(w2_hbm.at[expert], w2f_ref, sems.at[1])

    @pl.when(i == 0)
    def _():
        w1_copy(expert_of(0)).start()
        w2_copy(expert_of(0)).start()
        act1_ref[...] = jnp.zeros_like(act1_ref)
        w2b_ref[...] = jnp.zeros_like(w2b_ref)
        sc_ref[...] = jnp.zeros_like(sc_ref)
        asc1_ref[...] = jnp.zeros_like(asc1_ref)

    @pl.when(jnp.logical_and(i < nb, first_block(i)))
    def _():
        e = expert_of(i)
        w1_copy(e).wait()
        w8, inv = _fp8_scaled(w1f_ref[...])
        w1b_ref[...] = w8
        sc_ref[0:1, :] = jnp.broadcast_to(inv, (1, LANES))
        nxt = next_ref[e]

        @pl.when(nxt >= 0)
        def _():
            w1_copy(nxt).start(priority=WEIGHT_DMA_PRIORITY)

    @pl.when(jnp.logical_and(jnp.logical_and(i >= 1, i <= nb), first_block(i - 1)))
    def _():
        e = expert_of(i - 1)
        w2_copy(e).wait()
        w8, inv = _fp8_scaled(w2f_ref[...])
        w2b_ref[...] = w8
        sc_ref[1:2, :] = jnp.broadcast_to(inv, (1, LANES))
        nxt = next_ref[e]

        @pl.when(nxt >= 0)
        def _():
            w2_copy(nxt).start(priority=WEIGHT_DMA_PRIORITY)

    def both_halves(act_w_ref, asc_w_ref, act_r_ref, asc_r_ref):
        y = _dot(act_r_ref[...], w2b_ref[...]) * (asc_r_ref[:, 0:1] * sc_ref[1:2, 0:1]) + b2_ref[0]
        ys_ref[...] = _pack_bf16_pairs(y)
        x8, inv_x = _fp8_rows(_unpack_bf16_pairs(xs_ref[...]))
        h = _dot(x8, w1b_ref[...]) * (inv_x * sc_ref[0:1, 0:1]) + b1_ref[0]
        glu = jnp.minimum(h[:, :D_EXPERT], SWIGLU_LIMIT)
        lin = jnp.clip(h[:, D_EXPERT:], -SWIGLU_LIMIT, SWIGLU_LIMIT)
        a8, inv_a = _fp8_rows(glu * jax.nn.sigmoid(SWIGLU_ALPHA * glu) * (lin + 1.0))
        act_w_ref[...] = a8
        asc_w_ref[...] = jnp.broadcast_to(inv_a, asc_w_ref.shape)

    busy = i <= nb

    @pl.when(jnp.logical_and(busy, i % 2 == 0))
    def _():
        both_halves(act0_ref, asc0_ref, act1_ref, asc1_ref)

    @pl.when(jnp.logical_and(busy, i % 2 == 1))
    def _():
        both_halves(act1_ref, asc1_ref, act0_ref, asc0_ref)

    @pl.when(jnp.logical_not(busy))
    def _():
        ys_ref[...] = jnp.zeros_like(ys_ref)


def _expert_call(blk_e, nb_used, next_expert, xs, w1, b1, w2, b2):
    n_blocks = xs.shape[0] // MOE_BLOCK
    M = MOE_BLOCK
    cur = lambda i, nb: jnp.minimum(i, nb[0] - 1)
    prev = lambda i, nb: jnp.clip(i - 1, 0, nb[0] - 1)
    grid_spec = pltpu.PrefetchScalarGridSpec(
        num_scalar_prefetch=3,
        grid=(n_blocks + 1,),
        in_specs=[
            pl.BlockSpec((M, D_MODEL // 2), lambda i, be, nb, nx: (cur(i, nb), 0)),
            pl.BlockSpec(memory_space=pl.ANY),
            pl.BlockSpec((1, 1, 2 * D_EXPERT), lambda i, be, nb, nx: (be[cur(i, nb)], 0, 0)),
            pl.BlockSpec(memory_space=pl.ANY),
            pl.BlockSpec((1, 1, D_MODEL), lambda i, be, nb, nx: (be[prev(i, nb)], 0, 0)),
        ],
        out_specs=pl.BlockSpec((M, D_MODEL // 2), lambda i, be, nb, nx: (jnp.maximum(i - 1, 0), 0)),
        scratch_shapes=[pltpu.VMEM((D_MODEL, 2 * D_EXPERT), _F32), pltpu.VMEM((D_EXPERT, D_MODEL), _F32),
                        pltpu.VMEM((D_MODEL, 2 * D_EXPERT), _F8), pltpu.VMEM((D_EXPERT, D_MODEL), _F8),
                        pltpu.VMEM((M, D_EXPERT), _F8), pltpu.VMEM((M, D_EXPERT), _F8),
                        pltpu.VMEM((M, LANES), _F32), pltpu.VMEM((M, LANES), _F32),
                        pltpu.VMEM((SUBLANES, LANES), _F32),
                        pltpu.SemaphoreType.DMA((2,))],
    )
    return pl.pallas_call(
        _expert_kernel,
        grid_spec=grid_spec,
        out_shape=jax.ShapeDtypeStruct(xs.shape, jnp.uint32),
        compiler_params=pltpu.CompilerParams(
            dimension_semantics=("arbitrary",), vmem_limit_bytes=VMEM_LIMIT),
        name="experts",
    )(blk_e, nb_used, next_expert, xs, w1, b1, w2, b2)


def _combine_kernel(h1_ref, gate_ref, gfin_ref, y0_ref, y1_ref, y2_ref, y3_ref, *out_refs):
    out_ref = out_refs[-1]
    acc = h1_ref[...]
    g = gate_ref[...]
    gates = jnp.concatenate([g, jnp.zeros_like(g)], axis=0).T
    for k, y_ref in enumerate((y0_ref, y1_ref, y2_ref, y3_ref)):
        acc = acc + gates[:, k:k + 1] * _unpack_bf16_pairs(y_ref[...])
    out_ref[...] = _rms(acc, gfin_ref[...])


def _combine_call(h1, gates, gfin, yg, out_prev, tile0, n_total):
    N = h1.shape[0]
    T = COMBINE_TILE
    nT = N // T
    y_specs = [pl.BlockSpec((T, D_MODEL // 2), functools.partial(lambda i, k: (k * nT + i, 0), k=k))
               for k in range(TOP_K)]
    in_specs = [
        pl.BlockSpec((T, D_MODEL), lambda i: (i, 0)),
        pl.BlockSpec((TOP_K, T), lambda i: (0, i)),
        pl.BlockSpec((1, D_MODEL), lambda i: (0, 0)),
    ] + y_specs
    args = [h1, gates, gfin, yg, yg, yg, yg]
    aliases = {}
    if out_prev is not None:
        in_specs.append(pl.BlockSpec(memory_space=pl.ANY))
        args.append(out_prev)
        aliases = {len(args) - 1: 0}
    return pl.pallas_call(
        _combine_kernel,
        grid=(nT,),
        in_specs=in_specs,
        out_specs=pl.BlockSpec((T, D_MODEL), lambda i: (tile0 + i, 0)),
        out_shape=jax.ShapeDtypeStruct((n_total, D_MODEL), _F32),
        input_output_aliases=aliases,
        compiler_params=pltpu.CompilerParams(dimension_semantics=("arbitrary",)),
        name="combine",
    )(*args)


def _hgrn_level_table():
    t = np.arange(HGRN_CHUNK)[:, None]
    u = np.arange(HGRN_CHUNK)[None, :]
    top = np.floor(np.log2(np.maximum(t ^ u, 1))).astype(np.int32)
    return np.where(t > u, top, -1).astype(np.int32)


def _rope_tables(S):
    half = HEAD_DIM // 2
    inv_freq = ROPE_THETA ** (-(jnp.arange(half, dtype=_F32) * 2.0 / HEAD_DIM))
    ang = jnp.arange(S, dtype=_F32)[:, None] * inv_freq[None, :]
    cos = jnp.tile(jnp.cos(ang), (1, 4))
    sin = jnp.tile(jnp.sin(ang), (1, 4))
    sign = jnp.where(jnp.arange(LANES) % HEAD_DIM < ROT_HALF, -1.0, 1.0).astype(_F32)
    return cos, sin * sign[None, :]


def kernel(x, norm_mix_g, w_in, b_in, attn_sinks, hgrn_lb, hgrn_norm_g, w_up_attn, w_up_hgrn, w_out,
           norm_ffn_g, w_router, b_router, w_moe1, b_moe1, w_moe2, b_moe2, norm_final_g):
    B, S, D = x.shape
    N = B * S
    assert D == D_MODEL and S % SEQ_TILE == 0 and SEQ_TILE % WINDOW == 0
    assert B % BATCH_PARTS == 0 and (N // BATCH_PARTS) % (SC_WORKERS * SC_CHUNK) == 0
    assert (N // BATCH_PARTS) % COMBINE_TILE == 0 and (N // BATCH_PARTS) % DEST_TILE == 0
    assert norm_mix_g.shape[0] == 1 and hgrn_lb.shape[0] == 2
    assert GROUP == 4 and N_KV_HEADS == 2 and 2 * HEAD_DIM == LANES

    cos_t, sin_t = _rope_tables(S)
    weights = (attn_sinks[0], cos_t, sin_t, norm_mix_g,
               w_in[0].astype(_BF), b_in, hgrn_lb, hgrn_norm_g, jnp.asarray(_hgrn_level_table()),
               jnp.asarray(np.triu(np.ones((SEQ_TILE, SEQ_TILE), np.float32), 1), _BF),
               w_up_attn[0].astype(_BF), w_up_hgrn[0].astype(_BF), w_out[0].astype(_BF),
               norm_ffn_g, w_router[0].T, b_router[0][:, None])

    Bp = B // BATCH_PARTS
    Np = Bp * S
    n_blocks = (Np * TOP_K) // MOE_BLOCK + N_EXPERTS
    out = None
    for part in range(BATCH_PARTS):
        h1, xp, idx, rank, gates, cnt = _mixer_call(part * Bp, Bp, x, *weights)

        counts = cnt[:, 0].astype(jnp.int32)
        padded = ((counts + MOE_BLOCK - 1) // MOE_BLOCK) * MOE_BLOCK
        pad_ends = jnp.cumsum(padded)
        pad_starts = pad_ends - padded
        dest = _dest_call(pad_starts, idx, rank)
        blk_start = jnp.arange(n_blocks, dtype=jnp.int32) * MOE_BLOCK
        blk_e = jnp.minimum(jnp.sum((pad_ends[None, :] <= blk_start[:, None]).astype(jnp.int32), axis=1),
                            N_EXPERTS - 1)
        nb_used = (pad_ends[-1:] // MOE_BLOCK).astype(jnp.int32)
        e_ids = jnp.arange(N_EXPERTS, dtype=jnp.int32)
        later = jnp.logical_and(e_ids[None, :] > e_ids[:, None], counts[None, :] > 0)
        next_expert = jnp.min(jnp.where(later, e_ids[None, :], N_EXPERTS), axis=1)
        next_expert = jnp.where(next_expert < N_EXPERTS, next_expert, -1).astype(jnp.int32)

        xs = _sc_scatter_rows(xp.reshape(Np, D // 2), dest, n_blocks * MOE_BLOCK)
        ys = _expert_call(blk_e, nb_used, next_expert, xs, w_moe1[0], b_moe1[0][:, None, :], w_moe2[0], b_moe2[0][:, None, :])
        yg = _sc_gather_rows(ys, dest.reshape(TOP_K * Np))
        out = _combine_call(h1.reshape(Np, D), gates, norm_final_g[None, :], yg, out,
                            part * (Np // COMBINE_TILE), N)
    return out.reshape(B, S, D)
```
